```python
import math
import jax, jax.numpy as jnp
from jax import lax
import numpy as np

D_MODEL = 1024
BATCH = 8
SEQ = 2048
DEPTH = 4

N_MEM = 256
HEAD_DIM = 64
RWKV_HEADS = 16
RWKV_DIM = RWKV_HEADS * HEAD_DIM
DECAY_LORA = 64
ICLR_LORA = 64
GATE_LORA = 128
RWKV_GN_EPS = 64e-5
WIN_Q_HEADS = 16
WIN_KV_HEADS = 4
WIN_GROUP = WIN_Q_HEADS // WIN_KV_HEADS
WINDOW = 128
BLOCK = 128
DIFF_HEADS = 8
DIFF_V_DIM = 2 * HEAD_DIM
MEM_HEADS = 4
MEM_WIDTH = MEM_HEADS * HEAD_DIM
NUM_BUCKETS = 32
MAX_DISTANCE = 128
N_EXPERTS = 16
CAPACITY_FACTOR = 2
D_EXPERT = 2 * D_MODEL
DN_ALPHA = (2 * DEPTH) ** 0.25
DN_BETA = (8 * DEPTH) ** -0.25
N_BRANCHES = 4

RWKV_COLS = 3 * RWKV_DIM + DECAY_LORA + ICLR_LORA + GATE_LORA
WIN_Q_COLS = WIN_Q_HEADS * HEAD_DIM
WIN_KV_COLS = WIN_KV_HEADS * HEAD_DIM
WIN_COLS = WIN_Q_COLS + 2 * WIN_KV_COLS
DIFF_QK_COLS = 2 * DIFF_HEADS * HEAD_DIM
DIFF_V_COLS = DIFF_HEADS * DIFF_V_DIM
DIFF_COLS = 2 * DIFF_QK_COLS + DIFF_V_COLS
GATE_COLS = N_BRANCHES * D_MODEL
IN_COLS = RWKV_COLS + WIN_COLS + DIFF_COLS + MEM_WIDTH + GATE_COLS
IN_SPLITS = (RWKV_COLS, RWKV_COLS + WIN_COLS, RWKV_COLS + WIN_COLS + DIFF_COLS,
             RWKV_COLS + WIN_COLS + DIFF_COLS + MEM_WIDTH)
RWKV_SPLITS = (RWKV_DIM, 2 * RWKV_DIM, 3 * RWKV_DIM, 3 * RWKV_DIM + DECAY_LORA,
               3 * RWKV_DIM + DECAY_LORA + ICLR_LORA)
BRANCH_ROWS = RWKV_DIM + WIN_Q_COLS + DIFF_V_COLS + MEM_WIDTH
BRANCH_SPLITS = (RWKV_DIM, RWKV_DIM + WIN_Q_COLS, RWKV_DIM + WIN_Q_COLS + DIFF_V_COLS)
NEG_INF = -1e30

kernel_name = 'hybrid_rwkv7_window_diffattn_ecmoe_encoder'


def layer_norm(x, g, b, eps=1e-5):
    xf = x.astype(jnp.float32)
    mu = xf.mean(-1, keepdims=True)
    var = jnp.square(xf - mu).mean(-1, keepdims=True)
    return ((xf - mu) * lax.rsqrt(var + eps) * g + b).astype(x.dtype)


def t5_bucket(rel):
    half = NUM_BUCKETS // 2
    exact = half // 2
    n = jnp.abs(rel)
    nf = jnp.maximum(n, 1).astype(jnp.float32)
    large = exact + (jnp.log(nf / exact) / math.log(MAX_DISTANCE / exact)
                     * (half - exact)).astype(jnp.int32)
    large = jnp.minimum(large, half - 1)
    return jnp.where(rel > 0, half, 0) + jnp.where(n < exact, n, large)


def centred_shift(p, mu):
    prev = jnp.pad(p[:, :-1], ((0, 0), (1, 0), (0, 0)))
    nxt = jnp.pad(p[:, 1:], ((0, 0), (0, 1), (0, 0)))
    return p + mu[0] * (prev - p) + mu[1] * (nxt - p)


def rwkv7_branch(p, mu, w0, w_up, a0, a_up, g_up, k_k, k_a, r_k, gn_g, gn_b):
    B, S, _ = p.shape
    H, N = RWKV_HEADS, HEAD_DIM
    f32 = jnp.float32
    p = centred_shift(p, mu)
    r, k, v, dw, da, dg = jnp.split(p, RWKV_SPLITS, axis=-1)
    w_logit = (w0[:, None, None, :] + jnp.einsum('bsr,zrc->zbsc', jnp.tanh(dw), w_up)).astype(f32)
    decay = jnp.exp(-jnp.exp(-jax.nn.softplus(-w_logit) - 0.5))
    a = jax.nn.sigmoid((a0[:, None, None, :] + jnp.einsum('bsr,zrc->zbsc', da, a_up)).astype(f32))
    g = jnp.einsum('bsr,rc->bsc', jax.nn.sigmoid(dg), g_up).astype(f32)
    kk = (k * k_k).astype(f32).reshape(B, S, H, N)
    kk = kk * lax.rsqrt(jnp.maximum(jnp.sum(kk * kk, -1, keepdims=True), 1e-12))
    kk = kk.reshape(B, S, RWKV_DIM)
    kd = k.astype(f32)[None] * (1.0 + (a - 1.0) * k_a.astype(f32))
    rf, vf = r.astype(f32), v.astype(f32)

    def both(z):
        return jnp.stack([z, z[:, ::-1]])

    def rev1(z):
        return jnp.stack([z[0], z[1, :, ::-1]])

    def to_t(z):
        return jnp.moveaxis(z.reshape(2, B, S, H, N), 2, 0)

    kk2 = both(kk)
    xs = (to_t(both(rf)), to_t(rev1(decay)), to_t(rev1(kd)), to_t(both(vf)),
          to_t(kk2), to_t(rev1(a) * kk2))

    def step(state, inp):
        r_t, w_t, k_t, v_t, kk_t, kka_t = inp
        sa = jnp.einsum('zbhvk,zbhk->zbhv', state, kk_t)
        state = (state * w_t[..., None, :] - sa[..., None] * kka_t[..., None, :]
                 + v_t[..., None] * k_t[..., None, :])
        return state, jnp.einsum('zbhvk,zbhk->zbhv', state, r_t)

    state0 = jnp.zeros((2, B, H, N, N), f32)
    _, ys = lax.scan(step, state0, xs)
    ys = jnp.moveaxis(ys, 0, 2)
    y = ys[0] + ys[1, :, ::-1]
    mu_y = y.mean(-1, keepdims=True)
    var_y = jnp.square(y - mu_y).mean(-1, keepdims=True)
    y = ((y - mu_y) * lax.rsqrt(var_y + RWKV_GN_EPS)).reshape(B, S, RWKV_DIM) * gn_g + gn_b
    r_h = rf.reshape(B, S, H, N)
    bonus = jnp.sum(r_h * (kd[0] + kd[1]).reshape(B, S, H, N) * r_k.astype(f32), -1, keepdims=True) \
        * vf.reshape(B, S, H, N)
    return ((y + bonus.reshape(B, S, RWKV_DIM)) * g).astype(p.dtype)


def window_branch(p, sink, rel_bias):
    B, S, _ = p.shape
    nb = S // BLOCK
    Hkv, G, dh = WIN_KV_HEADS, WIN_GROUP, HEAD_DIM
    q = p[..., :WIN_Q_COLS].reshape(B, nb, BLOCK, Hkv, G, dh)
    k = p[..., WIN_Q_COLS:WIN_Q_COLS + WIN_KV_COLS].reshape(B, S, Hkv, dh)
    v = p[..., WIN_Q_COLS + WIN_KV_COLS:].reshape(B, S, Hkv, dh)

    def band(z):
        zp = jnp.pad(z, ((0, 0), (BLOCK, BLOCK), (0, 0), (0, 0))).reshape(B, nb + 2, BLOCK, Hkv, dh)
        return jnp.concatenate([zp[:, :-2], zp[:, 1:-1], zp[:, 2:]], axis=2)

    kb, vb = band(k), band(v)
    s = jnp.einsum('bnqhgd,bnkhd->bnhgqk', q, kb).astype(jnp.float32) * (dh ** -0.5)
    qi = jnp.arange(BLOCK)[:, None]
    kj = jnp.arange(3 * BLOCK)[None, :]
    rel = kj - BLOCK - qi
    bias = rel_bias[t5_bucket(rel)][..., :WIN_Q_HEADS].astype(jnp.float32)
    bias = bias.transpose(2, 0, 1).reshape(Hkv, G, BLOCK, 3 * BLOCK)
    kpos = jnp.arange(nb)[:, None] * BLOCK + kj - BLOCK
    valid = (jnp.abs(rel) <= WINDOW)[None] & ((kpos >= 0) & (kpos < S))[:, None, :]
    s = jnp.where(valid[None, :, None, None], s + bias, NEG_INF)
    sink_b = sink.astype(jnp.float32).reshape(Hkv, G)[..., None, None]
    m = jnp.maximum(s.max(-1, keepdims=True), sink_b)
    e = jnp.exp(s - m)
    pr = (e / (e.sum(-1, keepdims=True) + jnp.exp(sink_b - m))).astype(v.dtype)
    o = jnp.einsum('bnhgqk,bnkhd->bnqhgd', pr, vb)
    return o.reshape(B, S, WIN_Q_COLS)


def diff_branch(p, lam, subln_g, rel_bias, lambda_init):
    B, S, _ = p.shape
    H, d = DIFF_HEADS, HEAD_DIM
    nb = S // BLOCK
    q = p[..., :DIFF_QK_COLS].reshape(B, S, H, 2, d)
    k = p[..., DIFF_QK_COLS:2 * DIFF_QK_COLS].reshape(B, S, H, 2, d)
    v = p[..., 2 * DIFF_QK_COLS:].reshape(B, S, H, DIFF_V_DIM)
    lamf = lam.astype(jnp.float32)
    lam_full = (jnp.exp(jnp.sum(lamf[0] * lamf[1])) - jnp.exp(jnp.sum(lamf[2] * lamf[3]))
                + lambda_init)
    bias_tab = rel_bias[:, WIN_Q_HEADS:].astype(jnp.float32)
    kpos = jnp.arange(S)
    qb = q.reshape(B, nb, BLOCK, H, 2, d).transpose(1, 0, 2, 3, 4, 5)

    def one_block(args):
        q_blk, i0 = args
        s = jnp.einsum('bqhcd,bkhcd->bhcqk', q_blk, k).astype(jnp.float32) * (d ** -0.5)
        rel = kpos[None, :] - (i0 + jnp.arange(BLOCK))[:, None]
        bias = bias_tab[t5_bucket(rel)].transpose(2, 0, 1)
        a = jax.nn.softmax(s + bias[None, :, None], axis=-1)
        attn = a[:, :, 0] - lam_full * a[:, :, 1]
        return jnp.einsum('bhqk,bkhe->bqhe', attn.astype(v.dtype), v)

    o = lax.map(one_block, (qb, jnp.arange(nb) * BLOCK))
    o = o.transpose(1, 0, 2, 3, 4).reshape(B, S, H, DIFF_V_DIM).astype(jnp.float32)
    o = o * lax.rsqrt(jnp.mean(o * o, -1, keepdims=True) + 1e-5) * subln_g * (1.0 - lambda_init)
    return o.reshape(B, S, DIFF_V_COLS).astype(p.dtype)


def memory_branch(q_cols, mem, w_kv):
    B, S, _ = q_cols.shape
    q = q_cols.reshape(B, S, MEM_HEADS, HEAD_DIM)
    kv = jnp.einsum('bmd,dc->bmc', mem, w_kv)
    k = kv[..., :MEM_WIDTH].reshape(B, N_MEM, MEM_HEADS, HEAD_DIM)
    v = kv[..., MEM_WIDTH:].reshape(B, N_MEM, MEM_HEADS, HEAD_DIM)
    s = jnp.einsum('bshd,bmhd->bhsm', q, k).astype(jnp.float32) * (HEAD_DIM ** -0.5)
    a = jax.nn.softmax(s, axis=-1).astype(v.dtype)
    return jnp.einsum('bhsm,bmhd->bshd', a, v).reshape(B, S, MEM_WIDTH)


def expert_choice_ffn(x, router, w_gate, w_up, w_down):
    B, S, D = x.shape
    cap = CAPACITY_FACTOR * S // N_EXPERTS
    aff = jax.nn.softmax(jnp.einsum('bsd,de->bse', x, router).astype(jnp.float32), axis=-1)
    gate, idx = lax.top_k(aff.transpose(0, 2, 1), cap)
    xe = jax.vmap(lambda xb, ib: xb[ib])(x, idx)
    h = jax.nn.silu(jnp.einsum('becd,edf->becf', xe, w_gate)) * jnp.einsum('becd,edf->becf', xe, w_up)
    ye = jnp.einsum('becf,efd->becd', h, w_down) * gate[..., None].astype(x.dtype)
    flat = (idx + jnp.arange(B)[:, None, None] * S).reshape(-1)
    y = jnp.zeros((B * S, D), x.dtype).at[flat].add(ye.reshape(-1, D))
    return y.reshape(B, S, D)


def setup_inputs(seed: int = 0) -> dict:
    key = jax.random.key(seed)
    ks = jax.random.split(key, 32)
    L, D = DEPTH, D_MODEL

    def nrm(k, shape, s):
        return jax.random.normal(k, shape, jnp.float32) * s

    return {
        'x': nrm(ks[0], (BATCH, SEQ, D), 1.0),
        'mem': nrm(ks[1], (BATCH, N_MEM, D), 1.0),
        'rel_bias': nrm(ks[2], (NUM_BUCKETS, WIN_Q_HEADS + DIFF_HEADS), 0.5),
        'w_in': nrm(ks[3], (L, D, IN_COLS), D ** -0.5),
        'rwkv_mu': jax.random.uniform(ks[4], (L, 2, RWKV_COLS), jnp.float32, 0.0, 0.5),
        'rwkv_w0': jax.random.uniform(ks[5], (L, 2, RWKV_DIM), jnp.float32, -6.0, 1.0),
        'rwkv_w_up': nrm(ks[6], (L, 2, DECAY_LORA, RWKV_DIM), 0.1),
        'rwkv_a0': nrm(ks[7], (L, 2, RWKV_DIM), 0.5),
        'rwkv_a_up': nrm(ks[8], (L, 2, ICLR_LORA, RWKV_DIM), ICLR_LORA ** -0.5),
        'rwkv_g_up': nrm(ks[9], (L, GATE_LORA, RWKV_DIM), GATE_LORA ** -0.5),
        'rwkv_k_k': 0.85 + nrm(ks[10], (L, RWKV_DIM), 0.05),
        'rwkv_k_a': 1.0 + nrm(ks[11], (L, RWKV_DIM), 0.05),
        'rwkv_r_k': nrm(ks[12], (L, RWKV_HEADS, HEAD_DIM), 0.1),
        'rwkv_gn_g': 1.0 + nrm(ks[13], (L, RWKV_DIM), 0.1),
        'rwkv_gn_b': nrm(ks[14], (L, RWKV_DIM), 0.01),
        'win_sink': nrm(ks[15], (L, WIN_Q_HEADS), 0.5),
        'diff_lambda': nrm(ks[16], (L, 4, HEAD_DIM), 0.1),
        'diff_subln_g': 1.0 + nrm(ks[17], (L, DIFF_V_DIM), 0.1),
        'mem_w_kv': nrm(ks[18], (L, D, 2 * MEM_WIDTH), D ** -0.5),
        'w_branch': nrm(ks[19], (L, BRANCH_ROWS, D), (RWKV_DIM ** -0.5) * DN_BETA),
        'w_out': nrm(ks[20], (L, D, D), (D ** -0.5) * DN_BETA),
        'ln1_g': 1.0 + nrm(ks[21], (L, D), 0.1),
        'ln1_b': nrm(ks[22], (L, D), 0.01),
        'router': nrm(ks[23], (L, D, N_EXPERTS), D ** -0.5),
        'exp_w_gate': nrm(ks[24], (L, N_EXPERTS, D, D_EXPERT), (D ** -0.5) * DN_BETA),
        'exp_w_up': nrm(ks[25], (L, N_EXPERTS, D, D_EXPERT), (D ** -0.5) * DN_BETA),
        'exp_w_down': nrm(ks[26], (L, N_EXPERTS, D_EXPERT, D), (D_EXPERT ** -0.5) * DN_BETA),
        'ln2_g': 1.0 + nrm(ks[27], (L, D), 0.1),
        'ln2_b': nrm(ks[28], (L, D), 0.01),
    }


def reference(x, mem, rel_bias, w_in, rwkv_mu, rwkv_w0, rwkv_w_up, rwkv_a0, rwkv_a_up,
              rwkv_g_up, rwkv_k_k, rwkv_k_a, rwkv_r_k, rwkv_gn_g, rwkv_gn_b, win_sink,
              diff_lambda, diff_subln_g, mem_w_kv, w_branch, w_out, ln1_g, ln1_b, router,
              exp_w_gate, exp_w_up, exp_w_down, ln2_g, ln2_b):
    B, S, D = x.shape
    for l in range(DEPTH):
        lambda_init = 0.8 - 0.6 * math.exp(-0.3 * l)
        p = jnp.einsum('bsd,dc->bsc', x, w_in[l])
        p_rwkv, p_win, p_diff, p_mem, p_gate = jnp.split(p, IN_SPLITS, axis=-1)
        y_rwkv = rwkv7_branch(p_rwkv, rwkv_mu[l], rwkv_w0[l], rwkv_w_up[l], rwkv_a0[l],
                              rwkv_a_up[l], rwkv_g_up[l], rwkv_k_k[l], rwkv_k_a[l],
                              rwkv_r_k[l], rwkv_gn_g[l], rwkv_gn_b[l])
        y_win = window_branch(p_win, win_sink[l], rel_bias)
        y_diff = diff_branch(p_diff, diff_lambda[l], diff_subln_g[l], rel_bias, lambda_init)
        y_mem = memory_branch(p_mem, mem, mem_w_kv[l])
        wb_rwkv, wb_win, wb_diff, wb_mem = jnp.split(w_branch[l], BRANCH_SPLITS, axis=0)
        gates = jax.nn.sigmoid(p_gate.reshape(B, S, N_BRANCHES, D))
        merged = (gates[:, :, 0] * (y_rwkv @ wb_rwkv) + gates[:, :, 1] * (y_win @ wb_win)
                  + gates[:, :, 2] * (y_diff @ wb_diff) + gates[:, :, 3] * (y_mem @ wb_mem))
        x = layer_norm(DN_ALPHA * x + merged @ w_out[l], ln1_g[l], ln1_b[l])
        ffn = expert_choice_ffn(x, router[l], exp_w_gate[l], exp_w_up[l], exp_w_down[l])
        x = layer_norm(DN_ALPHA * x + ffn, ln2_g[l], ln2_b[l])
    return x
```

```python
import functools
import math

import jax
import jax.numpy as jnp
from jax import lax
from jax.experimental import pallas as pl
from jax.experimental.pallas import tpu as pltpu

F32 = jnp.float32
BF16 = jnp.bfloat16

HEAD_DIM = 64
LANES = 128
RWKV_HEADS = 16
RWKV_DIM = RWKV_HEADS * HEAD_DIM
DECAY_LORA = 64
ICLR_LORA = 64
GATE_LORA = 128
RWKV_GN_EPS = 64e-5
RWKV_COLS = 3 * RWKV_DIM + DECAY_LORA + ICLR_LORA + GATE_LORA
RWKV_CHUNK = 64
WIN_Q_HEADS = 16
WIN_KV_HEADS = 4
WIN_GROUP = WIN_Q_HEADS // WIN_KV_HEADS
WIN_Q_COLS = WIN_Q_HEADS * HEAD_DIM
WIN_KV_COLS = WIN_KV_HEADS * HEAD_DIM
WIN_COLS = WIN_Q_COLS + 2 * WIN_KV_COLS
WINDOW = 128
BLOCK = 128
DIFF_HEADS = 8
DIFF_V_DIM = 2 * HEAD_DIM
DIFF_QK_COLS = 2 * DIFF_HEADS * HEAD_DIM
DIFF_V_COLS = DIFF_HEADS * DIFF_V_DIM
DIFF_COLS = 2 * DIFF_QK_COLS + DIFF_V_COLS
MEM_HEADS = 4
MEM_WIDTH = MEM_HEADS * HEAD_DIM
NUM_BUCKETS = 32
N_EXPERTS = 16
CAPACITY_FACTOR = 2
N_BRANCHES = 4
NEG_INF = -1e30
LN_EPS = 1e-5
VMEM_LIMIT = 56 * 1024 * 1024

OFF_WIN = RWKV_COLS
OFF_DIFF = OFF_WIN + WIN_COLS
OFF_MEM = OFF_DIFF + DIFF_COLS
OFF_GATE = OFF_MEM + MEM_WIDTH


def _cparams(*sem):
    return pltpu.CompilerParams(dimension_semantics=sem, vmem_limit_bytes=VMEM_LIMIT)


def _bdot(a, b):
    return jnp.dot(a.astype(BF16), b.astype(BF16), preferred_element_type=F32)


def _bdot_nt(a, b):
    return lax.dot_general(a.astype(BF16), b.astype(BF16), (((1,), (1,)), ((), ())),
                           preferred_element_type=F32)


def _split_dot(a, b_exact):
    hi = a.astype(BF16)
    lo = (a - hi.astype(F32)).astype(BF16)
    return (jnp.dot(hi, b_exact, preferred_element_type=F32)
            + jnp.dot(lo, b_exact, preferred_element_type=F32))


def _sigmoid(x):
    return 1.0 / (1.0 + jnp.exp(-x))


def _layer_norm(x, g, b):
    mu = jnp.mean(x, axis=-1, keepdims=True)
    xc = x - mu
    var = jnp.mean(xc * xc, axis=-1, keepdims=True)
    return xc * lax.rsqrt(var + LN_EPS) * g + b


def _mm_kernel(x_ref, w_ref, o_ref):
    o_ref[...] = jnp.dot(x_ref[...], w_ref[...], preferred_element_type=F32).astype(o_ref.dtype)


def _matmul(x, w, tm, tn, out_dtype):
    m, k = x.shape
    n = w.shape[1]
    return pl.pallas_call(
        _mm_kernel,
        grid=(n // tn, m // tm),
        in_specs=[pl.BlockSpec((tm, k), lambda j, i: (i, 0)),
                  pl.BlockSpec((k, tn), lambda j, i: (0, j))],
        out_specs=pl.BlockSpec((tm, tn), lambda j, i: (i, j)),
        out_shape=jax.ShapeDtypeStruct((m, n), out_dtype),
        compiler_params=_cparams("arbitrary", "arbitrary"),
        name="in_proj",
    )(x, w)


def _bias_kernel(tab_ref, o_ref):
    h = pl.program_id(0)
    qi = lax.broadcasted_iota(jnp.int32, (BLOCK, 3 * BLOCK), 0)
    kj = lax.broadcasted_iota(jnp.int32, (BLOCK, 3 * BLOCK), 1)
    rel = kj - BLOCK - qi
    n = jnp.abs(rel)
    n2 = n * n
    large = jnp.full_like(n, NUM_BUCKETS // 4)
    for kpow in range(1, 8):
        large = large + jnp.where(n2 >= 64 * (2 ** kpow), 1, 0)
    bucket = jnp.where(rel > 0, NUM_BUCKETS // 2, 0) + jnp.where(n < NUM_BUCKETS // 4, n, large)
    acc = jnp.zeros((BLOCK, 3 * BLOCK), F32)
    for bkt in range(NUM_BUCKETS):
        acc = jnp.where(bucket == bkt, tab_ref[h, bkt], acc)
    o_ref[0] = acc


def _bias_tiles(rel_bias_t):
    nh = rel_bias_t.shape[0]
    return pl.pallas_call(
        _bias_kernel,
        grid=(nh,),
        in_specs=[pl.BlockSpec(memory_space=pltpu.SMEM)],
        out_specs=pl.BlockSpec((1, BLOCK, 3 * BLOCK), lambda h: (h, 0, 0)),
        out_shape=jax.ShapeDtypeStruct((nh, BLOCK, 3 * BLOCK), F32),
        compiler_params=_cparams("arbitrary"),
        name="t5_bias_tiles",
    )(rel_bias_t)


def _rwkv_kernel(pr_ref, pk_ref, pv_ref, pl_ref, mur_ref, muk_ref, muv_ref, mul_ref,
                 w0_ref, wup_ref, a0_ref, aup_ref, gup_ref, kk_ref, ka_ref, rk_ref,
                 gng_ref, gnb_ref, o_ref,
                 kt_s, rt_s, kh_s, bh_s, pt_s, v_s, y_s, h_s, *, seq):
    S = seq
    C = RWKV_CHUNK
    nchunk = S // C
    row = lax.broadcasted_iota(jnp.int32, (S, 1), 0)
    tmod = row % C
    lane = lax.broadcasted_iota(jnp.int32, (1, LANES), 1)
    head0 = lane < HEAD_DIM
    ri = lax.broadcasted_iota(jnp.int32, (LANES, LANES), 0)
    ci = lax.broadcasted_iota(jnp.int32, (LANES, LANES), 1)
    same_head = (ri < HEAD_DIM) == (ci < HEAD_DIM)
    eye = ri == ci
    seg_ones = jnp.where(same_head, 1.0, 0.0).astype(BF16)

    def shift(x, mu_ref):
        prev = jnp.where(row == 0, 0.0, pltpu.roll(x, 1, 0))
        nxt = jnp.where(row == S - 1, 0.0, pltpu.roll(x, S - 1, 0))
        return x + mu_ref[0:1, :] * (prev - x) + mu_ref[1:2, :] * (nxt - x)

    r = shift(pr_ref[0].astype(F32), mur_ref)
    k = shift(pk_ref[0].astype(F32), muk_ref)
    v = shift(pv_ref[0].astype(F32), muv_ref)
    lo = shift(pl_ref[0].astype(F32), mul_ref)
    l1 = lo[:, :LANES]
    l2 = lo[:, LANES:]
    g = _bdot(_sigmoid(l2), gup_ref[...])
    kkr = k * kk_ref[...]
    kk = kkr * lax.rsqrt(jnp.maximum(_split_dot(kkr * kkr, seg_ones), 1e-12))
    v_s[...] = v
    tanh_l1 = jnp.tanh(l1)
    kd_sum = jnp.zeros((S, LANES), F32)

    for z in range(2):
        wl = w0_ref[z:z + 1, :] + _bdot(tanh_l1, wup_ref[z])
        logw = -math.exp(-0.5) * _sigmoid(wl)
        a = _sigmoid(a0_ref[z:z + 1, :] + _bdot(l1, aup_ref[z]))
        kd = k * (1.0 + (a - 1.0) * ka_ref[...])
        bb = a * kk
        kd_sum = kd_sum + kd
        cum = logw
        sh = 1
        while sh < C:
            if z == 0:
                cum = cum + jnp.where(tmod >= sh, pltpu.roll(cum, sh, 0), 0.0)
            else:
                cum = cum + jnp.where(tmod < C - sh, pltpu.roll(cum, S - sh, 0), 0.0)
            sh *= 2
        tot = cum
        sh = 1
        while sh < C:
            if z == 0:
                tot = jnp.where(tmod < C - sh, pltpu.roll(tot, S - sh, 0), tot)
            else:
                tot = jnp.where(tmod >= sh, pltpu.roll(tot, sh, 0), tot)
            sh *= 2
        e_out = jnp.exp(-cum)
        kt_s[z] = kk * jnp.exp(cum - logw)
        rt_s[z] = r * jnp.exp(cum)
        kh_s[z] = kd * e_out
        bh_s[z] = bb * e_out
        pt_s[z] = jnp.exp(tot)
        h_s[z] = jnp.zeros((LANES, LANES), F32)

    y_s[...] = jnp.zeros((S, LANES), F32)
    ti = lax.broadcasted_iota(jnp.int32, (C, C), 0)
    tj = lax.broadcasted_iota(jnp.int32, (C, C), 1)

    def chunk_dir(z, c):
        rows = pl.ds(pl.multiple_of(c * C, C), C)
        ktc = kt_s[z, rows, :]
        rtc = rt_s[z, rows, :]
        khc = kh_s[z, rows, :]
        bhc = bh_s[z, rows, :]
        pt = pt_s[z, rows, :][0:1, :]
        vc = v_s[rows, :]
        strict = (tj < ti) if z == 0 else (tj > ti)
        incl = (tj <= ti) if z == 0 else (tj >= ti)
        lhs = jnp.concatenate([ktc, rtc], axis=0)
        wt_h, u_h, rr_h, yl_h = [], [], [], []
        for hh in range(2):
            lm = jnp.where(head0, lhs, 0.0) if hh == 0 else jnp.where(head0, 0.0, lhs)
            gb = _bdot_nt(lm, bhc)
            gk = _bdot_nt(lm, khc)
            a_kb = jnp.where(strict, gb[:C], 0.0)
            a_rb = jnp.where(incl, gb[C:], 0.0)
            a_kk = jnp.where(strict, gk[:C], 0.0)
            a_rk = jnp.where(incl, gk[C:], 0.0)
            yy = -a_kb
            x = yy
            for _ in range(5):
                yy = _bdot(yy, yy)
                x = x + yy + _bdot(x, yy)
            av = _bdot(a_kk, vc)
            zz = jnp.concatenate([ktc, av], axis=1)
            tz = zz + _bdot(x, zz)
            cor = _bdot(a_rb, tz)
            wt_h.append(tz[:, :LANES])
            u_h.append(tz[:, LANES:])
            rr_h.append(rtc - cor[:, :LANES])
            yl_h.append(_bdot(a_rk, vc) - cor[:, LANES:])
        wt = jnp.where(head0, wt_h[0], wt_h[1])
        u = jnp.where(head0, u_h[0], u_h[1])
        rr = jnp.where(head0, rr_h[0], rr_h[1])
        yl = jnp.where(head0, yl_h[0], yl_h[1])
        wu = jnp.concatenate([wt, u], axis=1)
        bw = _bdot((bhc * pt).T, wu)
        kv = _bdot((khc * pt).T, vc)
        m_c = jnp.where(eye, jnp.broadcast_to(pt, (LANES, LANES)), 0.0) \
            - jnp.where(same_head, bw[:, :LANES], 0.0)
        n_c = jnp.where(same_head, kv - bw[:, LANES:], 0.0)
        h_in = h_s[z]
        y_s[rows, :] = y_s[rows, :] + _bdot(rr, h_in) + yl
        h_s[z] = _bdot(m_c, h_in) + n_c

    def body(c, carry):
        chunk_dir(0, c)
        chunk_dir(1, nchunk - 1 - c)
        return carry

    lax.fori_loop(0, nchunk, body, 0)

    y = y_s[...]
    inv_n = 1.0 / HEAD_DIM
    mu_y = _split_dot(y, seg_ones) * inv_n
    yc = y - mu_y
    var_y = _split_dot(yc * yc, seg_ones) * inv_n
    yn = yc * lax.rsqrt(var_y + RWKV_GN_EPS) * gng_ref[...] + gnb_ref[...]
    bonus = _split_dot(r * kd_sum * rk_ref[...], seg_ones) * v
    o_ref[0] = ((yn + bonus) * g).astype(o_ref.dtype)


def _rwkv_branch(p, mu, w0, w_up, a0, a_up, g_up, k_k, k_a, r_k, gn_g, gn_b, out_dtype):
    B, S, _ = p.shape
    npair = RWKV_DIM // LANES
    zpad = jnp.zeros((2, LANES - DECAY_LORA, RWKV_DIM), F32)
    wup_p = jnp.concatenate([w_up, zpad], axis=1).astype(BF16)
    aup_p = jnp.concatenate([zpad, a_up], axis=1).astype(BF16)
    row = lambda a: a.reshape(1, RWKV_DIM)
    lora_blk = 3 * RWKV_DIM // (2 * LANES)

    def pspec(off):
        return pl.BlockSpec((1, S, LANES), lambda b, j: (b, 0, off + j))

    def vspec(rows, off=0):
        return pl.BlockSpec((rows, LANES), lambda b, j: (0, off + j))

    scratch = [pltpu.VMEM((2, S, LANES), F32) for _ in range(5)] + [
        pltpu.VMEM((S, LANES), F32), pltpu.VMEM((S, LANES), F32), pltpu.VMEM((2, LANES, LANES), F32)]
    return pl.pallas_call(
        functools.partial(_rwkv_kernel, seq=S),
        grid=(B, npair),
        in_specs=[pspec(0), pspec(npair), pspec(2 * npair),
                  pl.BlockSpec((1, S, 2 * LANES), lambda b, j: (b, 0, lora_blk)),
                  vspec(2, 0), vspec(2, npair), vspec(2, 2 * npair),
                  pl.BlockSpec((2, 2 * LANES), lambda b, j: (0, lora_blk)),
                  vspec(2),
                  pl.BlockSpec((2, LANES, LANES), lambda b, j: (0, 0, j)),
                  vspec(2),
                  pl.BlockSpec((2, LANES, LANES), lambda b, j: (0, 0, j)),
                  vspec(GATE_LORA), vspec(1), vspec(1), vspec(1), vspec(1), vspec(1)],
        out_specs=pl.BlockSpec((1, S, LANES), lambda b, j: (b, 0, j)),
        out_shape=jax.ShapeDtypeStruct((B, S, RWKV_DIM), out_dtype),
        scratch_shapes=scratch,
        compiler_params=_cparams("arbitrary", "arbitrary"),
        name="rwkv7_scan",
    )(p, p, p, p, mu, mu, mu, mu, w0, wup_p, a0, aup_p, g_up.astype(BF16),
      row(k_k), row(k_a), row(r_k), row(gn_g), row(gn_b))


def _win_kernel(q_ref, k_ref, v_ref, bias_ref, sink_ref, o_ref, kp_s, vp_s, *, seq):
    S = seq
    G = WIN_GROUP
    hk = pl.program_id(1)
    odd = (hk % 2) == 1
    kfull = k_ref[0]
    vfull = v_ref[0]
    zpad = jnp.zeros((BLOCK, HEAD_DIM), kp_s.dtype)
    kp_s[0:BLOCK, :] = zpad
    kp_s[BLOCK + S:, :] = zpad
    vp_s[0:BLOCK, :] = zpad
    vp_s[BLOCK + S:, :] = zpad
    kp_s[BLOCK:BLOCK + S, :] = jnp.where(odd, kfull[:, HEAD_DIM:], kfull[:, :HEAD_DIM]).astype(kp_s.dtype)
    vp_s[BLOCK:BLOCK + S, :] = jnp.where(odd, vfull[:, HEAD_DIM:], vfull[:, :HEAD_DIM]).astype(vp_s.dtype)
    bias4 = bias_ref[...].reshape(G * BLOCK, 3 * BLOCK)
    rowi = lax.broadcasted_iota(jnp.int32, (G * BLOCK, 3 * BLOCK), 0)
    kj = lax.broadcasted_iota(jnp.int32, (G * BLOCK, 3 * BLOCK), 1)
    rel = kj - BLOCK - (rowi % BLOCK)
    in_band = jnp.abs(rel) <= WINDOW
    rgrp = lax.broadcasted_iota(jnp.int32, (G * BLOCK, 1), 0) // BLOCK
    sink4 = jnp.zeros((G * BLOCK, 1), F32)
    for gi in range(G):
        sink4 = jnp.where(rgrp == gi, sink_ref[hk * G + gi], sink4)

    def body(n, carry):
        r0 = pl.multiple_of(n * BLOCK, BLOCK)
        qb = q_ref[0, pl.ds(r0, BLOCK), :]
        q4 = jnp.concatenate([qb[:, gi * HEAD_DIM:(gi + 1) * HEAD_DIM] for gi in range(G)], axis=0)
        kb = kp_s[pl.ds(r0, 3 * BLOCK), :]
        vb = vp_s[pl.ds(r0, 3 * BLOCK), :]
        s = _bdot_nt(q4, kb) * (HEAD_DIM ** -0.5)
        kpos = r0 + kj - BLOCK
        valid = in_band & (kpos >= 0) & (kpos < S)
        s = jnp.where(valid, s + bias4, NEG_INF)
        m = jnp.maximum(jnp.max(s, axis=-1, keepdims=True), sink4)
        e = jnp.exp(s - m)
        den = jnp.sum(e, axis=-1, keepdims=True) + jnp.exp(sink4 - m)
        o4 = _bdot(e / den, vb)
        o_ref[0, pl.ds(r0, BLOCK), :] = jnp.concatenate(
            [o4[gi * BLOCK:(gi + 1) * BLOCK] for gi in range(G)], axis=1).astype(o_ref.dtype)
        return carry

    lax.fori_loop(0, S // BLOCK, body, 0)


def _window_branch(p, bias_tiles, sink, out_dtype):
    B, S, _ = p.shape
    gw = WIN_GROUP * HEAD_DIM
    qoff = OFF_WIN // gw
    koff = (OFF_WIN + WIN_Q_COLS) // LANES
    voff = (OFF_WIN + WIN_Q_COLS + WIN_KV_COLS) // LANES
    return pl.pallas_call(
        functools.partial(_win_kernel, seq=S),
        grid=(B, WIN_KV_HEADS),
        in_specs=[pl.BlockSpec((1, S, gw), lambda b, h: (b, 0, qoff + h)),
                  pl.BlockSpec((1, S, LANES), lambda b, h: (b, 0, koff + h // 2)),
                  pl.BlockSpec((1, S, LANES), lambda b, h: (b, 0, voff + h // 2)),
                  pl.BlockSpec((WIN_GROUP, BLOCK, 3 * BLOCK), lambda b, h: (h, 0, 0)),
                  pl.BlockSpec(memory_space=pltpu.SMEM)],
        out_specs=pl.BlockSpec((1, S, gw), lambda b, h: (b, 0, h)),
        out_shape=jax.ShapeDtypeStruct((B, S, WIN_Q_COLS), out_dtype),
        scratch_shapes=[pltpu.VMEM((S + 2 * BLOCK, HEAD_DIM), F32),
                        pltpu.VMEM((S + 2 * BLOCK, HEAD_DIM), F32)],
        compiler_params=_cparams("arbitrary", "arbitrary"),
        name="window_attn",
    )(p, p, p, bias_tiles, sink)


def _diff_kernel(q_ref, k_ref, v_ref, band_ref, tab_ref, lam_ref, g_ref, o_ref, b_s, *,
                 seq, lambda_init):
    S = seq
    h = pl.program_id(1)
    i = pl.program_id(2)
    lane = lax.broadcasted_iota(jnp.int32, (1, LANES), 1)
    c0 = lane < HEAD_DIM
    q = q_ref[0].astype(F32) * (HEAD_DIM ** -0.5)
    kmat = k_ref[0]
    cp = lax.broadcasted_iota(jnp.int32, (BLOCK, S + 2 * BLOCK), 1)
    far_l = tab_ref[WIN_Q_HEADS + h, NUM_BUCKETS // 2 - 1]
    far_r = tab_ref[WIN_Q_HEADS + h, NUM_BUCKETS - 1]
    b_s[...] = jnp.where(cp < i * BLOCK + BLOCK, far_l, far_r)
    c_lo = pl.multiple_of(i * BLOCK, BLOCK)
    b_s[:, pl.ds(c_lo, 3 * BLOCK)] = band_ref[0]
    bias = b_s[:, BLOCK:BLOCK + S]
    lam = lam_ref[...].astype(F32)
    lam_full = (jnp.exp(jnp.sum(lam[0:1] * lam[1:2], axis=-1, keepdims=True))
                - jnp.exp(jnp.sum(lam[2:3] * lam[3:4], axis=-1, keepdims=True)) + lambda_init)

    def softmax(sc):
        m = jnp.max(sc, axis=-1, keepdims=True)
        e = jnp.exp(sc - m)
        return e / jnp.sum(e, axis=-1, keepdims=True)

    a0 = softmax(_bdot_nt(jnp.where(c0, q, 0.0), kmat) + bias)
    a1 = softmax(_bdot_nt(jnp.where(c0, 0.0, q), kmat) + bias)
    o = _bdot(a0 - lam_full * a1, v_ref[0])
    o = o * lax.rsqrt(jnp.mean(o * o, axis=-1, keepdims=True) + 1e-5) * g_ref[...] * (1.0 - lambda_init)
    o_ref[0] = o.astype(o_ref.dtype)


def _diff_branch(p, band_tiles, rel_bias_t, lam, subln_g, lambda_init, out_dtype):
    B, S, _ = p.shape
    qoff = OFF_DIFF // LANES
    koff = (OFF_DIFF + DIFF_QK_COLS) // LANES
    voff = (OFF_DIFF + 2 * DIFF_QK_COLS) // LANES
    return pl.pallas_call(
        functools.partial(_diff_kernel, seq=S, lambda_init=lambda_init),
        grid=(B, DIFF_HEADS, S // BLOCK),
        in_specs=[pl.BlockSpec((1, BLOCK, LANES), lambda b, h, i: (b, i, qoff + h)),
                  pl.BlockSpec((1, S, LANES), lambda b, h, i: (b, 0, koff + h)),
                  pl.BlockSpec((1, S, LANES), lambda b, h, i: (b, 0, voff + h)),
                  pl.BlockSpec((1, BLOCK, 3 * BLOCK), lambda b, h, i: (WIN_Q_HEADS + h, 0, 0)),
                  pl.BlockSpec(memory_space=pltpu.SMEM),
                  pl.BlockSpec((4, HEAD_DIM), lambda b, h, i: (0, 0)),
                  pl.BlockSpec((1, DIFF_V_DIM), lambda b, h, i: (0, 0))],
        out_specs=pl.BlockSpec((1, BLOCK, LANES), lambda b, h, i: (b, i, h)),
        out_shape=jax.ShapeDtypeStruct((B, S, DIFF_V_COLS), out_dtype),
        scratch_shapes=[pltpu.VMEM((BLOCK, S + 2 * BLOCK), F32)],
        compiler_params=_cparams("arbitrary", "arbitrary", "arbitrary"),
        name="diff_attn",
    )(p, p, p, band_tiles, rel_bias_t, lam, subln_g.reshape(1, DIFF_V_DIM))


def _mem_kernel(q_ref, mem_ref, wkv_ref, o_ref):
    kv = _bdot(mem_ref[0], wkv_ref[...])
    kmat = kv[:, :MEM_WIDTH]
    vmat = kv[:, MEM_WIDTH:]
    q = q_ref[0].astype(F32) * (HEAD_DIM ** -0.5)
    lane = lax.broadcasted_iota(jnp.int32, (1, MEM_WIDTH), 1) // HEAD_DIM
    out = jnp.zeros(q.shape, F32)
    for hh in range(MEM_HEADS):
        s = _bdot_nt(jnp.where(lane == hh, q, 0.0), kmat)
        m = jnp.max(s, axis=-1, keepdims=True)
        e = jnp.exp(s - m)
        a = e / jnp.sum(e, axis=-1, keepdims=True)
        out = jnp.where(lane == hh, _bdot(a, vmat), out)
    o_ref[0] = out.astype(o_ref.dtype)


def _memory_branch(p, mem, w_kv, out_dtype):
    B, S, _ = p.shape
    nm, d = mem.shape[1], mem.shape[2]
    return pl.pallas_call(
        _mem_kernel,
        grid=(B,),
        in_specs=[pl.BlockSpec((1, S, MEM_WIDTH), lambda b: (b, 0, OFF_MEM // MEM_WIDTH)),
                  pl.BlockSpec((1, nm, d), lambda b: (b, 0, 0)),
                  pl.BlockSpec((d, 2 * MEM_WIDTH), lambda b: (0, 0))],
        out_specs=pl.BlockSpec((1, S, MEM_WIDTH), lambda b: (b, 0, 0)),
        out_shape=jax.ShapeDtypeStruct((B, S, MEM_WIDTH), out_dtype),
        compiler_params=_cparams("arbitrary"),
        name="memory_attn",
    )(p, mem, w_kv)


def _merge_kernel(x_ref, yr_ref, yw_ref, yd_ref, ym_ref, pg_ref, wb_ref, wo_ref, g_ref, b_ref,
                  o_ref, *, d_model, alpha):
    D = d_model
    ys = (yr_ref[...], yw_ref[...], yd_ref[...], ym_ref[...])
    off = 0
    merged = jnp.zeros((x_ref.shape[0], D), F32)
    for bi, yb in enumerate(ys):
        rows = yb.shape[1]
        proj = _bdot(yb, wb_ref[off:off + rows, :])
        merged = merged + _sigmoid(pg_ref[:, bi * D:(bi + 1) * D].astype(F32)) * proj
        off += rows
    z = alpha * x_ref[...] + _bdot(merged, wo_ref[...])
    o_ref[...] = _layer_norm(z, g_ref[...], b_ref[...])


def _merge(x2, y_rwkv, y_win, y_diff, y_mem, p2, w_branch, w_out, ln_g, ln_b, alpha, tm):
    T, D = x2.shape
    gate_blk = OFF_GATE // (N_BRANCHES * D)
    full = lambda a: pl.BlockSpec(a.shape, lambda i: (0, 0))
    tile = lambda w: pl.BlockSpec((tm, w), lambda i: (i, 0))
    return pl.pallas_call(
        functools.partial(_merge_kernel, d_model=D, alpha=alpha),
        grid=(T // tm,),
        in_specs=[tile(D), tile(y_rwkv.shape[1]), tile(y_win.shape[1]), tile(y_diff.shape[1]),
                  tile(y_mem.shape[1]),
                  pl.BlockSpec((tm, N_BRANCHES * D), lambda i: (i, gate_blk)),
                  full(w_branch), full(w_out),
                  pl.BlockSpec((1, D), lambda i: (0, 0)), pl.BlockSpec((1, D), lambda i: (0, 0))],
        out_specs=tile(D),
        out_shape=jax.ShapeDtypeStruct((T, D), F32),
        compiler_params=_cparams("arbitrary"),
        name="merge_out_ln",
    )(x2, y_rwkv, y_win, y_diff, y_mem, p2, w_branch, w_out, ln_g.reshape(1, D), ln_b.reshape(1, D))


def _route_kernel(x_ref, wr_ref, xe_ref, slot_ref, gate_ref, xb_s, lg_s, *, seq, cap):
    S = seq
    E = N_EXPERTS
    TR = min(S, 512)
    wr = wr_ref[...]
    w_hi = wr.astype(BF16)
    w_md = (wr - w_hi.astype(F32)).astype(BF16)
    dot = lambda a, b: jnp.dot(a, b, preferred_element_type=F32)

    def logit_tile(t, carry):
        rows = pl.ds(pl.multiple_of(t * TR, TR), TR)
        x = x_ref[0, rows, :]
        x_hi = x.astype(BF16)
        x_r1 = x - x_hi.astype(F32)
        x_md = x_r1.astype(BF16)
        x_lo = (x_r1 - x_md.astype(F32)).astype(BF16)
        xb_s[rows, :] = x_hi
        lg_s[rows, :] = (dot(x_hi, w_hi) + (dot(x_hi, w_md) + dot(x_md, w_hi))
                         + (dot(x_md, w_md) + dot(x_lo, w_hi)))
        return carry

    lax.fori_loop(0, S // TR, logit_tile, 0)
    col = lax.broadcasted_iota(jnp.int32, (1, LANES), 1)
    logits = jnp.where(col < E, lg_s[...], NEG_INF)
    mx = jnp.max(logits, axis=-1, keepdims=True)
    ex = jnp.exp(logits - mx)
    aff = ex / jnp.sum(ex, axis=-1, keepdims=True)
    aff_t = aff.T[0:E, :]
    bits = pltpu.bitcast(aff_t, jnp.int32)

    def bis(_, carry):
        lo_b, step = carry
        cand = lo_b + step
        cnt = jnp.sum(jnp.where(bits >= cand, 1.0, 0.0), axis=-1, keepdims=True)
        return jnp.where(cnt >= cap, cand, lo_b), step // 2

    lo0 = jnp.zeros((E, 1), jnp.int32)
    thr, _ = lax.fori_loop(0, 31, bis, (lo0, jnp.full((E, 1), 1 << 30, jnp.int32)))
    above = bits > thr
    tie = bits == thr
    n_above = jnp.sum(jnp.where(above, 1.0, 0.0), axis=-1, keepdims=True)
    PB = min(S, 256)
    tri = (lax.broadcasted_iota(jnp.int32, (PB, PB), 0)
           < lax.broadcasted_iota(jnp.int32, (PB, PB), 1)).astype(BF16)

    def prefix_count(mask_f):
        parts, run = [], jnp.zeros((E, 1), F32)
        for blk in range(S // PB):
            seg = mask_f[:, blk * PB:(blk + 1) * PB]
            parts.append(jnp.dot(seg.astype(BF16), tri, preferred_element_type=F32) + run)
            run = run + jnp.sum(seg, axis=-1, keepdims=True)
        return jnp.concatenate(parts, axis=1)

    tie_rank = prefix_count(jnp.where(tie, 1.0, 0.0))
    sel = above | (tie & (tie_rank < cap - n_above))
    pos = prefix_count(jnp.where(sel, 1.0, 0.0))
    slot = jnp.where(sel, pos, -1.0)
    slot_pad = jnp.concatenate([slot, jnp.full((LANES - E, S), -1.0, F32)], axis=0)
    slot_ref[0] = slot_pad.T
    ci = lax.broadcasted_iota(jnp.int32, (cap, S), 0).astype(F32)
    x_hi = xb_s[...]
    for e in range(E):
        onehot = slot[e:e + 1, :] == ci
        xe_ref[e, 0] = jnp.dot(jnp.where(onehot, 1.0, 0.0).astype(BF16), x_hi,
                               preferred_element_type=F32).astype(xe_ref.dtype)
        gsel = jnp.sum(jnp.where(onehot, aff_t[e:e + 1, :], 0.0), axis=-1, keepdims=True)
        gate_ref[e, 0] = jnp.broadcast_to(gsel, (cap, LANES))


def _route(x3, router_pad, cap):
    B, S, D = x3.shape
    E = N_EXPERTS
    return pl.pallas_call(
        functools.partial(_route_kernel, seq=S, cap=cap),
        grid=(B,),
        in_specs=[pl.BlockSpec((1, S, D), lambda b: (b, 0, 0)),
                  pl.BlockSpec((D, LANES), lambda b: (0, 0))],
        out_specs=[pl.BlockSpec((E, 1, cap, D), lambda b: (0, b, 0, 0)),
                   pl.BlockSpec((1, S, LANES), lambda b: (b, 0, 0)),
                   pl.BlockSpec((E, 1, cap, LANES), lambda b: (0, b, 0, 0))],
        out_shape=[jax.ShapeDtypeStruct((E, B, cap, D), BF16),
                   jax.ShapeDtypeStruct((B, S, LANES), F32),
                   jax.ShapeDtypeStruct((E, B, cap, LANES), F32)],
        scratch_shapes=[pltpu.VMEM((S, D), BF16), pltpu.VMEM((S, LANES), F32)],
        compiler_params=_cparams("arbitrary"),
        name="moe_route_gather",
    )(x3, router_pad)


def _ffn_kernel(xe_ref, g_ref, wg_ref, wu_ref, wd_ref, ye_ref, acc_s):
    f = pl.program_id(1)
    nb, cap, D = xe_ref.shape[1], xe_ref.shape[2], xe_ref.shape[3]
    xe = xe_ref[0].reshape(nb * cap, D)
    hg = jnp.dot(xe, wg_ref[0, 0].astype(BF16), preferred_element_type=F32)
    hu = jnp.dot(xe, wu_ref[0, 0].astype(BF16), preferred_element_type=F32)
    hid = (hg * _sigmoid(hg)) * hu
    part = _bdot(hid, wd_ref[0, 0])

    @pl.when(f == 0)
    def _():
        acc_s[...] = part

    @pl.when(f > 0)
    def _():
        acc_s[...] = acc_s[...] + part

    @pl.when(f == pl.num_programs(1) - 1)
    def _():
        gate = g_ref[0].reshape(nb * cap, LANES)[:, 0:1]
        ye_ref[0] = (acc_s[...] * gate).reshape(nb, cap, D).astype(ye_ref.dtype)


def _expert_ffn(xe, gate, w_gate, w_up, w_down, layer, tf):
    E, B, cap, D = xe.shape
    F = w_gate.shape[-1]
    return pl.pallas_call(
        _ffn_kernel,
        grid=(E, F // tf),
        in_specs=[pl.BlockSpec((1, B, cap, D), lambda e, f: (e, 0, 0, 0)),
                  pl.BlockSpec((1, B, cap, LANES), lambda e, f: (e, 0, 0, 0)),
                  pl.BlockSpec((1, 1, D, tf), lambda e, f: (layer, e, 0, f)),
                  pl.BlockSpec((1, 1, D, tf), lambda e, f: (layer, e, 0, f)),
                  pl.BlockSpec((1, 1, tf, D), lambda e, f: (layer, e, f, 0))],
        out_specs=pl.BlockSpec((1, B, cap, D), lambda e, f: (e, 0, 0, 0)),
        out_shape=jax.ShapeDtypeStruct((E, B, cap, D), BF16),
        scratch_shapes=[pltpu.VMEM((B * cap, D), F32)],
        compiler_params=_cparams("arbitrary", "arbitrary"),
        name="moe_expert_ffn",
    )(xe, gate, w_gate, w_up, w_down)


def _scatter_kernel(x_ref, slot_ref, ye_ref, g_ref, b_ref, o_ref, *, cap, alpha):
    ts = x_ref.shape[1]
    slot = slot_ref[0]
    ci = lax.broadcasted_iota(jnp.int32, (ts, cap), 1).astype(F32)
    acc = jnp.zeros((ts, x_ref.shape[2]), F32)
    for e in range(N_EXPERTS):
        onehot = slot[:, e:e + 1] == ci
        acc = acc + jnp.dot(jnp.where(onehot, 1.0, 0.0).astype(BF16), ye_ref[e, 0],
                            preferred_element_type=F32)
    o_ref[0] = _layer_norm(alpha * x_ref[0] + acc, g_ref[...], b_ref[...])


def _scatter_ln(x3, slot, ye, ln_g, ln_b, alpha, ts):
    B, S, D = x3.shape
    E, _, cap, _ = ye.shape
    return pl.pallas_call(
        functools.partial(_scatter_kernel, cap=cap, alpha=alpha),
        grid=(B, S // ts),
        in_specs=[pl.BlockSpec((1, ts, D), lambda b, i: (b, i, 0)),
                  pl.BlockSpec((1, ts, LANES), lambda b, i: (b, i, 0)),
                  pl.BlockSpec((E, 1, cap, D), lambda b, i: (0, b, 0, 0)),
                  pl.BlockSpec((1, D), lambda b, i: (0, 0)),
                  pl.BlockSpec((1, D), lambda b, i: (0, 0))],
        out_specs=pl.BlockSpec((1, ts, D), lambda b, i: (b, i, 0)),
        out_shape=jax.ShapeDtypeStruct((B, S, D), F32),
        compiler_params=_cparams("arbitrary", "arbitrary"),
        name="moe_scatter_ln",
    )(x3, slot, ye, ln_g.reshape(1, D), ln_b.reshape(1, D))


def kernel(x, mem, rel_bias, w_in, rwkv_mu, rwkv_w0, rwkv_w_up, rwkv_a0, rwkv_a_up, rwkv_g_up, rwkv_k_k, rwkv_k_a, rwkv_r_k, rwkv_gn_g, rwkv_gn_b, win_sink, diff_lambda, diff_subln_g, mem_w_kv, w_branch, w_out, ln1_g, ln1_b, router, exp_w_gate, exp_w_up, exp_w_down, ln2_g, ln2_b):
    B, S, D = x.shape
    depth = w_in.shape[0]
    alpha = (2 * depth) ** 0.25
    cap = CAPACITY_FACTOR * S // N_EXPERTS
    T = B * S
    rel_bias_t = rel_bias.T
    bias_tiles = _bias_tiles(rel_bias_t)
    act = F32
    for l in range(depth):
        lambda_init = 0.8 - 0.6 * math.exp(-0.3 * l)
        p2 = _matmul(x.reshape(T, D).astype(BF16), w_in[l].astype(BF16), 512, 1024, act)
        p = p2.reshape(B, S, -1)
        y_rwkv = _rwkv_branch(p, rwkv_mu[l], rwkv_w0[l], rwkv_w_up[l], rwkv_a0[l], rwkv_a_up[l],
                              rwkv_g_up[l], rwkv_k_k[l], rwkv_k_a[l], rwkv_r_k[l], rwkv_gn_g[l],
                              rwkv_gn_b[l], act)
        y_win = _window_branch(p, bias_tiles, win_sink[l], act)
        y_diff = _diff_branch(p, bias_tiles, rel_bias_t, diff_lambda[l], diff_subln_g[l], lambda_init, act)
        y_mem = _memory_branch(p, mem, mem_w_kv[l].astype(BF16), act)
        x1 = _merge(x.reshape(T, D), y_rwkv.reshape(T, -1), y_win.reshape(T, -1), y_diff.reshape(T, -1),
                    y_mem.reshape(T, -1), p2, w_branch[l].astype(BF16), w_out[l].astype(BF16),
                    ln1_g[l], ln1_b[l], alpha, 256).reshape(B, S, D)
        router_pad = jnp.concatenate([router[l], jnp.zeros((D, LANES - N_EXPERTS), F32)], axis=1)
        xe, slot, gate = _route(x1, router_pad, cap)
        ye = _expert_ffn(xe, gate, exp_w_gate, exp_w_up, exp_w_down, l, 512)
        x = _scatter_ln(x1, slot, ye, ln2_g[l], ln2_b[l], alpha, 512)
    return x
```

```python
import functools
import math

import jax
import jax.numpy as jnp
from jax import lax
from jax.experimental import pallas as pl
from jax.experimental.pallas import tpu as pltpu

F32 = jnp.float32
BF16 = jnp.bfloat16

HEAD_DIM = 64
LANES = 128
RWKV_HEADS = 16
RWKV_DIM = RWKV_HEADS * HEAD_DIM
DECAY_LORA = 64
ICLR_LORA = 64
GATE_LORA = 128
RWKV_GN_EPS = 64e-5
RWKV_COLS = 3 * RWKV_DIM + DECAY_LORA + ICLR_LORA + GATE_LORA
RWKV_CHUNK = 64
RWKV_UNROLL = 8
assert RWKV_CHUNK == HEAD_DIM
WIN_Q_HEADS = 16
WIN_KV_HEADS = 4
WIN_GROUP = WIN_Q_HEADS // WIN_KV_HEADS
WIN_Q_COLS = WIN_Q_HEADS * HEAD_DIM
WIN_KV_COLS = WIN_KV_HEADS * HEAD_DIM
WIN_COLS = WIN_Q_COLS + 2 * WIN_KV_COLS
WINDOW = 128
BLOCK = 128
DIFF_HEADS = 8
DIFF_V_DIM = 2 * HEAD_DIM
DIFF_QK_COLS = 2 * DIFF_HEADS * HEAD_DIM
DIFF_V_COLS = DIFF_HEADS * DIFF_V_DIM
DIFF_COLS = 2 * DIFF_QK_COLS + DIFF_V_COLS
MEM_HEADS = 4
MEM_WIDTH = MEM_HEADS * HEAD_DIM
NUM_BUCKETS = 32
N_EXPERTS = 16
CAPACITY_FACTOR = 2
N_BRANCHES = 4
NEG_INF = -1e30
LN_EPS = 1e-5
VMEM_LIMIT = 56 * 1024 * 1024

OFF_WIN = RWKV_COLS
OFF_DIFF = OFF_WIN + WIN_COLS
OFF_MEM = OFF_DIFF + DIFF_COLS
OFF_GATE = OFF_MEM + MEM_WIDTH


def _cparams(*sem):
    return pltpu.CompilerParams(dimension_semantics=sem, vmem_limit_bytes=VMEM_LIMIT)


def _bdot(a, b):
    return jnp.dot(a.astype(BF16), b.astype(BF16), preferred_element_type=F32)


def _bdot_nt(a, b):
    return lax.dot_general(a.astype(BF16), b.astype(BF16), (((1,), (1,)), ((), ())),
                           preferred_element_type=F32)


def _split_dot(a, b_exact):
    hi = a.astype(BF16)
    lo = (a - hi.astype(F32)).astype(BF16)
    return (jnp.dot(hi, b_exact, preferred_element_type=F32)
            + jnp.dot(lo, b_exact, preferred_element_type=F32))


def _sigmoid(x):
    return 0.5 * jnp.tanh(0.5 * x) + 0.5


def _layer_norm(x, g, b):
    mu = jnp.mean(x, axis=-1, keepdims=True)
    xc = x - mu
    var = jnp.mean(xc * xc, axis=-1, keepdims=True)
    return xc * lax.rsqrt(var + LN_EPS) * g + b


def _mm_kernel(x_ref, w_ref, o_ref):
    o_ref[...] = jnp.dot(x_ref[...], w_ref[...], preferred_element_type=F32).astype(o_ref.dtype)


def _matmul(x, w, tm, tn, out_dtype):
    m, k = x.shape
    n = w.shape[1]
    return pl.pallas_call(
        _mm_kernel,
        grid=(n // tn, m // tm),
        in_specs=[pl.BlockSpec((tm, k), lambda j, i: (i, 0)),
                  pl.BlockSpec((k, tn), lambda j, i: (0, j))],
        out_specs=pl.BlockSpec((tm, tn), lambda j, i: (i, j)),
        out_shape=jax.ShapeDtypeStruct((m, n), out_dtype),
        compiler_params=_cparams("arbitrary", "arbitrary"),
        name="in_proj",
    )(x, w)


def _bias_kernel(tab_ref, o_ref):
    h = pl.program_id(0)
    qi = lax.broadcasted_iota(jnp.int32, (BLOCK, 3 * BLOCK), 0)
    kj = lax.broadcasted_iota(jnp.int32, (BLOCK, 3 * BLOCK), 1)
    rel = kj - BLOCK - qi
    n = jnp.abs(rel)
    n2 = n * n
    large = jnp.full_like(n, NUM_BUCKETS // 4)
    for kpow in range(1, 8):
        large = large + jnp.where(n2 >= 64 * (2 ** kpow), 1, 0)
    bucket = jnp.where(rel > 0, NUM_BUCKETS // 2, 0) + jnp.where(n < NUM_BUCKETS // 4, n, large)
    acc = jnp.zeros((BLOCK, 3 * BLOCK), F32)
    for bkt in range(NUM_BUCKETS):
        acc = jnp.where(bucket == bkt, tab_ref[h, bkt], acc)
    o_ref[0] = acc


def _bias_tiles(rel_bias_t):
    nh = rel_bias_t.shape[0]
    return pl.pallas_call(
        _bias_kernel,
        grid=(nh,),
        in_specs=[pl.BlockSpec(memory_space=pltpu.SMEM)],
        out_specs=pl.BlockSpec((1, BLOCK, 3 * BLOCK), lambda h: (h, 0, 0)),
        out_shape=jax.ShapeDtypeStruct((nh, BLOCK, 3 * BLOCK), F32),
        compiler_params=_cparams("arbitrary"),
        name="t5_bias_tiles",
    )(rel_bias_t)


def _rwkv_kernel(pr_ref, pk_ref, pv_ref, pl_ref, mur_ref, muk_ref, muv_ref, mul_ref,
                 w0_ref, wup_ref, a0_ref, aup_ref, gup_ref, kk_ref, ka_ref, rk_ref,
                 gng_ref, gnb_ref, o_ref,
                 kt_s, rt_s, kh_s, bh_s, cum_s, v_s, y_s, m_s, n_s, *, seq):
    S = seq
    C = RWKV_CHUNK
    nchunk = S // C
    row = lax.broadcasted_iota(jnp.int32, (S, 1), 0)
    lane = lax.broadcasted_iota(jnp.int32, (1, LANES), 1)
    head0 = lane < HEAD_DIM
    ri = lax.broadcasted_iota(jnp.int32, (LANES, LANES), 0)
    ci = lax.broadcasted_iota(jnp.int32, (LANES, LANES), 1)
    same_head = (ri < HEAD_DIM) == (ci < HEAD_DIM)
    eye = ri == ci
    seg_ones = jnp.where(same_head, 1.0, 0.0).astype(BF16)

    GRP = min(S, 4 * C)
    gi = lax.broadcasted_iota(jnp.int32, (GRP, GRP), 0)
    gj = lax.broadcasted_iota(jnp.int32, (GRP, GRP), 1)
    same_chunk = (gi // C) == (gj // C)
    scan_tri = (jnp.where(same_chunk & (gj <= gi), 1.0, 0.0).astype(BF16),
                jnp.where(same_chunk & (gj >= gi), 1.0, 0.0).astype(BF16))

    def shift(x, mu_ref):
        prev = jnp.where(row == 0, 0.0, pltpu.roll(x, 1, 0))
        nxt = jnp.where(row == S - 1, 0.0, pltpu.roll(x, S - 1, 0))
        return x + mu_ref[0:1, :] * (prev - x) + mu_ref[1:2, :] * (nxt - x)

    r = shift(pr_ref[0].astype(F32), mur_ref)
    k = shift(pk_ref[0].astype(F32), muk_ref)
    v = shift(pv_ref[0].astype(F32), muv_ref)
    lo = shift(pl_ref[0].astype(F32), mul_ref)
    l1 = lo[:, :LANES]
    l2 = lo[:, LANES:]
    g = _bdot(_sigmoid(l2), gup_ref[...])
    kkr = k * kk_ref[...]
    kk = kkr * lax.rsqrt(jnp.maximum(_split_dot(kkr * kkr, seg_ones), 1e-12))
    v_s[...] = v
    tanh_l1 = jnp.tanh(l1)
    kd_sum = jnp.zeros((S, LANES), F32)

    for z in range(2):
        wl = w0_ref[z:z + 1, :] + _bdot(tanh_l1, wup_ref[z])
        logw = -math.exp(-0.5) * _sigmoid(wl)
        a = _sigmoid(a0_ref[z:z + 1, :] + _bdot(l1, aup_ref[z]))
        kd = k * (1.0 + (a - 1.0) * ka_ref[...])
        bb = a * kk
        kd_sum = kd_sum + kd
        lw_hi = logw.astype(BF16)
        lw_lo = (logw - lw_hi.astype(F32)).astype(BF16)
        tri = scan_tri[z]
        cum = jnp.concatenate(
            [jnp.dot(tri, lw_hi[g0:g0 + GRP], preferred_element_type=F32)
             + jnp.dot(tri, lw_lo[g0:g0 + GRP], preferred_element_type=F32)
             for g0 in range(0, S, GRP)], axis=0)
        e_out = jnp.exp(-cum)
        kt_s[z] = kk * jnp.exp(cum - logw)
        rt_s[z] = r * jnp.exp(cum)
        kh_s[z] = kd * e_out
        bh_s[z] = bb * e_out
        cum_s[z] = cum

    Q = 4 * HEAD_DIM
    qi = lax.broadcasted_iota(jnp.int32, (Q, Q), 0)
    qj = lax.broadcasted_iota(jnp.int32, (Q, Q), 1)
    same_blk = (qi // HEAD_DIM) == (qj // HEAD_DIM)
    ct = lax.broadcasted_iota(jnp.int32, (C, Q), 0)
    cj = lax.broadcasted_iota(jnp.int32, (C, Q), 1)
    dt = ((cj % C) - ct) * jnp.where(cj < 2 * C, 1, -1)
    strict_m = dt < 0
    incl_m = dt <= 0

    def both(ref, rows):
        return jnp.concatenate([ref[0, rows, :], ref[1, rows, :]], axis=1)

    def bd(a):
        return jnp.where(same_blk, jnp.concatenate([a.astype(BF16)] * 4, axis=0), 0.0)

    def dot(a, b):
        return jnp.dot(a.astype(BF16), b, preferred_element_type=F32)

    def dot_nt(a, b):
        return lax.dot_general(a.astype(BF16), b, (((1,), (1,)), ((), ())), preferred_element_type=F32)

    U = min(RWKV_UNROLL, nchunk)

    def phase1(it, carry):
        ur = range(U)
        cs = [it * U + u for u in ur]
        rws = [pl.ds(pl.multiple_of(c * C, C), C) for c in cs]
        kt = [both(kt_s, r_) for r_ in rws]
        rt = [both(rt_s, r_) for r_ in rws]
        vbd = [bd(jnp.concatenate([v_s[r_, :]] * 2, axis=1)) for r_ in rws]
        lhs = [jnp.concatenate([kt[u], rt[u]], axis=0) for u in ur]
        gb = [dot_nt(lhs[u], bd(both(bh_s, rws[u]))) for u in ur]
        gk = [dot_nt(lhs[u], bd(both(kh_s, rws[u]))) for u in ur]
        a_rb = [jnp.where(incl_m, gb[u][C:], 0.0) for u in ur]
        a_kr = [jnp.concatenate([jnp.where(strict_m, gk[u][:C], 0.0),
                                 jnp.where(incl_m, gk[u][C:], 0.0)], axis=0) for u in ur]
        yy = [jnp.where(strict_m, -gb[u][:C], 0.0) for u in ur]
        x = list(yy)
        p = [dot(yy[u], bd(yy[u])) for u in ur]
        yy = p
        for step in range(1, 6):
            last = step == 5
            p = [dot(x[u] if last else jnp.concatenate([yy[u], x[u]], axis=0), bd(yy[u])) for u in ur]
            x = [x[u] + yy[u] + (p[u] if last else p[u][C:]) for u in ur]
            if not last:
                yy = [p[u][:C] for u in ur]
        avr = [dot(a_kr[u], vbd[u]) for u in ur]
        tz = []
        for u in ur:
            av = avr[u][:C]
            zz = jnp.concatenate([kt[u], av], axis=1)
            tz.append(zz + dot(x[u], jnp.concatenate([bd(kt[u]), bd(av)], axis=1)))
        cor = [dot(a_rb[u], jnp.concatenate([bd(tz[u][:, :Q]), bd(tz[u][:, Q:])], axis=1)) for u in ur]
        for u in ur:
            rr = rt[u] - cor[u][:, :Q]
            yl = avr[u][C:] - cor[u][:, Q:]
            y_s[rws[u], :] = yl[:, :LANES] + yl[:, LANES:]
            for z in range(2):
                zl = slice(z * LANES, (z + 1) * LANES)
                rt_s[z, rws[u], :] = rr[:, zl]
                far = C - 1 if z == 0 else 0
                pt = jnp.exp(cum_s[z, rws[u], :][far:far + 1, :])
                ends = jnp.concatenate([bh_s[z, rws[u], :] * pt, kh_s[z, rws[u], :] * pt], axis=0)
                rhs = jnp.concatenate(
                    [jnp.concatenate([-tz[u][:, zl], -tz[u][:, Q + z * LANES:Q + (z + 1) * LANES]], axis=1),
                     jnp.concatenate([jnp.zeros((C, LANES), F32), v_s[rws[u], :]], axis=1)], axis=0)
                mn = _bdot(ends.T, rhs)
                m_s[z, cs[u]] = jnp.where(eye, jnp.broadcast_to(pt, (LANES, LANES)), 0.0) \
                    + jnp.where(same_head, mn[:, :LANES], 0.0)
                n_s[z, cs[u]] = jnp.where(same_head, mn[:, LANES:], 0.0)
        return carry

    lax.fori_loop(0, nchunk // U, phase1, 0)

    def phase2(i, hs):
        out = []
        for z in range(2):
            c = i if z == 0 else nchunk - 1 - i
            rows = pl.ds(pl.multiple_of(c * C, C), C)
            y_s[rows, :] = y_s[rows, :] + dot(rt_s[z, rows, :], hs[z])
            out.append((dot(m_s[z, c], hs[z]) + n_s[z, c]).astype(BF16))
        return tuple(out)

    h0 = jnp.zeros((LANES, LANES), BF16)
    lax.fori_loop(0, nchunk, phase2, (h0, h0))

    y = y_s[...]
    inv_n = 1.0 / HEAD_DIM
    mu_y = _split_dot(y, seg_ones) * inv_n
    yc = y - mu_y
    var_y = _split_dot(yc * yc, seg_ones) * inv_n
    yn = yc * lax.rsqrt(var_y + RWKV_GN_EPS) * gng_ref[...] + gnb_ref[...]
    bonus = _split_dot(r * kd_sum * rk_ref[...], seg_ones) * v
    o_ref[0] = ((yn + bonus) * g).astype(o_ref.dtype)


def _rwkv_branch(p, mu, w0, w_up, a0, a_up, g_up, k_k, k_a, r_k, gn_g, gn_b, out_dtype):
    B, S, _ = p.shape
    npair = RWKV_DIM // LANES
    zpad = jnp.zeros((2, LANES - DECAY_LORA, RWKV_DIM), F32)
    wup_p = jnp.concatenate([w_up, zpad], axis=1).astype(BF16)
    aup_p = jnp.concatenate([zpad, a_up], axis=1).astype(BF16)
    row = lambda a: a.reshape(1, RWKV_DIM)
    lora_blk = 3 * RWKV_DIM // (2 * LANES)

    def pspec(off):
        return pl.BlockSpec((1, S, LANES), lambda b, j: (b, 0, off + j))

    def vspec(rows, off=0):
        return pl.BlockSpec((rows, LANES), lambda b, j: (0, off + j))

    scratch = [pltpu.VMEM((2, S, LANES), F32) for _ in range(5)] + [
        pltpu.VMEM((S, LANES), F32), pltpu.VMEM((S, LANES), F32),
        pltpu.VMEM((2, S // RWKV_CHUNK, LANES, LANES), F32),
        pltpu.VMEM((2, S // RWKV_CHUNK, LANES, LANES), F32)]
    return pl.pallas_call(
        functools.partial(_rwkv_kernel, seq=S),
        grid=(B, npair),
        in_specs=[pspec(0), pspec(npair), pspec(2 * npair),
                  pl.BlockSpec((1, S, 2 * LANES), lambda b, j: (b, 0, lora_blk)),
                  vspec(2, 0), vspec(2, npair), vspec(2, 2 * npair),
                  pl.BlockSpec((2, 2 * LANES), lambda b, j: (0, lora_blk)),
                  vspec(2),
                  pl.BlockSpec((2, LANES, LANES), lambda b, j: (0, 0, j)),
                  vspec(2),
                  pl.BlockSpec((2, LANES, LANES), lambda b, j: (0, 0, j)),
                  vspec(GATE_LORA), vspec(1), vspec(1), vspec(1), vspec(1), vspec(1)],
        out_specs=pl.BlockSpec((1, S, LANES), lambda b, j: (b, 0, j)),
        out_shape=jax.ShapeDtypeStruct((B, S, RWKV_DIM), out_dtype),
        scratch_shapes=scratch,
        compiler_params=_cparams("arbitrary", "arbitrary"),
        name="rwkv7_scan",
    )(p, p, p, p, mu, mu, mu, mu, w0, wup_p, a0, aup_p, g_up.astype(BF16),
      row(k_k), row(k_a), row(r_k), row(gn_g), row(gn_b))


def _win_kernel(q_ref, k_ref, v_ref, bias_ref, sink_ref, o_ref, kp_s, vp_s, *, seq):
    S = seq
    G = WIN_GROUP
    hk = pl.program_id(1)
    odd = (hk % 2) == 1
    kfull = k_ref[0].astype(F32)
    vfull = v_ref[0].astype(F32)
    zpad = jnp.zeros((BLOCK, HEAD_DIM), kp_s.dtype)
    kp_s[0:BLOCK, :] = zpad
    kp_s[BLOCK + S:, :] = zpad
    vp_s[0:BLOCK, :] = zpad
    vp_s[BLOCK + S:, :] = zpad
    kp_s[BLOCK:BLOCK + S, :] = jnp.where(odd, kfull[:, HEAD_DIM:], kfull[:, :HEAD_DIM]).astype(kp_s.dtype)
    vp_s[BLOCK:BLOCK + S, :] = jnp.where(odd, vfull[:, HEAD_DIM:], vfull[:, :HEAD_DIM]).astype(vp_s.dtype)
    bias4 = bias_ref[...].reshape(G * BLOCK, 3 * BLOCK)
    rowi = lax.broadcasted_iota(jnp.int32, (G * BLOCK, 3 * BLOCK), 0)
    kj = lax.broadcasted_iota(jnp.int32, (G * BLOCK, 3 * BLOCK), 1)
    rel = kj - BLOCK - (rowi % BLOCK)
    in_band = jnp.abs(rel) <= WINDOW
    rgrp = lax.broadcasted_iota(jnp.int32, (G * BLOCK, 1), 0) // BLOCK
    sink4 = jnp.zeros((G * BLOCK, 1), F32)
    for gi in range(G):
        sink4 = jnp.where(rgrp == gi, sink_ref[hk * G + gi], sink4)

    def body(n, carry):
        r0 = pl.multiple_of(n * BLOCK, BLOCK)
        qb = q_ref[0, pl.ds(r0, BLOCK), :].astype(F32)
        q4 = jnp.concatenate([qb[:, gi * HEAD_DIM:(gi + 1) * HEAD_DIM] for gi in range(G)], axis=0)
        kb = kp_s[pl.ds(r0, 3 * BLOCK), :]
        vb = vp_s[pl.ds(r0, 3 * BLOCK), :]
        s = _bdot_nt(q4, kb) * (HEAD_DIM ** -0.5)
        kpos = r0 + kj - BLOCK
        valid = in_band & (kpos >= 0) & (kpos < S)
        s = jnp.where(valid, s + bias4, NEG_INF)
        m = jnp.maximum(jnp.max(s, axis=-1, keepdims=True), sink4)
        e = jnp.exp(s - m)
        den = jnp.sum(e, axis=-1, keepdims=True) + jnp.exp(sink4 - m)
        o4 = _bdot(e / den, vb)
        o_ref[0, pl.ds(r0, BLOCK), :] = jnp.concatenate(
            [o4[gi * BLOCK:(gi + 1) * BLOCK] for gi in range(G)], axis=1).astype(o_ref.dtype)
        return carry

    lax.fori_loop(0, S // BLOCK, body, 0)


def _window_branch(p, bias_tiles, sink, out_dtype):
    B, S, _ = p.shape
    gw = WIN_GROUP * HEAD_DIM
    qoff = OFF_WIN // gw
    koff = (OFF_WIN + WIN_Q_COLS) // LANES
    voff = (OFF_WIN + WIN_Q_COLS + WIN_KV_COLS) // LANES
    return pl.pallas_call(
        functools.partial(_win_kernel, seq=S),
        grid=(B, WIN_KV_HEADS),
        in_specs=[pl.BlockSpec((1, S, gw), lambda b, h: (b, 0, qoff + h)),
                  pl.BlockSpec((1, S, LANES), lambda b, h: (b, 0, koff + h // 2)),
                  pl.BlockSpec((1, S, LANES), lambda b, h: (b, 0, voff + h // 2)),
                  pl.BlockSpec((WIN_GROUP, BLOCK, 3 * BLOCK), lambda b, h: (h, 0, 0)),
                  pl.BlockSpec(memory_space=pltpu.SMEM)],
        out_specs=pl.BlockSpec((1, S, gw), lambda b, h: (b, 0, h)),
        out_shape=jax.ShapeDtypeStruct((B, S, WIN_Q_COLS), out_dtype),
        scratch_shapes=[pltpu.VMEM((S + 2 * BLOCK, HEAD_DIM), F32),
                        pltpu.VMEM((S + 2 * BLOCK, HEAD_DIM), F32)],
        compiler_params=_cparams("arbitrary", "arbitrary"),
        name="window_attn",
    )(p, p, p, bias_tiles, sink)


def _diff_kernel(q_ref, k_ref, v_ref, band_ref, tab_ref, lam_ref, g_ref, o_ref, b_s, *,
                 seq, lambda_init):
    S = seq
    h = pl.program_id(1)
    i = pl.program_id(2)
    lane = lax.broadcasted_iota(jnp.int32, (1, LANES), 1)
    c0 = lane < HEAD_DIM
    q = q_ref[0].astype(F32) * (HEAD_DIM ** -0.5)
    kmat = k_ref[0]
    cp = lax.broadcasted_iota(jnp.int32, (BLOCK, S + 2 * BLOCK), 1)
    far_l = tab_ref[WIN_Q_HEADS + h, NUM_BUCKETS // 2 - 1]
    far_r = tab_ref[WIN_Q_HEADS + h, NUM_BUCKETS - 1]
    b_s[...] = jnp.where(cp < i * BLOCK + BLOCK, far_l, far_r)
    c_lo = pl.multiple_of(i * BLOCK, BLOCK)
    b_s[:, pl.ds(c_lo, 3 * BLOCK)] = band_ref[0]
    bias = b_s[:, BLOCK:BLOCK + S]
    lam = lam_ref[...].astype(F32)
    lam_full = (jnp.exp(jnp.sum(lam[0:1] * lam[1:2], axis=-1, keepdims=True))
                - jnp.exp(jnp.sum(lam[2:3] * lam[3:4], axis=-1, keepdims=True)) + lambda_init)

    def softmax(sc):
        m = jnp.max(sc, axis=-1, keepdims=True)
        e = jnp.exp(sc - m)
        return e / jnp.sum(e, axis=-1, keepdims=True)

    a0 = softmax(_bdot_nt(jnp.where(c0, q, 0.0), kmat) + bias)
    a1 = softmax(_bdot_nt(jnp.where(c0, 0.0, q), kmat) + bias)
    o = _bdot(a0 - lam_full * a1, v_ref[0])
    o = o * lax.rsqrt(jnp.mean(o * o, axis=-1, keepdims=True) + 1e-5) * g_ref[...] * (1.0 - lambda_init)
    o_ref[0] = o.astype(o_ref.dtype)


def _diff_branch(p, band_tiles, rel_bias_t, lam, subln_g, lambda_init, out_dtype):
    B, S, _ = p.shape
    qoff = OFF_DIFF // LANES
    koff = (OFF_DIFF + DIFF_QK_COLS) // LANES
    voff = (OFF_DIFF + 2 * DIFF_QK_COLS) // LANES
    return pl.pallas_call(
        functools.partial(_diff_kernel, seq=S, lambda_init=lambda_init),
        grid=(B, DIFF_HEADS, S // BLOCK),
        in_specs=[pl.BlockSpec((1, BLOCK, LANES), lambda b, h, i: (b, i, qoff + h)),
                  pl.BlockSpec((1, S, LANES), lambda b, h, i: (b, 0, koff + h)),
                  pl.BlockSpec((1, S, LANES), lambda b, h, i: (b, 0, voff + h)),
                  pl.BlockSpec((1, BLOCK, 3 * BLOCK), lambda b, h, i: (WIN_Q_HEADS + h, 0, 0)),
                  pl.BlockSpec(memory_space=pltpu.SMEM),
                  pl.BlockSpec((4, HEAD_DIM), lambda b, h, i: (0, 0)),
                  pl.BlockSpec((1, DIFF_V_DIM), lambda b, h, i: (0, 0))],
        out_specs=pl.BlockSpec((1, BLOCK, LANES), lambda b, h, i: (b, i, h)),
        out_shape=jax.ShapeDtypeStruct((B, S, DIFF_V_COLS), out_dtype),
        scratch_shapes=[pltpu.VMEM((BLOCK, S + 2 * BLOCK), F32)],
        compiler_params=_cparams("arbitrary", "arbitrary", "arbitrary"),
        name="diff_attn",
    )(p, p, p, band_tiles, rel_bias_t, lam, subln_g.reshape(1, DIFF_V_DIM))


def _mem_kernel(q_ref, mem_ref, wkv_ref, o_ref):
    kv = _bdot(mem_ref[0], wkv_ref[...])
    kmat = kv[:, :MEM_WIDTH]
    vmat = kv[:, MEM_WIDTH:]
    q = q_ref[0].astype(F32) * (HEAD_DIM ** -0.5)
    lane = lax.broadcasted_iota(jnp.int32, (1, MEM_WIDTH), 1) // HEAD_DIM
    out = jnp.zeros(q.shape, F32)
    for hh in range(MEM_HEADS):
        s = _bdot_nt(jnp.where(lane == hh, q, 0.0), kmat)
        m = jnp.max(s, axis=-1, keepdims=True)
        e = jnp.exp(s - m)
        a = e / jnp.sum(e, axis=-1, keepdims=True)
        out = jnp.where(lane == hh, _bdot(a, vmat), out)
    o_ref[0] = out.astype(o_ref.dtype)


def _memory_branch(p, mem, w_kv, out_dtype):
    B, S, _ = p.shape
    nm, d = mem.shape[1], mem.shape[2]
    return pl.pallas_call(
        _mem_kernel,
        grid=(B,),
        in_specs=[pl.BlockSpec((1, S, MEM_WIDTH), lambda b: (b, 0, OFF_MEM // MEM_WIDTH)),
                  pl.BlockSpec((1, nm, d), lambda b: (b, 0, 0)),
                  pl.BlockSpec((d, 2 * MEM_WIDTH), lambda b: (0, 0))],
        out_specs=pl.BlockSpec((1, S, MEM_WIDTH), lambda b: (b, 0, 0)),
        out_shape=jax.ShapeDtypeStruct((B, S, MEM_WIDTH), out_dtype),
        compiler_params=_cparams("arbitrary"),
        name="memory_attn",
    )(p, mem, w_kv)


def _merge_kernel(x_ref, yr_ref, yw_ref, yd_ref, ym_ref, pg_ref, wb_ref, wo_ref, g_ref, b_ref,
                  o_ref, *, d_model, alpha):
    D = d_model
    ys = (yr_ref[...], yw_ref[...], yd_ref[...], ym_ref[...])
    off = 0
    merged = jnp.zeros((x_ref.shape[0], D), F32)
    for bi, yb in enumerate(ys):
        rows = yb.shape[1]
        proj = _bdot(yb, wb_ref[off:off + rows, :])
        merged = merged + _sigmoid(pg_ref[:, bi * D:(bi + 1) * D].astype(F32)) * proj
        off += rows
    z = alpha * x_ref[...] + _bdot(merged, wo_ref[...])
    o_ref[...] = _layer_norm(z, g_ref[...], b_ref[...])


def _merge(x2, y_rwkv, y_win, y_diff, y_mem, p2, w_branch, w_out, ln_g, ln_b, alpha, tm):
    T, D = x2.shape
    gate_blk = OFF_GATE // (N_BRANCHES * D)
    full = lambda a: pl.BlockSpec(a.shape, lambda i: (0, 0))
    tile = lambda w: pl.BlockSpec((tm, w), lambda i: (i, 0))
    return pl.pallas_call(
        functools.partial(_merge_kernel, d_model=D, alpha=alpha),
        grid=(T // tm,),
        in_specs=[tile(D), tile(y_rwkv.shape[1]), tile(y_win.shape[1]), tile(y_diff.shape[1]),
                  tile(y_mem.shape[1]),
                  pl.BlockSpec((tm, N_BRANCHES * D), lambda i: (i, gate_blk)),
                  full(w_branch), full(w_out),
                  pl.BlockSpec((1, D), lambda i: (0, 0)), pl.BlockSpec((1, D), lambda i: (0, 0))],
        out_specs=tile(D),
        out_shape=jax.ShapeDtypeStruct((T, D), F32),
        compiler_params=_cparams("arbitrary"),
        name="merge_out_ln",
    )(x2, y_rwkv, y_win, y_diff, y_mem, p2, w_branch, w_out, ln_g.reshape(1, D), ln_b.reshape(1, D))


def _route_kernel(x_ref, wr_ref, xe_ref, slot_ref, gate_ref, xb_s, lg_s, *, seq, cap):
    S = seq
    E = N_EXPERTS
    TR = min(S, 512)
    wr = wr_ref[...]
    w_hi = wr.astype(BF16)
    w_md = (wr - w_hi.astype(F32)).astype(BF16)
    dot = lambda a, b: jnp.dot(a, b, preferred_element_type=F32)

    def logit_tile(t, carry):
        rows = pl.ds(pl.multiple_of(t * TR, TR), TR)
        x = x_ref[0, rows, :]
        x_hi = x.astype(BF16)
        x_r1 = x - x_hi.astype(F32)
        x_md = x_r1.astype(BF16)
        x_lo = (x_r1 - x_md.astype(F32)).astype(BF16)
        xb_s[rows, :] = x_hi
        lg_s[rows, :] = (dot(x_hi, w_hi) + (dot(x_hi, w_md) + dot(x_md, w_hi))
                         + (dot(x_md, w_md) + dot(x_lo, w_hi)))
        return carry

    lax.fori_loop(0, S // TR, logit_tile, 0)
    col = lax.broadcasted_iota(jnp.int32, (1, LANES), 1)
    logits = jnp.where(col < E, lg_s[...], NEG_INF)
    mx = jnp.max(logits, axis=-1, keepdims=True)
    ex = jnp.exp(logits - mx)
    aff = ex / jnp.sum(ex, axis=-1, keepdims=True)
    aff_t = aff.T[0:E, :]
    bits = pltpu.bitcast(aff_t, jnp.int32)

    def bis(_, carry):
        lo_b, step = carry
        cand = lo_b + step
        cnt = jnp.sum(jnp.where(bits >= cand, 1.0, 0.0), axis=-1, keepdims=True)
        return jnp.where(cnt >= cap, cand, lo_b), step // 2

    lo0 = jnp.zeros((E, 1), jnp.int32)
    thr, _ = lax.fori_loop(0, 31, bis, (lo0, jnp.full((E, 1), 1 << 30, jnp.int32)))
    above = bits > thr
    tie = bits == thr
    n_above = jnp.sum(jnp.where(above, 1.0, 0.0), axis=-1, keepdims=True)
    PB = min(S, 256)
    tri = (lax.broadcasted_iota(jnp.int32, (PB, PB), 0)
           < lax.broadcasted_iota(jnp.int32, (PB, PB), 1)).astype(BF16)

    def prefix_count(mask_f):
        parts, run = [], jnp.zeros((E, 1), F32)
        for blk in range(S // PB):
            seg = mask_f[:, blk * PB:(blk + 1) * PB]
            parts.append(jnp.dot(seg.astype(BF16), tri, preferred_element_type=F32) + run)
            run = run + jnp.sum(seg, axis=-1, keepdims=True)
        return jnp.concatenate(parts, axis=1)

    tie_rank = prefix_count(jnp.where(tie, 1.0, 0.0))
    sel = above | (tie & (tie_rank < cap - n_above))
    pos = prefix_count(jnp.where(sel, 1.0, 0.0))
    slot = jnp.where(sel, pos, -1.0)
    slot_pad = jnp.concatenate([slot, jnp.full((LANES - E, S), -1.0, F32)], axis=0)
    slot_ref[0] = slot_pad.T
    ci = lax.broadcasted_iota(jnp.int32, (cap, S), 0).astype(F32)
    x_hi = xb_s[...]
    for e in range(E):
        onehot = slot[e:e + 1, :] == ci
        xe_ref[e, 0] = jnp.dot(jnp.where(onehot, 1.0, 0.0).astype(BF16), x_hi,
                               preferred_element_type=F32).astype(xe_ref.dtype)
        gsel = jnp.sum(jnp.where(onehot, aff_t[e:e + 1, :], 0.0), axis=-1, keepdims=True)
        gate_ref[e, 0] = jnp.broadcast_to(gsel, (cap, LANES))


def _route(x3, router_pad, cap):
    B, S, D = x3.shape
    E = N_EXPERTS
    return pl.pallas_call(
        functools.partial(_route_kernel, seq=S, cap=cap),
        grid=(B,),
        in_specs=[pl.BlockSpec((1, S, D), lambda b: (b, 0, 0)),
                  pl.BlockSpec((D, LANES), lambda b: (0, 0))],
        out_specs=[pl.BlockSpec((E, 1, cap, D), lambda b: (0, b, 0, 0)),
                   pl.BlockSpec((1, S, LANES), lambda b: (b, 0, 0)),
                   pl.BlockSpec((E, 1, cap, LANES), lambda b: (0, b, 0, 0))],
        out_shape=[jax.ShapeDtypeStruct((E, B, cap, D), BF16),
                   jax.ShapeDtypeStruct((B, S, LANES), F32),
                   jax.ShapeDtypeStruct((E, B, cap, LANES), F32)],
        scratch_shapes=[pltpu.VMEM((S, D), BF16), pltpu.VMEM((S, LANES), F32)],
        compiler_params=_cparams("arbitrary"),
        name="moe_route_gather",
    )(x3, router_pad)


def _ffn_kernel(xe_ref, g_ref, wg_ref, wu_ref, wd_ref, ye_ref, acc_s):
    f = pl.program_id(1)
    nb, cap, D = xe_ref.shape[1], xe_ref.shape[2], xe_ref.shape[3]
    xe = xe_ref[0].reshape(nb * cap, D)
    hg = jnp.dot(xe, wg_ref[0, 0].astype(BF16), preferred_element_type=F32)
    hu = jnp.dot(xe, wu_ref[0, 0].astype(BF16), preferred_element_type=F32)
    hid = (hg * _sigmoid(hg)) * hu
    part = _bdot(hid, wd_ref[0, 0])

    @pl.when(f == 0)
    def _():
        acc_s[...] = part

    @pl.when(f > 0)
    def _():
        acc_s[...] = acc_s[...] + part

    @pl.when(f == pl.num_programs(1) - 1)
    def _():
        gate = g_ref[0].reshape(nb * cap, LANES)[:, 0:1]
        ye_ref[0] = (acc_s[...] * gate).reshape(nb, cap, D).astype(ye_ref.dtype)


def _expert_ffn(xe, gate, w_gate, w_up, w_down, layer, tf):
    E, B, cap, D = xe.shape
    F = w_gate.shape[-1]
    return pl.pallas_call(
        _ffn_kernel,
        grid=(E, F // tf),
        in_specs=[pl.BlockSpec((1, B, cap, D), lambda e, f: (e, 0, 0, 0)),
                  pl.BlockSpec((1, B, cap, LANES), lambda e, f: (e, 0, 0, 0)),
                  pl.BlockSpec((1, 1, D, tf), lambda e, f: (layer, e, 0, f)),
                  pl.BlockSpec((1, 1, D, tf), lambda e, f: (layer, e, 0, f)),
                  pl.BlockSpec((1, 1, tf, D), lambda e, f: (layer, e, f, 0))],
        out_specs=pl.BlockSpec((1, B, cap, D), lambda e, f: (e, 0, 0, 0)),
        out_shape=jax.ShapeDtypeStruct((E, B, cap, D), BF16),
        scratch_shapes=[pltpu.VMEM((B * cap, D), F32)],
        compiler_params=_cparams("arbitrary", "arbitrary"),
        name="moe_expert_ffn",
    )(xe, gate, w_gate, w_up, w_down)


def _scatter_kernel(x_ref, slot_ref, ye_ref, g_ref, b_ref, o_ref, *, cap, alpha):
    ts = x_ref.shape[1]
    slot = slot_ref[0]
    ci = lax.broadcasted_iota(jnp.int32, (ts, cap), 1).astype(F32)
    acc = jnp.zeros((ts, x_ref.shape[2]), F32)
    for e in range(N_EXPERTS):
        onehot = slot[:, e:e + 1] == ci
        acc = acc + jnp.dot(jnp.where(onehot, 1.0, 0.0).astype(BF16), ye_ref[e, 0],
                            preferred_element_type=F32)
    o_ref[0] = _layer_norm(alpha * x_ref[0] + acc, g_ref[...], b_ref[...])


def _scatter_ln(x3, slot, ye, ln_g, ln_b, alpha, ts):
    B, S, D = x3.shape
    E, _, cap, _ = ye.shape
    return pl.pallas_call(
        functools.partial(_scatter_kernel, cap=cap, alpha=alpha),
        grid=(B, S // ts),
        in_specs=[pl.BlockSpec((1, ts, D), lambda b, i: (b, i, 0)),
                  pl.BlockSpec((1, ts, LANES), lambda b, i: (b, i, 0)),
                  pl.BlockSpec((E, 1, cap, D), lambda b, i: (0, b, 0, 0)),
                  pl.BlockSpec((1, D), lambda b, i: (0, 0)),
                  pl.BlockSpec((1, D), lambda b, i: (0, 0))],
        out_specs=pl.BlockSpec((1, ts, D), lambda b, i: (b, i, 0)),
        out_shape=jax.ShapeDtypeStruct((B, S, D), F32),
        compiler_params=_cparams("arbitrary", "arbitrary"),
        name="moe_scatter_ln",
    )(x3, slot, ye, ln_g.reshape(1, D), ln_b.reshape(1, D))


def kernel(x, mem, rel_bias, w_in, rwkv_mu, rwkv_w0, rwkv_w_up, rwkv_a0, rwkv_a_up, rwkv_g_up, rwkv_k_k, rwkv_k_a, rwkv_r_k, rwkv_gn_g, rwkv_gn_b, win_sink, diff_lambda, diff_subln_g, mem_w_kv, w_branch, w_out, ln1_g, ln1_b, router, exp_w_gate, exp_w_up, exp_w_down, ln2_g, ln2_b):
    B, S, D = x.shape
    depth = w_in.shape[0]
    alpha = (2 * depth) ** 0.25
    cap = CAPACITY_FACTOR * S // N_EXPERTS
    T = B * S
    rel_bias_t = rel_bias.T
    bias_tiles = _bias_tiles(rel_bias_t)
    act = BF16
    for l in range(depth):
        lambda_init = 0.8 - 0.6 * math.exp(-0.3 * l)
        p2 = _matmul(x.reshape(T, D).astype(BF16), w_in[l].astype(BF16), 512, 1024, act)
        p = p2.reshape(B, S, -1)
        y_rwkv = _rwkv_branch(p, rwkv_mu[l], rwkv_w0[l], rwkv_w_up[l], rwkv_a0[l], rwkv_a_up[l],
                              rwkv_g_up[l], rwkv_k_k[l], rwkv_k_a[l], rwkv_r_k[l], rwkv_gn_g[l],
                              rwkv_gn_b[l], act)
        y_win = _window_branch(p, bias_tiles, win_sink[l], act)
        y_diff = _diff_branch(p, bias_tiles, rel_bias_t, diff_lambda[l], diff_subln_g[l], lambda_init, act)
        y_mem = _memory_branch(p, mem, mem_w_kv[l].astype(BF16), act)
        x1 = _merge(x.reshape(T, D), y_rwkv.reshape(T, -1), y_win.reshape(T, -1), y_diff.reshape(T, -1),
                    y_mem.reshape(T, -1), p2, w_branch[l].astype(BF16), w_out[l].astype(BF16),
                    ln1_g[l], ln1_b[l], alpha, 256).reshape(B, S, D)
        router_pad = jnp.concatenate([router[l], jnp.zeros((D, LANES - N_EXPERTS), F32)], axis=1)
        xe, slot, gate = _route(x1, router_pad, cap)
        ye = _expert_ffn(xe, gate, exp_w_gate, exp_w_up, exp_w_down, l, 512)
        x = _scatter_ln(x1, slot, ye, ln2_g[l], ln2_b[l], alpha, 512)
    return x
```

```python
import functools
import math

import jax
import jax.numpy as jnp
from jax import lax
from jax.experimental import pallas as pl
from jax.experimental.pallas import tpu as pltpu

F32 = jnp.float32
BF16 = jnp.bfloat16

HEAD_DIM = 64
LANES = 128
RWKV_HEADS = 16
RWKV_DIM = RWKV_HEADS * HEAD_DIM
DECAY_LORA = 64
ICLR_LORA = 64
GATE_LORA = 128
RWKV_GN_EPS = 64e-5
RWKV_COLS = 3 * RWKV_DIM + DECAY_LORA + ICLR_LORA + GATE_LORA
RWKV_CHUNK = 64
RWKV_UNROLL = 8
assert RWKV_CHUNK == HEAD_DIM
WIN_Q_HEADS = 16
WIN_KV_HEADS = 4
WIN_GROUP = WIN_Q_HEADS // WIN_KV_HEADS
WIN_Q_COLS = WIN_Q_HEADS * HEAD_DIM
WIN_KV_COLS = WIN_KV_HEADS * HEAD_DIM
WIN_COLS = WIN_Q_COLS + 2 * WIN_KV_COLS
WINDOW = 128
BLOCK = 128
DIFF_HEADS = 8
DIFF_V_DIM = 2 * HEAD_DIM
DIFF_QK_COLS = 2 * DIFF_HEADS * HEAD_DIM
DIFF_V_COLS = DIFF_HEADS * DIFF_V_DIM
DIFF_COLS = 2 * DIFF_QK_COLS + DIFF_V_COLS
DIFF_Q_TILE = 2 * BLOCK
DIFF_CHUNK_BLOCKS = 16
MEM_HEADS = 4
MEM_WIDTH = MEM_HEADS * HEAD_DIM
NUM_BUCKETS = 32
N_EXPERTS = 16
CAPACITY_FACTOR = 2
N_BRANCHES = 4
NEG_INF = -1e30
LN_EPS = 1e-5
VMEM_LIMIT = 56 * 1024 * 1024
IN_PROJ_TN = 512

OFF_WIN = RWKV_COLS
OFF_DIFF = OFF_WIN + WIN_COLS
OFF_MEM = OFF_DIFF + DIFF_COLS
OFF_GATE = OFF_MEM + MEM_WIDTH


def _cparams(*sem):
    return pltpu.CompilerParams(dimension_semantics=sem, vmem_limit_bytes=VMEM_LIMIT)


def _bdot(a, b):
    return jnp.dot(a.astype(BF16), b.astype(BF16), preferred_element_type=F32)


def _bdot_nt(a, b):
    return lax.dot_general(a.astype(BF16), b.astype(BF16), (((1,), (1,)), ((), ())),
                           preferred_element_type=F32)


def _split_dot(a, b_exact):
    hi = a.astype(BF16)
    lo = (a - hi.astype(F32)).astype(BF16)
    return (jnp.dot(hi, b_exact, preferred_element_type=F32)
            + jnp.dot(lo, b_exact, preferred_element_type=F32))


def _sigmoid(x):
    return 0.5 * jnp.tanh(0.5 * x) + 0.5


def _layer_norm(x, g, b):
    mu = jnp.mean(x, axis=-1, keepdims=True)
    xc = x - mu
    var = jnp.mean(xc * xc, axis=-1, keepdims=True)
    return xc * lax.rsqrt(var + LN_EPS) * g + b


def _inproj_shift_kernel(x_ref, w_ref, mu_ref, o_ref):
    acc = jnp.dot(x_ref[...], w_ref[...], preferred_element_type=F32)
    S = acc.shape[0]
    row = lax.broadcasted_iota(jnp.int32, (S, 1), 0)
    prev = jnp.where(row == 0, 0.0, pltpu.roll(acc, 1, 0))
    nxt = jnp.where(row == S - 1, 0.0, pltpu.roll(acc, S - 1, 0))
    out = acc + mu_ref[0:1, :] * (prev - acc) + mu_ref[1:2, :] * (nxt - acc)
    o_ref[...] = out.astype(o_ref.dtype)


def _inproj_plain_kernel(x_ref, w_ref, p_ref, o_ref):
    del p_ref
    o_ref[...] = jnp.dot(x_ref[...], w_ref[...], preferred_element_type=F32).astype(o_ref.dtype)


def _in_proj(x2, w, mu, seq, tn, out_dtype):
    m, k = x2.shape
    n = w.shape[1]
    shift_tiles = -(-mu.shape[1] // tn)
    mu_pad = jnp.concatenate([mu, jnp.zeros((2, shift_tiles * tn - mu.shape[1]), F32)], axis=1)
    x_spec = pl.BlockSpec((seq, k), lambda j, i: (i, 0))
    head = pl.pallas_call(
        _inproj_shift_kernel,
        grid=(shift_tiles, m // seq),
        in_specs=[x_spec, pl.BlockSpec((k, tn), lambda j, i: (0, j)), pl.BlockSpec((2, tn), lambda j, i: (0, j))],
        out_specs=pl.BlockSpec((seq, tn), lambda j, i: (i, j)),
        out_shape=jax.ShapeDtypeStruct((m, n), out_dtype),
        compiler_params=_cparams("arbitrary", "arbitrary"),
        name="in_proj_shift",
    )(x2, w, mu_pad)
    return pl.pallas_call(
        _inproj_plain_kernel,
        grid=(n // tn - shift_tiles, m // seq),
        in_specs=[x_spec, pl.BlockSpec((k, tn), lambda j, i: (0, j + shift_tiles)),
                  pl.BlockSpec(memory_space=pl.ANY)],
        out_specs=pl.BlockSpec((seq, tn), lambda j, i: (i, j + shift_tiles)),
        out_shape=jax.ShapeDtypeStruct((m, n), out_dtype),
        input_output_aliases={2: 0},
        compiler_params=_cparams("arbitrary", "arbitrary"),
        name="in_proj",
    )(x2, w, head)


def _bias_kernel(tab_ref, o_ref):
    h = pl.program_id(0)
    qi = lax.broadcasted_iota(jnp.int32, (BLOCK, 3 * BLOCK), 0)
    kj = lax.broadcasted_iota(jnp.int32, (BLOCK, 3 * BLOCK), 1)
    rel = kj - BLOCK - qi
    n = jnp.abs(rel)
    n2 = n * n
    large = jnp.full_like(n, NUM_BUCKETS // 4)
    for kpow in range(1, 8):
        large = large + jnp.where(n2 >= 64 * (2 ** kpow), 1, 0)
    bucket = jnp.where(rel > 0, NUM_BUCKETS // 2, 0) + jnp.where(n < NUM_BUCKETS // 4, n, large)
    acc = jnp.zeros((BLOCK, 3 * BLOCK), F32)
    for bkt in range(NUM_BUCKETS):
        acc = jnp.where(bucket == bkt, tab_ref[h, bkt], acc)
    o_ref[0] = acc


def _bias_tiles(rel_bias_t):
    nh = rel_bias_t.shape[0]
    return pl.pallas_call(
        _bias_kernel,
        grid=(nh,),
        in_specs=[pl.BlockSpec(memory_space=pltpu.SMEM)],
        out_specs=pl.BlockSpec((1, BLOCK, 3 * BLOCK), lambda h: (h, 0, 0)),
        out_shape=jax.ShapeDtypeStruct((nh, BLOCK, 3 * BLOCK), F32),
        compiler_params=_cparams("arbitrary"),
        name="t5_bias_tiles",
    )(rel_bias_t)


def _rwkv_kernel(pr_ref, pk_ref, pv_ref, pl_ref,
                 w0_ref, wup_ref, a0_ref, aup_ref, gup_ref, kk_ref, ka_ref, rk_ref,
                 gng_ref, gnb_ref, o_ref,
                 kt_s, rt_s, kh_s, bh_s, cum_s, v_s, y_s, m_s, n_s, *, seq):
    S = seq
    C = RWKV_CHUNK
    nchunk = S // C
    row = lax.broadcasted_iota(jnp.int32, (S, 1), 0)
    lane = lax.broadcasted_iota(jnp.int32, (1, LANES), 1)
    head0 = lane < HEAD_DIM
    ri = lax.broadcasted_iota(jnp.int32, (LANES, LANES), 0)
    ci = lax.broadcasted_iota(jnp.int32, (LANES, LANES), 1)
    same_head = (ri < HEAD_DIM) == (ci < HEAD_DIM)
    eye = ri == ci
    seg_ones = jnp.where(same_head, 1.0, 0.0).astype(BF16)

    GRP = min(S, 4 * C)
    gi = lax.broadcasted_iota(jnp.int32, (GRP, GRP), 0)
    gj = lax.broadcasted_iota(jnp.int32, (GRP, GRP), 1)
    same_chunk = (gi // C) == (gj // C)
    scan_tri = (jnp.where(same_chunk & (gj <= gi), 1.0, 0.0).astype(BF16),
                jnp.where(same_chunk & (gj >= gi), 1.0, 0.0).astype(BF16))

    r = pr_ref[0].astype(F32)
    k = pk_ref[0].astype(F32)
    v = pv_ref[0].astype(F32)
    lo = pl_ref[0].astype(F32)
    l1 = lo[:, :LANES]
    l2 = lo[:, LANES:]
    g = _bdot(_sigmoid(l2), gup_ref[...])
    kkr = k * kk_ref[...]
    kk = kkr * lax.rsqrt(jnp.maximum(_split_dot(kkr * kkr, seg_ones), 1e-12))
    v_s[...] = v
    tanh_l1 = jnp.tanh(l1)
    kd_sum = jnp.zeros((S, LANES), F32)

    for z in range(2):
        wl = w0_ref[z:z + 1, :] + _bdot(tanh_l1, wup_ref[z])
        logw = -math.exp(-0.5) * _sigmoid(wl)
        a = _sigmoid(a0_ref[z:z + 1, :] + _bdot(l1, aup_ref[z]))
        kd = k * (1.0 + (a - 1.0) * ka_ref[...])
        bb = a * kk
        kd_sum = kd_sum + kd
        lw_hi = logw.astype(BF16)
        lw_lo = (logw - lw_hi.astype(F32)).astype(BF16)
        tri = scan_tri[z]
        cum = jnp.concatenate(
            [jnp.dot(tri, lw_hi[g0:g0 + GRP], preferred_element_type=F32)
             + jnp.dot(tri, lw_lo[g0:g0 + GRP], preferred_element_type=F32)
             for g0 in range(0, S, GRP)], axis=0)
        e_out = jnp.exp(-cum)
        kt_s[z] = kk * jnp.exp(cum - logw)
        rt_s[z] = r * jnp.exp(cum)
        kh_s[z] = kd * e_out
        bh_s[z] = bb * e_out
        cum_s[z] = cum

    Q = 4 * HEAD_DIM
    qi = lax.broadcasted_iota(jnp.int32, (Q, Q), 0)
    qj = lax.broadcasted_iota(jnp.int32, (Q, Q), 1)
    same_blk = (qi // HEAD_DIM) == (qj // HEAD_DIM)
    ct = lax.broadcasted_iota(jnp.int32, (C, Q), 0)
    cj = lax.broadcasted_iota(jnp.int32, (C, Q), 1)
    dt = ((cj % C) - ct) * jnp.where(cj < 2 * C, 1, -1)
    strict_m = dt < 0
    incl_m = dt <= 0

    def both(ref, rws2):
        return jnp.concatenate([ref[0, rws2[0], :], ref[1, rws2[1], :]], axis=1)

    def bd(a):
        return jnp.where(same_blk, jnp.concatenate([a.astype(BF16)] * 4, axis=0), 0.0)

    def dot(a, b):
        return jnp.dot(a.astype(BF16), b, preferred_element_type=F32)

    def dot_nt(a, b):
        return lax.dot_general(a.astype(BF16), b, (((1,), (1,)), ((), ())), preferred_element_type=F32)

    U = min(RWKV_UNROLL, nchunk)

    def chunk_of(g, u, z):
        c = g * U + u
        return c if z == 0 else nchunk - 1 - c

    def rows_of(c):
        return pl.ds(c * C, C) if isinstance(c, int) else pl.ds(pl.multiple_of(c * C, C), C)

    def phase1(g, between=lambda k: None):
        ur = range(U)
        cs = [(chunk_of(g, u, 0), chunk_of(g, u, 1)) for u in ur]
        rws = [(rows_of(c0_), rows_of(c1_)) for c0_, c1_ in cs]
        kt = [both(kt_s, r_) for r_ in rws]
        rt = [both(rt_s, r_) for r_ in rws]
        vbd = [bd(jnp.concatenate([v_s[r_[0], :], v_s[r_[1], :]], axis=1)) for r_ in rws]
        lhs = [jnp.concatenate([kt[u], rt[u]], axis=0) for u in ur]
        gb = [dot_nt(lhs[u], bd(both(bh_s, rws[u]))) for u in ur]
        gk = [dot_nt(lhs[u], bd(both(kh_s, rws[u]))) for u in ur]
        between(0)
        a_rb = [jnp.where(incl_m, gb[u][C:], 0.0) for u in ur]
        a_kr = [jnp.concatenate([jnp.where(strict_m, gk[u][:C], 0.0),
                                 jnp.where(incl_m, gk[u][C:], 0.0)], axis=0) for u in ur]
        yy = [jnp.where(strict_m, -gb[u][:C], 0.0) for u in ur]
        x = list(yy)
        p = [dot(yy[u], bd(yy[u])) for u in ur]
        yy = p
        between(1)
        for step in range(1, 6):
            last = step == 5
            p = [dot(x[u] if last else jnp.concatenate([yy[u], x[u]], axis=0), bd(yy[u])) for u in ur]
            x = [x[u] + yy[u] + (p[u] if last else p[u][C:]) for u in ur]
            if not last:
                yy = [p[u][:C] for u in ur]
            between(1 + step)
        avr = [dot(a_kr[u], vbd[u]) for u in ur]
        between(7)
        tz = []
        for u in ur:
            av = avr[u][:C]
            zz = jnp.concatenate([kt[u], av], axis=1)
            tz.append(zz + dot(x[u], jnp.concatenate([bd(kt[u]), bd(av)], axis=1)))
        cor = [dot(a_rb[u], jnp.concatenate([bd(tz[u][:, :Q]), bd(tz[u][:, Q:])], axis=1)) for u in ur]
        for u in ur:
            rr = rt[u] - cor[u][:, :Q]
            yl = avr[u][C:] - cor[u][:, Q:]
            for z in range(2):
                zl = slice(z * LANES, (z + 1) * LANES)
                rz = rws[u][z]
                y_s[z, rz, :] = yl[:, zl]
                rt_s[z, rz, :] = rr[:, zl]
                far = C - 1 if z == 0 else 0
                pt = jnp.exp(cum_s[z, rz, :][far:far + 1, :])
                ends = jnp.concatenate([bh_s[z, rz, :] * pt, kh_s[z, rz, :] * pt], axis=0)
                rhs = jnp.concatenate(
                    [jnp.concatenate([-tz[u][:, zl], -tz[u][:, Q + z * LANES:Q + (z + 1) * LANES]], axis=1),
                     jnp.concatenate([jnp.zeros((C, LANES), F32), v_s[rz, :]], axis=1)], axis=0)
                mn = _bdot(ends.T, rhs)
                m_s[z, cs[u][z]] = jnp.where(eye, jnp.broadcast_to(pt, (LANES, LANES)), 0.0) \
                    + jnp.where(same_head, mn[:, :LANES], 0.0)
                n_s[z, cs[u][z]] = jnp.where(same_head, mn[:, LANES:], 0.0)

    def carry_step(g, u, hs):
        for z in range(2):
            c = chunk_of(g, u, z)
            rows = rows_of(c)
            y_s[z, rows, :] = y_s[z, rows, :] + dot(rt_s[z, rows, :], hs[z])
            hs[z] = (dot(m_s[z, c], hs[z]) + n_s[z, c]).astype(BF16)

    def phase2(g, hs):
        hs = list(hs)
        for u in range(U):
            carry_step(g, u, hs)
        return tuple(hs)

    def lagged(g, hs):
        hs = list(hs)
        steps_per_gap = -(-U // 8)

        def between(k):
            for u in range(k * steps_per_gap, min(U, (k + 1) * steps_per_gap)):
                carry_step(g - 1, u, hs)

        phase1(g, between)
        return tuple(hs)

    ngroup = nchunk // U
    h0 = jnp.zeros((LANES, LANES), BF16)
    phase1(0)
    hs = lax.fori_loop(1, ngroup, lagged, (h0, h0))
    phase2(ngroup - 1, hs)

    y = y_s[0] + y_s[1]
    inv_n = 1.0 / HEAD_DIM
    mu_y = _split_dot(y, seg_ones) * inv_n
    yc = y - mu_y
    var_y = _split_dot(yc * yc, seg_ones) * inv_n
    yn = yc * lax.rsqrt(var_y + RWKV_GN_EPS) * gng_ref[...] + gnb_ref[...]
    bonus = _split_dot(r * kd_sum * rk_ref[...], seg_ones) * v
    o_ref[0] = ((yn + bonus) * g).astype(o_ref.dtype)


def _rwkv_branch(p, w0, w_up, a0, a_up, g_up, k_k, k_a, r_k, gn_g, gn_b, out_dtype):
    B, S, _ = p.shape
    npair = RWKV_DIM // LANES
    zpad = jnp.zeros((2, LANES - DECAY_LORA, RWKV_DIM), F32)
    wup_p = jnp.concatenate([w_up, zpad], axis=1).astype(BF16)
    aup_p = jnp.concatenate([zpad, a_up], axis=1).astype(BF16)
    row = lambda a: a.reshape(1, RWKV_DIM)
    lora_blk = 3 * RWKV_DIM // (2 * LANES)

    def pspec(off):
        return pl.BlockSpec((1, S, LANES), lambda b, j: (b, 0, off + j))

    def vspec(rows, off=0):
        return pl.BlockSpec((rows, LANES), lambda b, j: (0, off + j))

    scratch = [pltpu.VMEM((2, S, LANES), F32) for _ in range(5)] + [
        pltpu.VMEM((S, LANES), F32), pltpu.VMEM((2, S, LANES), F32),
        pltpu.VMEM((2, S // RWKV_CHUNK, LANES, LANES), F32),
        pltpu.VMEM((2, S // RWKV_CHUNK, LANES, LANES), F32)]
    return pl.pallas_call(
        functools.partial(_rwkv_kernel, seq=S),
        grid=(B, npair),
        in_specs=[pspec(0), pspec(npair), pspec(2 * npair),
                  pl.BlockSpec((1, S, 2 * LANES), lambda b, j: (b, 0, lora_blk)),
                  vspec(2),
                  pl.BlockSpec((2, LANES, LANES), lambda b, j: (0, 0, j)),
                  vspec(2),
                  pl.BlockSpec((2, LANES, LANES), lambda b, j: (0, 0, j)),
                  vspec(GATE_LORA), vspec(1), vspec(1), vspec(1), vspec(1), vspec(1)],
        out_specs=pl.BlockSpec((1, S, LANES), lambda b, j: (b, 0, j)),
        out_shape=jax.ShapeDtypeStruct((B, S, RWKV_DIM), out_dtype),
        scratch_shapes=scratch,
        compiler_params=_cparams("arbitrary", "arbitrary"),
        name="rwkv7_scan",
    )(p, p, p, p, w0, wup_p, a0, aup_p, g_up.astype(BF16),
      row(k_k), row(k_a), row(r_k), row(gn_g), row(gn_b))


def _win_kernel(q_ref, k_ref, v_ref, bias_ref, sink_ref, o_ref, kp_s, vp_s, *, seq):
    S = seq
    G = WIN_GROUP
    hk = pl.program_id(1)
    odd = (hk % 2) == 1
    kfull = k_ref[0].astype(F32)
    vfull = v_ref[0].astype(F32)
    zpad = jnp.zeros((BLOCK, HEAD_DIM), kp_s.dtype)
    kp_s[0:BLOCK, :] = zpad
    kp_s[BLOCK + S:, :] = zpad
    vp_s[0:BLOCK, :] = zpad
    vp_s[BLOCK + S:, :] = zpad
    kp_s[BLOCK:BLOCK + S, :] = jnp.where(odd, kfull[:, HEAD_DIM:], kfull[:, :HEAD_DIM]).astype(kp_s.dtype)
    vp_s[BLOCK:BLOCK + S, :] = jnp.where(odd, vfull[:, HEAD_DIM:], vfull[:, :HEAD_DIM]).astype(vp_s.dtype)
    bias4 = bias_ref[...].reshape(G * BLOCK, 3 * BLOCK)
    rowi = lax.broadcasted_iota(jnp.int32, (G * BLOCK, 3 * BLOCK), 0)
    kj = lax.broadcasted_iota(jnp.int32, (G * BLOCK, 3 * BLOCK), 1)
    rel = kj - BLOCK - (rowi % BLOCK)
    in_band = jnp.abs(rel) <= WINDOW
    rgrp = lax.broadcasted_iota(jnp.int32, (G * BLOCK, 1), 0) // BLOCK
    sink4 = jnp.zeros((G * BLOCK, 1), F32)
    for gi in range(G):
        sink4 = jnp.where(rgrp == gi, sink_ref[hk * G + gi], sink4)

    def body(n, carry):
        r0 = pl.multiple_of(n * BLOCK, BLOCK)
        qb = q_ref[0, pl.ds(r0, BLOCK), :].astype(F32)
        q4 = jnp.concatenate([qb[:, gi * HEAD_DIM:(gi + 1) * HEAD_DIM] for gi in range(G)], axis=0)
        kb = kp_s[pl.ds(r0, 3 * BLOCK), :]
        vb = vp_s[pl.ds(r0, 3 * BLOCK), :]
        s = _bdot_nt(q4, kb) * (HEAD_DIM ** -0.5)
        kpos = r0 + kj - BLOCK
        valid = in_band & (kpos >= 0) & (kpos < S)
        s = jnp.where(valid, s + bias4, NEG_INF)
        m = jnp.maximum(jnp.max(s, axis=-1, keepdims=True), sink4)
        e = jnp.exp(s - m)
        den = jnp.sum(e, axis=-1, keepdims=True) + jnp.exp(sink4 - m)
        o4 = _bdot(e / den, vb)
        o_ref[0, pl.ds(r0, BLOCK), :] = jnp.concatenate(
            [o4[gi * BLOCK:(gi + 1) * BLOCK] for gi in range(G)], axis=1).astype(o_ref.dtype)
        return carry

    lax.fori_loop(0, S // BLOCK, body, 0)


def _window_branch(p, bias_tiles, sink, out_dtype):
    B, S, _ = p.shape
    gw = WIN_GROUP * HEAD_DIM
    qoff = OFF_WIN // gw
    koff = (OFF_WIN + WIN_Q_COLS) // LANES
    voff = (OFF_WIN + WIN_Q_COLS + WIN_KV_COLS) // LANES
    return pl.pallas_call(
        functools.partial(_win_kernel, seq=S),
        grid=(B, WIN_KV_HEADS),
        in_specs=[pl.BlockSpec((1, S, gw), lambda b, h: (b, 0, qoff + h)),
                  pl.BlockSpec((1, S, LANES), lambda b, h: (b, 0, koff + h // 2)),
                  pl.BlockSpec((1, S, LANES), lambda b, h: (b, 0, voff + h // 2)),
                  pl.BlockSpec((WIN_GROUP, BLOCK, 3 * BLOCK), lambda b, h: (h, 0, 0)),
                  pl.BlockSpec(memory_space=pltpu.SMEM)],
        out_specs=pl.BlockSpec((1, S, gw), lambda b, h: (b, 0, h)),
        out_shape=jax.ShapeDtypeStruct((B, S, WIN_Q_COLS), out_dtype),
        scratch_shapes=[pltpu.VMEM((S + 2 * BLOCK, HEAD_DIM), F32),
                        pltpu.VMEM((S + 2 * BLOCK, HEAD_DIM), F32)],
        compiler_params=_cparams("arbitrary", "arbitrary"),
        name="window_attn",
    )(p, p, p, bias_tiles, sink)


def _diff_kernel(q_ref, k_ref, v_ref, bandt_ref, tab_ref, lam_ref, g_ref, o_ref, ka_s, vt_s, *,
                 seq, lambda_init):
    S = seq
    NB = S // BLOCK
    CB = min(DIFF_CHUNK_BLOCKS, NB)
    h = pl.program_id(1)
    it = pl.program_id(2)
    halves = q_ref.shape[1] // BLOCK
    log2e = math.log2(math.e)
    lane = lax.broadcasted_iota(jnp.int32, (1, LANES), 1)
    c0 = lane < HEAD_DIM

    @pl.when(it == 0)
    def _():
        jblk = lax.broadcasted_iota(jnp.int32, (S, LANES), 0) // BLOCK
        lj = lax.broadcasted_iota(jnp.int32, (S, LANES), 1)
        ka_s[:, :LANES] = k_ref[0].astype(BF16)
        ka_s[:, LANES:] = jnp.where((lj < 2 * NB) & (lj % NB == jblk), 1.0, 0.0).astype(BF16)
        for blk in range(NB):
            vt_s[blk] = v_ref[0, blk * BLOCK:(blk + 1) * BLOCK, :].astype(F32).T.astype(BF16)

    far_l = tab_ref[WIN_Q_HEADS + h, NUM_BUCKETS // 2 - 1] * log2e
    far_r = tab_ref[WIN_Q_HEADS + h, NUM_BUCKETS - 1] * log2e
    band_t = bandt_ref[0] * log2e
    lam = lam_ref[...].astype(F32)
    lam_full = (jnp.exp(jnp.sum(lam[0:1] * lam[1:2], axis=-1, keepdims=True))
                - jnp.exp(jnp.sum(lam[2:3] * lam[3:4], axis=-1, keepdims=True)) + lambda_init)
    ibs = [it * halves + hf for hf in range(halves)]
    zblk = jnp.zeros((BLOCK, BLOCK), F32)
    qq, bands = [], []
    for hf in range(halves):
        q = q_ref[0, hf * BLOCK:(hf + 1) * BLOCK, :].astype(F32) * (log2e * HEAD_DIM ** -0.5)
        kb = lane % NB
        val = jnp.where(kb < ibs[hf] - 1, far_l, jnp.where(kb > ibs[hf] + 1, far_r, 0.0))
        val = jnp.where(lane < 2 * NB, val, 0.0)
        hi = val.astype(BF16)
        lo = (val - hi.astype(F32)).astype(BF16)
        bcols = jnp.broadcast_to(jnp.where(lane < NB, hi, lo), (BLOCK, LANES))
        qq.append(jnp.concatenate(
            [jnp.concatenate([jnp.where(c0, q, 0.0).astype(BF16), bcols], axis=1),
             jnp.concatenate([jnp.where(c0, 0.0, q).astype(BF16), bcols], axis=1)], axis=0))
        tiles = [jnp.where(ibs[hf] > 0, band_t[:BLOCK], 0.0), band_t[BLOCK:2 * BLOCK],
                 jnp.where(ibs[hf] < NB - 1, band_t[2 * BLOCK:], 0.0)]
        col = jnp.concatenate([zblk] * hf + tiles + [zblk] * (CB - 3 - hf), axis=0)
        bands.append(jnp.concatenate([col, col], axis=1))

    m_run, l_run, acc = [None] * halves, [None] * halves, [None] * halves
    for r in range(NB // CB):
        kbs = [lax.rem(it * halves - 1 + CB * r + u + NB, NB) for u in range(CB)]
        kc = jnp.concatenate([ka_s[pl.ds(pl.multiple_of(kb * BLOCK, BLOCK), BLOCK), :] for kb in kbs], axis=0)
        vct = jnp.concatenate([vt_s[kb] for kb in kbs], axis=1)
        s = [lax.dot_general(kc, qq[hf], (((1,), (1,)), ((), ())), preferred_element_type=F32)
             for hf in range(halves)]
        if r == 0:
            s = [s[hf] + bands[hf] for hf in range(halves)]
        m_loc = [jnp.max(s[hf], axis=0, keepdims=True) for hf in range(halves)]
        for hf in range(halves):
            if r == 0:
                m_new = m_loc[hf]
                e = jnp.exp2(s[hf] - m_new)
                l_run[hf] = jnp.sum(e, axis=0, keepdims=True)
                acc[hf] = jnp.dot(vct, e.astype(BF16), preferred_element_type=F32)
            else:
                m_new = jnp.maximum(m_run[hf], m_loc[hf])
                alpha = jnp.exp2(m_run[hf] - m_new)
                e = jnp.exp2(s[hf] - m_new)
                l_run[hf] = alpha * l_run[hf] + jnp.sum(e, axis=0, keepdims=True)
                acc[hf] = alpha * acc[hf] + jnp.dot(vct, e.astype(BF16), preferred_element_type=F32)
            m_run[hf] = m_new
    for hf in range(halves):
        rinv = 1.0 / l_run[hf]
        o_t = acc[hf][:, :BLOCK] * rinv[:, :BLOCK] - acc[hf][:, BLOCK:] * (lam_full * rinv[:, BLOCK:])
        o = o_t.T
        o = o * lax.rsqrt(jnp.mean(o * o, axis=-1, keepdims=True) + 1e-5) * g_ref[...] * (1.0 - lambda_init)
        o_ref[0, hf * BLOCK:(hf + 1) * BLOCK, :] = o.astype(o_ref.dtype)


def _diff_branch(p, band_tiles, rel_bias_t, lam, subln_g, lambda_init, out_dtype):
    B, S, _ = p.shape
    qoff = OFF_DIFF // LANES
    koff = (OFF_DIFF + DIFF_QK_COLS) // LANES
    voff = (OFF_DIFF + 2 * DIFF_QK_COLS) // LANES
    tq = DIFF_Q_TILE
    nb = S // BLOCK
    assert 2 * nb <= LANES and S % tq == 0 and nb % min(DIFF_CHUNK_BLOCKS, nb) == 0 and nb >= 4
    return pl.pallas_call(
        functools.partial(_diff_kernel, seq=S, lambda_init=lambda_init),
        grid=(B, DIFF_HEADS, S // tq),
        in_specs=[pl.BlockSpec((1, tq, LANES), lambda b, h, i: (b, i, qoff + h)),
                  pl.BlockSpec((1, S, LANES), lambda b, h, i: (b, 0, koff + h)),
                  pl.BlockSpec((1, S, LANES), lambda b, h, i: (b, 0, voff + h)),
                  pl.BlockSpec((1, 3 * BLOCK, BLOCK), lambda b, h, i: (WIN_Q_HEADS + h, 0, 0)),
                  pl.BlockSpec(memory_space=pltpu.SMEM),
                  pl.BlockSpec((4, HEAD_DIM), lambda b, h, i: (0, 0)),
                  pl.BlockSpec((1, DIFF_V_DIM), lambda b, h, i: (0, 0))],
        out_specs=pl.BlockSpec((1, tq, LANES), lambda b, h, i: (b, i, h)),
        out_shape=jax.ShapeDtypeStruct((B, S, DIFF_V_COLS), out_dtype),
        scratch_shapes=[pltpu.VMEM((S, 2 * LANES), BF16),
                        pltpu.VMEM((S // BLOCK, DIFF_V_DIM, BLOCK), BF16)],
        compiler_params=_cparams("arbitrary", "arbitrary", "arbitrary"),
        name="diff_attn",
    )(p, p, p, jnp.swapaxes(band_tiles, 1, 2), rel_bias_t, lam, subln_g.reshape(1, DIFF_V_DIM))


def _mem_kernel(q_ref, mem_ref, wkv_ref, o_ref):
    kv = _bdot(mem_ref[0], wkv_ref[...])
    kmat = kv[:, :MEM_WIDTH]
    vmat = kv[:, MEM_WIDTH:]
    q = q_ref[0].astype(F32) * (HEAD_DIM ** -0.5)
    lane = lax.broadcasted_iota(jnp.int32, (1, MEM_WIDTH), 1) // HEAD_DIM
    out = jnp.zeros(q.shape, F32)
    for hh in range(MEM_HEADS):
        s = _bdot_nt(jnp.where(lane == hh, q, 0.0), kmat)
        m = jnp.max(s, axis=-1, keepdims=True)
        e = jnp.exp(s - m)
        a = e / jnp.sum(e, axis=-1, keepdims=True)
        out = jnp.where(lane == hh, _bdot(a, vmat), out)
    o_ref[0] = out.astype(o_ref.dtype)


def _memory_branch(p, mem, w_kv, out_dtype):
    B, S, _ = p.shape
    nm, d = mem.shape[1], mem.shape[2]
    return pl.pallas_call(
        _mem_kernel,
        grid=(B,),
        in_specs=[pl.BlockSpec((1, S, MEM_WIDTH), lambda b: (b, 0, OFF_MEM // MEM_WIDTH)),
                  pl.BlockSpec((1, nm, d), lambda b: (b, 0, 0)),
                  pl.BlockSpec((d, 2 * MEM_WIDTH), lambda b: (0, 0))],
        out_specs=pl.BlockSpec((1, S, MEM_WIDTH), lambda b: (b, 0, 0)),
        out_shape=jax.ShapeDtypeStruct((B, S, MEM_WIDTH), out_dtype),
        compiler_params=_cparams("arbitrary"),
        name="memory_attn",
    )(p, mem, w_kv)


def _merge_kernel(x_ref, yr_ref, yw_ref, yd_ref, ym_ref, pg_ref, wb_ref, wo_ref, g_ref, b_ref,
                  o_ref, *, d_model, alpha):
    D = d_model
    ys = (yr_ref[...], yw_ref[...], yd_ref[...], ym_ref[...])
    off = 0
    merged = jnp.zeros((x_ref.shape[0], D), F32)
    for bi, yb in enumerate(ys):
        rows = yb.shape[1]
        proj = _bdot(yb, wb_ref[off:off + rows, :])
        merged = merged + _sigmoid(pg_ref[:, bi * D:(bi + 1) * D].astype(F32)) * proj
        off += rows
    z = alpha * x_ref[...] + _bdot(merged, wo_ref[...])
    o_ref[...] = _layer_norm(z, g_ref[...], b_ref[...])


def _merge(x2, y_rwkv, y_win, y_diff, y_mem, p2, w_branch, w_out, ln_g, ln_b, alpha, tm):
    T, D = x2.shape
    gate_blk = OFF_GATE // (N_BRANCHES * D)
    full = lambda a: pl.BlockSpec(a.shape, lambda i: (0, 0))
    tile = lambda w: pl.BlockSpec((tm, w), lambda i: (i, 0))
    return pl.pallas_call(
        functools.partial(_merge_kernel, d_model=D, alpha=alpha),
        grid=(T // tm,),
        in_specs=[tile(D), tile(y_rwkv.shape[1]), tile(y_win.shape[1]), tile(y_diff.shape[1]),
                  tile(y_mem.shape[1]),
                  pl.BlockSpec((tm, N_BRANCHES * D), lambda i: (i, gate_blk)),
                  full(w_branch), full(w_out),
                  pl.BlockSpec((1, D), lambda i: (0, 0)), pl.BlockSpec((1, D), lambda i: (0, 0))],
        out_specs=tile(D),
        out_shape=jax.ShapeDtypeStruct((T, D), F32),
        compiler_params=_cparams("arbitrary"),
        name="merge_out_ln",
    )(x2, y_rwkv, y_win, y_diff, y_mem, p2, w_branch, w_out, ln_g.reshape(1, D), ln_b.reshape(1, D))


def _route_kernel(x_ref, wr_ref, xe_ref, slot_ref, gate_ref, xb_s, lg_s, *, seq, cap):
    S = seq
    E = N_EXPERTS
    TR = min(S, 512)
    wr = wr_ref[...]
    w_hi = wr.astype(BF16)
    w_md = (wr - w_hi.astype(F32)).astype(BF16)
    dot = lambda a, b: jnp.dot(a, b, preferred_element_type=F32)

    def logit_tile(t, carry):
        rows = pl.ds(pl.multiple_of(t * TR, TR), TR)
        x = x_ref[0, rows, :]
        x_hi = x.astype(BF16)
        x_r1 = x - x_hi.astype(F32)
        x_md = x_r1.astype(BF16)
        x_lo = (x_r1 - x_md.astype(F32)).astype(BF16)
        xb_s[rows, :] = x_hi
        lg_s[rows, :] = (dot(x_hi, w_hi) + (dot(x_hi, w_md) + dot(x_md, w_hi))
                         + (dot(x_md, w_md) + dot(x_lo, w_hi)))
        return carry

    lax.fori_loop(0, S // TR, logit_tile, 0)
    col = lax.broadcasted_iota(jnp.int32, (1, LANES), 1)
    logits = jnp.where(col < E, lg_s[...], NEG_INF)
    mx = jnp.max(logits, axis=-1, keepdims=True)
    ex = jnp.exp(logits - mx)
    aff = ex / jnp.sum(ex, axis=-1, keepdims=True)
    aff_t = aff.T[0:E, :]
    bits = pltpu.bitcast(aff_t, jnp.int32)

    def bis(_, carry):
        lo_b, step = carry
        cand = lo_b + step
        cnt = jnp.sum(jnp.where(bits >= cand, 1.0, 0.0), axis=-1, keepdims=True)
        return jnp.where(cnt >= cap, cand, lo_b), step // 2

    lo0 = jnp.zeros((E, 1), jnp.int32)
    thr, _ = lax.fori_loop(0, 31, bis, (lo0, jnp.full((E, 1), 1 << 30, jnp.int32)))
    above = bits > thr
    tie = bits == thr
    n_above = jnp.sum(jnp.where(above, 1.0, 0.0), axis=-1, keepdims=True)
    PB = min(S, 256)
    tri = (lax.broadcasted_iota(jnp.int32, (PB, PB), 0)
           < lax.broadcasted_iota(jnp.int32, (PB, PB), 1)).astype(BF16)

    def prefix_count(mask_f):
        parts, run = [], jnp.zeros((E, 1), F32)
        for blk in range(S // PB):
            seg = mask_f[:, blk * PB:(blk + 1) * PB]
            parts.append(jnp.dot(seg.astype(BF16), tri, preferred_element_type=F32) + run)
            run = run + jnp.sum(seg, axis=-1, keepdims=True)
        return jnp.concatenate(parts, axis=1)

    tie_rank = prefix_count(jnp.where(tie, 1.0, 0.0))
    sel = above | (tie & (tie_rank < cap - n_above))
    pos = prefix_count(jnp.where(sel, 1.0, 0.0))
    slot = jnp.where(sel, pos, -1.0)
    slot_pad = jnp.concatenate([slot, jnp.full((LANES - E, S), -1.0, F32)], axis=0)
    slot_ref[0] = slot_pad.T
    ci = lax.broadcasted_iota(jnp.int32, (cap, S), 0).astype(F32)
    x_hi = xb_s[...]
    for e in range(E):
        onehot = slot[e:e + 1, :] == ci
        xe_ref[e, 0] = jnp.dot(jnp.where(onehot, 1.0, 0.0).astype(BF16), x_hi,
                               preferred_element_type=F32).astype(xe_ref.dtype)
        gsel = jnp.sum(jnp.where(onehot, aff_t[e:e + 1, :], 0.0), axis=-1, keepdims=True)
        gate_ref[e, 0] = jnp.broadcast_to(gsel, (cap, LANES))


def _route(x3, router_pad, cap):
    B, S, D = x3.shape
    E = N_EXPERTS
    return pl.pallas_call(
        functools.partial(_route_kernel, seq=S, cap=cap),
        grid=(B,),
        in_specs=[pl.BlockSpec((1, S, D), lambda b: (b, 0, 0)),
                  pl.BlockSpec((D, LANES), lambda b: (0, 0))],
        out_specs=[pl.BlockSpec((E, 1, cap, D), lambda b: (0, b, 0, 0)),
                   pl.BlockSpec((1, S, LANES), lambda b: (b, 0, 0)),
                   pl.BlockSpec((E, 1, cap, LANES), lambda b: (0, b, 0, 0))],
        out_shape=[jax.ShapeDtypeStruct((E, B, cap, D), BF16),
                   jax.ShapeDtypeStruct((B, S, LANES), F32),
                   jax.ShapeDtypeStruct((E, B, cap, LANES), F32)],
        scratch_shapes=[pltpu.VMEM((S, D), BF16), pltpu.VMEM((S, LANES), F32)],
        compiler_params=_cparams("arbitrary"),
        name="moe_route_gather",
    )(x3, router_pad)


def _ffn_kernel(xe_ref, g_ref, wg_ref, wu_ref, wd_ref, ye_ref, acc_s):
    f = pl.program_id(1)
    nb, cap, D = xe_ref.shape[1], xe_ref.shape[2], xe_ref.shape[3]
    xe = xe_ref[0].reshape(nb * cap, D)
    hg = jnp.dot(xe, wg_ref[0, 0].astype(BF16), preferred_element_type=F32)
    hu = jnp.dot(xe, wu_ref[0, 0].astype(BF16), preferred_element_type=F32)
    hid = (hg * _sigmoid(hg)) * hu
    part = _bdot(hid, wd_ref[0, 0])

    @pl.when(f == 0)
    def _():
        acc_s[...] = part

    @pl.when(f > 0)
    def _():
        acc_s[...] = acc_s[...] + part

    @pl.when(f == pl.num_programs(1) - 1)
    def _():
        gate = g_ref[0].reshape(nb * cap, LANES)[:, 0:1]
        ye_ref[0] = (acc_s[...] * gate).reshape(nb, cap, D).astype(ye_ref.dtype)


def _expert_ffn(xe, gate, w_gate, w_up, w_down, layer, tf):
    E, B, cap, D = xe.shape
    F = w_gate.shape[-1]
    return pl.pallas_call(
        _ffn_kernel,
        grid=(E, F // tf),
        in_specs=[pl.BlockSpec((1, B, cap, D), lambda e, f: (e, 0, 0, 0)),
                  pl.BlockSpec((1, B, cap, LANES), lambda e, f: (e, 0, 0, 0)),
                  pl.BlockSpec((1, 1, D, tf), lambda e, f: (layer, e, 0, f)),
                  pl.BlockSpec((1, 1, D, tf), lambda e, f: (layer, e, 0, f)),
                  pl.BlockSpec((1, 1, tf, D), lambda e, f: (layer, e, f, 0))],
        out_specs=pl.BlockSpec((1, B, cap, D), lambda e, f: (e, 0, 0, 0)),
        out_shape=jax.ShapeDtypeStruct((E, B, cap, D), BF16),
        scratch_shapes=[pltpu.VMEM((B * cap, D), F32)],
        compiler_params=_cparams("arbitrary", "arbitrary"),
        name="moe_expert_ffn",
    )(xe, gate, w_gate, w_up, w_down)


def _scatter_kernel(x_ref, slot_ref, ye_ref, g_ref, b_ref, o_ref, *, cap, alpha):
    ts = x_ref.shape[1]
    slot = slot_ref[0]
    ci = lax.broadcasted_iota(jnp.int32, (ts, cap), 1).astype(F32)
    acc = jnp.zeros((ts, x_ref.shape[2]), F32)
    for e in range(N_EXPERTS):
        onehot = slot[:, e:e + 1] == ci
        acc = acc + jnp.dot(jnp.where(onehot, 1.0, 0.0).astype(BF16), ye_ref[e, 0],
                            preferred_element_type=F32)
    o_ref[0] = _layer_norm(alpha * x_ref[0] + acc, g_ref[...], b_ref[...])


def _scatter_ln(x3, slot, ye, ln_g, ln_b, alpha, ts):
    B, S, D = x3.shape
    E, _, cap, _ = ye.shape
    return pl.pallas_call(
        functools.partial(_scatter_kernel, cap=cap, alpha=alpha),
        grid=(B, S // ts),
        in_specs=[pl.BlockSpec((1, ts, D), lambda b, i: (b, i, 0)),
                  pl.BlockSpec((1, ts, LANES), lambda b, i: (b, i, 0)),
                  pl.BlockSpec((E, 1, cap, D), lambda b, i: (0, b, 0, 0)),
                  pl.BlockSpec((1, D), lambda b, i: (0, 0)),
                  pl.BlockSpec((1, D), lambda b, i: (0, 0))],
        out_specs=pl.BlockSpec((1, ts, D), lambda b, i: (b, i, 0)),
        out_shape=jax.ShapeDtypeStruct((B, S, D), F32),
        compiler_params=_cparams("arbitrary", "arbitrary"),
        name="moe_scatter_ln",
    )(x3, slot, ye, ln_g.reshape(1, D), ln_b.reshape(1, D))


def kernel(x, mem, rel_bias, w_in, rwkv_mu, rwkv_w0, rwkv_w_up, rwkv_a0, rwkv_a_up, rwkv_g_up, rwkv_k_k, rwkv_k_a, rwkv_r_k, rwkv_gn_g, rwkv_gn_b, win_sink, diff_lambda, diff_subln_g, mem_w_kv, w_branch, w_out, ln1_g, ln1_b, router, exp_w_gate, exp_w_up, exp_w_down, ln2_g, ln2_b):
    B, S, D = x.shape
    depth = w_in.shape[0]
    alpha = (2 * depth) ** 0.25
    cap = CAPACITY_FACTOR * S // N_EXPERTS
    T = B * S
    rel_bias_t = rel_bias.T
    bias_tiles = _bias_tiles(rel_bias_t)
    act = BF16
    for l in range(depth):
        lambda_init = 0.8 - 0.6 * math.exp(-0.3 * l)
        p2 = _in_proj(x.reshape(T, D).astype(BF16), w_in[l].astype(BF16), rwkv_mu[l], S, IN_PROJ_TN, act)
        p = p2.reshape(B, S, -1)
        y_rwkv = _rwkv_branch(p, rwkv_w0[l], rwkv_w_up[l], rwkv_a0[l], rwkv_a_up[l],
                              rwkv_g_up[l], rwkv_k_k[l], rwkv_k_a[l], rwkv_r_k[l], rwkv_gn_g[l],
                              rwkv_gn_b[l], act)
        y_win = _window_branch(p, bias_tiles, win_sink[l], act)
        y_diff = _diff_branch(p, bias_tiles, rel_bias_t, diff_lambda[l], diff_subln_g[l], lambda_init, act)
        y_mem = _memory_branch(p, mem, mem_w_kv[l].astype(BF16), act)
        x1 = _merge(x.reshape(T, D), y_rwkv.reshape(T, -1), y_win.reshape(T, -1), y_diff.reshape(T, -1),
                    y_mem.reshape(T, -1), p2, w_branch[l].astype(BF16), w_out[l].astype(BF16),
                    ln1_g[l], ln1_b[l], alpha, 256).reshape(B, S, D)
        router_pad = jnp.concatenate([router[l], jnp.zeros((D, LANES - N_EXPERTS), F32)], axis=1)
        xe, slot, gate = _route(x1, router_pad, cap)
        ye = _expert_ffn(xe, gate, exp_w_gate, exp_w_up, exp_w_down, l, 512)
        x = _scatter_ln(x1, slot, ye, ln2_g[l], ln2_b[l], alpha, 512)
    return x
```

```python
import functools
import math

import jax
import jax.numpy as jnp
from jax import lax
from jax.experimental import pallas as pl
from jax.experimental.pallas import tpu as pltpu

F32 = jnp.float32
BF16 = jnp.bfloat16

HEAD_DIM = 64
LANES = 128
RWKV_HEADS = 16
RWKV_DIM = RWKV_HEADS * HEAD_DIM
DECAY_LORA = 64
ICLR_LORA = 64
GATE_LORA = 128
RWKV_GN_EPS = 64e-5
RWKV_COLS = 3 * RWKV_DIM + DECAY_LORA + ICLR_LORA + GATE_LORA
RWKV_CHUNK = 64
RWKV_UNROLL = 8
assert RWKV_CHUNK == HEAD_DIM
WIN_Q_HEADS = 16
WIN_KV_HEADS = 4
WIN_GROUP = WIN_Q_HEADS // WIN_KV_HEADS
WIN_Q_COLS = WIN_Q_HEADS * HEAD_DIM
WIN_KV_COLS = WIN_KV_HEADS * HEAD_DIM
WIN_COLS = WIN_Q_COLS + 2 * WIN_KV_COLS
WINDOW = 128
BLOCK = 128
WIN_UNROLL = 4
DIFF_HEADS = 8
DIFF_V_DIM = 2 * HEAD_DIM
DIFF_QK_COLS = 2 * DIFF_HEADS * HEAD_DIM
DIFF_V_COLS = DIFF_HEADS * DIFF_V_DIM
DIFF_COLS = 2 * DIFF_QK_COLS + DIFF_V_COLS
DIFF_Q_TILE = 8 * BLOCK
DIFF_CHUNK_BLOCKS = 16
MEM_HEADS = 4
MEM_WIDTH = MEM_HEADS * HEAD_DIM
NUM_BUCKETS = 32
N_EXPERTS = 16
CAPACITY_FACTOR = 2
N_BRANCHES = 4
NEG_INF = -1e30
LN_EPS = 1e-5
VMEM_LIMIT = 56 * 1024 * 1024
D_MODEL = 1024
IN_PROJ_TN_SHIFT = 256
IN_PROJ_TN = 1280

OFF_GATE = 0
OFF_WIN = N_BRANCHES * D_MODEL
OFF_DIFF = OFF_WIN + WIN_COLS
OFF_MEM = OFF_DIFF + DIFF_COLS
REST_COLS = OFF_MEM + MEM_WIDTH


def _cparams(*sem):
    return pltpu.CompilerParams(dimension_semantics=sem, vmem_limit_bytes=VMEM_LIMIT)


def _bdot(a, b):
    return jnp.dot(a.astype(BF16), b.astype(BF16), preferred_element_type=F32)


def _bdot_nt(a, b):
    return lax.dot_general(a.astype(BF16), b.astype(BF16), (((1,), (1,)), ((), ())),
                           preferred_element_type=F32)


def _split_dot(a, b_exact):
    hi = a.astype(BF16)
    lo = (a - hi.astype(F32)).astype(BF16)
    return (jnp.dot(hi, b_exact, preferred_element_type=F32)
            + jnp.dot(lo, b_exact, preferred_element_type=F32))


def _sigmoid(x):
    return 0.5 * jnp.tanh(0.5 * x) + 0.5


def _layer_norm(x, g, b):
    mu = jnp.mean(x, axis=-1, keepdims=True)
    xc = x - mu
    var = jnp.mean(xc * xc, axis=-1, keepdims=True)
    return xc * lax.rsqrt(var + LN_EPS) * g + b


def _inproj_shift_kernel(x_ref, w_ref, mu_ref, o_ref):
    acc = jnp.dot(x_ref[...], w_ref[...], preferred_element_type=F32)
    S = acc.shape[0]
    row = lax.broadcasted_iota(jnp.int32, (S, 1), 0)
    prev = jnp.where(row == 0, 0.0, pltpu.roll(acc, 1, 0))
    nxt = jnp.where(row == S - 1, 0.0, pltpu.roll(acc, S - 1, 0))
    out = acc + mu_ref[0:1, :] * (prev - acc) + mu_ref[1:2, :] * (nxt - acc)
    o_ref[...] = out.astype(o_ref.dtype)


def _inproj_plain_kernel(x_ref, w_ref, o_ref):
    o_ref[...] = jnp.dot(x_ref[...], w_ref[...], preferred_element_type=F32).astype(o_ref.dtype)


def _in_proj(x2, w_rwkv, w_rest, mu, seq, out_dtype):
    m, k = x2.shape
    x_spec = pl.BlockSpec((seq, k), lambda j, i: (i, 0))

    def call(kern, w, tn, extra, extra_specs, name):
        n = w.shape[1]
        assert n % tn == 0
        return pl.pallas_call(
            kern,
            grid=(n // tn, m // seq),
            in_specs=[x_spec, pl.BlockSpec((k, tn), lambda j, i: (0, j))] + extra_specs,
            out_specs=pl.BlockSpec((seq, tn), lambda j, i: (i, j)),
            out_shape=jax.ShapeDtypeStruct((m, n), out_dtype),
            compiler_params=_cparams("arbitrary", "arbitrary"),
            name=name,
        )(x2, w, *extra)

    p_rwkv = call(_inproj_shift_kernel, w_rwkv, IN_PROJ_TN_SHIFT, [mu],
                  [pl.BlockSpec((2, IN_PROJ_TN_SHIFT), lambda j, i: (0, j))], "in_proj_shift")
    p_rest = call(_inproj_plain_kernel, w_rest, IN_PROJ_TN, [], [], "in_proj")
    return p_rwkv, p_rest


def _bias_kernel(tab_ref, o_ref):
    h = pl.program_id(0)
    qi = lax.broadcasted_iota(jnp.int32, (BLOCK, 3 * BLOCK), 0)
    kj = lax.broadcasted_iota(jnp.int32, (BLOCK, 3 * BLOCK), 1)
    rel = kj - BLOCK - qi
    n = jnp.abs(rel)
    n2 = n * n
    large = jnp.full_like(n, NUM_BUCKETS // 4)
    for kpow in range(1, 8):
        large = large + jnp.where(n2 >= 64 * (2 ** kpow), 1, 0)
    bucket = jnp.where(rel > 0, NUM_BUCKETS // 2, 0) + jnp.where(n < NUM_BUCKETS // 4, n, large)
    acc = jnp.zeros((BLOCK, 3 * BLOCK), F32)
    for bkt in range(NUM_BUCKETS):
        acc = jnp.where(bucket == bkt, tab_ref[h, bkt], acc)
    o_ref[0] = acc


def _bias_tiles(rel_bias_t):
    nh = rel_bias_t.shape[0]
    return pl.pallas_call(
        _bias_kernel,
        grid=(nh,),
        in_specs=[pl.BlockSpec(memory_space=pltpu.SMEM)],
        out_specs=pl.BlockSpec((1, BLOCK, 3 * BLOCK), lambda h: (h, 0, 0)),
        out_shape=jax.ShapeDtypeStruct((nh, BLOCK, 3 * BLOCK), F32),
        compiler_params=_cparams("arbitrary"),
        name="t5_bias_tiles",
    )(rel_bias_t)


def _rwkv_kernel(pr_ref, pk_ref, pv_ref, pl_ref,
                 w0_ref, wup_ref, a0_ref, aup_ref, gup_ref, kk_ref, ka_ref, rk_ref,
                 gng_ref, gnb_ref, o_ref,
                 kt_s, rt_s, kh_s, bh_s, cum_s, v_s, y_s, m_s, n_s, *, seq):
    S = seq
    C = RWKV_CHUNK
    nchunk = S // C
    row = lax.broadcasted_iota(jnp.int32, (S, 1), 0)
    lane = lax.broadcasted_iota(jnp.int32, (1, LANES), 1)
    head0 = lane < HEAD_DIM
    ri = lax.broadcasted_iota(jnp.int32, (LANES, LANES), 0)
    ci = lax.broadcasted_iota(jnp.int32, (LANES, LANES), 1)
    same_head = (ri < HEAD_DIM) == (ci < HEAD_DIM)
    eye = ri == ci
    seg_ones = jnp.where(same_head, 1.0, 0.0).astype(BF16)

    GRP = min(S, 4 * C)
    gi = lax.broadcasted_iota(jnp.int32, (GRP, GRP), 0)
    gj = lax.broadcasted_iota(jnp.int32, (GRP, GRP), 1)
    same_chunk = (gi // C) == (gj // C)
    scan_tri = (jnp.where(same_chunk & (gj <= gi), 1.0, 0.0).astype(BF16),
                jnp.where(same_chunk & (gj >= gi), 1.0, 0.0).astype(BF16))

    r = pr_ref[0].astype(F32)
    k = pk_ref[0].astype(F32)
    v = pv_ref[0].astype(F32)
    lo = pl_ref[0].astype(F32)
    l1 = lo[:, :LANES]
    l2 = lo[:, LANES:]
    g = _bdot(_sigmoid(l2), gup_ref[...])
    kkr = k * kk_ref[...]
    kk = kkr * lax.rsqrt(jnp.maximum(_split_dot(kkr * kkr, seg_ones), 1e-12))
    v_s[...] = v
    tanh_l1 = jnp.tanh(l1)
    kd_sum = jnp.zeros((S, LANES), F32)

    for z in range(2):
        wl = w0_ref[z:z + 1, :] + _bdot(tanh_l1, wup_ref[z])
        logw = -math.exp(-0.5) * _sigmoid(wl)
        a = _sigmoid(a0_ref[z:z + 1, :] + _bdot(l1, aup_ref[z]))
        kd = k * (1.0 + (a - 1.0) * ka_ref[...])
        bb = a * kk
        kd_sum = kd_sum + kd
        lw_hi = logw.astype(BF16)
        lw_lo = (logw - lw_hi.astype(F32)).astype(BF16)
        tri = scan_tri[z]
        cum = jnp.concatenate(
            [jnp.dot(tri, lw_hi[g0:g0 + GRP], preferred_element_type=F32)
             + jnp.dot(tri, lw_lo[g0:g0 + GRP], preferred_element_type=F32)
             for g0 in range(0, S, GRP)], axis=0)
        e_out = jnp.exp(-cum)
        kt_s[z] = kk * jnp.exp(cum - logw)
        rt_s[z] = r * jnp.exp(cum)
        kh_s[z] = kd * e_out
        bh_s[z] = bb * e_out
        cum_s[z] = cum

    Q = 4 * HEAD_DIM
    qi = lax.broadcasted_iota(jnp.int32, (Q, Q), 0)
    qj = lax.broadcasted_iota(jnp.int32, (Q, Q), 1)
    same_blk = (qi // HEAD_DIM) == (qj // HEAD_DIM)
    ct = lax.broadcasted_iota(jnp.int32, (C, Q), 0)
    cj = lax.broadcasted_iota(jnp.int32, (C, Q), 1)
    dt = ((cj % C) - ct) * jnp.where(cj < 2 * C, 1, -1)
    strict_m = dt < 0
    incl_m = dt <= 0

    def both(ref, rws2):
        return jnp.concatenate([ref[0, rws2[0], :], ref[1, rws2[1], :]], axis=1)

    def bd(a):
        return jnp.where(same_blk, jnp.concatenate([a.astype(BF16)] * 4, axis=0), 0.0)

    def dot(a, b):
        return jnp.dot(a.astype(BF16), b, preferred_element_type=F32)

    def dot_nt(a, b):
        return lax.dot_general(a.astype(BF16), b, (((1,), (1,)), ((), ())), preferred_element_type=F32)

    U = min(RWKV_UNROLL, nchunk)

    def chunk_of(g, u, z):
        c = g * U + u
        return c if z == 0 else nchunk - 1 - c

    def rows_of(c):
        return pl.ds(c * C, C) if isinstance(c, int) else pl.ds(pl.multiple_of(c * C, C), C)

    def phase1(g, between=lambda k: None):
        ur = range(U)
        cs = [(chunk_of(g, u, 0), chunk_of(g, u, 1)) for u in ur]
        rws = [(rows_of(c0_), rows_of(c1_)) for c0_, c1_ in cs]
        kt = [both(kt_s, r_) for r_ in rws]
        rt = [both(rt_s, r_) for r_ in rws]
        vbd = [bd(jnp.concatenate([v_s[r_[0], :], v_s[r_[1], :]], axis=1)) for r_ in rws]
        lhs = [jnp.concatenate([kt[u], rt[u]], axis=0) for u in ur]
        gb = [dot_nt(lhs[u], bd(both(bh_s, rws[u]))) for u in ur]
        gk = [dot_nt(lhs[u], bd(both(kh_s, rws[u]))) for u in ur]
        between(0)
        a_rb = [jnp.where(incl_m, gb[u][C:], 0.0) for u in ur]
        a_kr = [jnp.concatenate([jnp.where(strict_m, gk[u][:C], 0.0),
                                 jnp.where(incl_m, gk[u][C:], 0.0)], axis=0) for u in ur]
        yy = [jnp.where(strict_m, -gb[u][:C], 0.0) for u in ur]
        x = list(yy)
        p = [dot(yy[u], bd(yy[u])) for u in ur]
        yy = p
        between(1)
        for step in range(1, 6):
            last = step == 5
            p = [dot(x[u] if last else jnp.concatenate([yy[u], x[u]], axis=0), bd(yy[u])) for u in ur]
            x = [x[u] + yy[u] + (p[u] if last else p[u][C:]) for u in ur]
            if not last:
                yy = [p[u][:C] for u in ur]
            between(1 + step)
        avr = [dot(a_kr[u], vbd[u]) for u in ur]
        between(7)
        tz = []
        for u in ur:
            av = avr[u][:C]
            zz = jnp.concatenate([kt[u], av], axis=1)
            tz.append(zz + dot(x[u], jnp.concatenate([bd(kt[u]), bd(av)], axis=1)))
        cor = [dot(a_rb[u], jnp.concatenate([bd(tz[u][:, :Q]), bd(tz[u][:, Q:])], axis=1)) for u in ur]
        for u in ur:
            rr = rt[u] - cor[u][:, :Q]
            yl = avr[u][C:] - cor[u][:, Q:]
            for z in range(2):
                zl = slice(z * LANES, (z + 1) * LANES)
                rz = rws[u][z]
                y_s[z, rz, :] = yl[:, zl]
                rt_s[z, rz, :] = rr[:, zl]
                far = C - 1 if z == 0 else 0
                pt = jnp.exp(cum_s[z, rz, :][far:far + 1, :])
                ends = jnp.concatenate([bh_s[z, rz, :] * pt, kh_s[z, rz, :] * pt], axis=0)
                rhs = jnp.concatenate(
                    [jnp.concatenate([-tz[u][:, zl], -tz[u][:, Q + z * LANES:Q + (z + 1) * LANES]], axis=1),
                     jnp.concatenate([jnp.zeros((C, LANES), F32), v_s[rz, :]], axis=1)], axis=0)
                mn = _bdot(ends.T, rhs)
                m_s[z, cs[u][z]] = jnp.where(eye, jnp.broadcast_to(pt, (LANES, LANES)), 0.0) \
                    + jnp.where(same_head, mn[:, :LANES], 0.0)
                n_s[z, cs[u][z]] = jnp.where(same_head, mn[:, LANES:], 0.0)

    def carry_step(g, u, hs):
        for z in range(2):
            c = chunk_of(g, u, z)
            rows = rows_of(c)
            y_s[z, rows, :] = y_s[z, rows, :] + dot(rt_s[z, rows, :], hs[z])
            hs[z] = (dot(m_s[z, c], hs[z]) + n_s[z, c]).astype(BF16)

    def phase2(g, hs):
        hs = list(hs)
        for u in range(U):
            carry_step(g, u, hs)
        return tuple(hs)

    def lagged(g, hs):
        hs = list(hs)
        steps_per_gap = -(-U // 8)

        def between(k):
            for u in range(k * steps_per_gap, min(U, (k + 1) * steps_per_gap)):
                carry_step(g - 1, u, hs)

        phase1(g, between)
        return tuple(hs)

    ngroup = nchunk // U
    h0 = jnp.zeros((LANES, LANES), BF16)
    phase1(0)
    hs = lax.fori_loop(1, ngroup, lagged, (h0, h0))
    phase2(ngroup - 1, hs)

    y = y_s[0] + y_s[1]
    inv_n = 1.0 / HEAD_DIM
    mu_y = _split_dot(y, seg_ones) * inv_n
    yc = y - mu_y
    var_y = _split_dot(yc * yc, seg_ones) * inv_n
    yn = yc * lax.rsqrt(var_y + RWKV_GN_EPS) * gng_ref[...] + gnb_ref[...]
    bonus = _split_dot(r * kd_sum * rk_ref[...], seg_ones) * v
    o_ref[0] = ((yn + bonus) * g).astype(o_ref.dtype)


def _rwkv_branch(p, w0, w_up, a0, a_up, g_up, k_k, k_a, r_k, gn_g, gn_b, out_dtype):
    B, S, _ = p.shape
    npair = RWKV_DIM // LANES
    zpad = jnp.zeros((2, LANES - DECAY_LORA, RWKV_DIM), F32)
    wup_p = jnp.concatenate([w_up, zpad], axis=1).astype(BF16)
    aup_p = jnp.concatenate([zpad, a_up], axis=1).astype(BF16)
    row = lambda a: a.reshape(1, RWKV_DIM)
    lora_blk = 3 * RWKV_DIM // (2 * LANES)

    def pspec(off):
        return pl.BlockSpec((1, S, LANES), lambda b, j: (b, 0, off + j))

    def vspec(rows, off=0):
        return pl.BlockSpec((rows, LANES), lambda b, j: (0, off + j))

    scratch = [pltpu.VMEM((2, S, LANES), F32) for _ in range(5)] + [
        pltpu.VMEM((S, LANES), F32), pltpu.VMEM((2, S, LANES), F32),
        pltpu.VMEM((2, S // RWKV_CHUNK, LANES, LANES), F32),
        pltpu.VMEM((2, S // RWKV_CHUNK, LANES, LANES), F32)]
    return pl.pallas_call(
        functools.partial(_rwkv_kernel, seq=S),
        grid=(B, npair),
        in_specs=[pspec(0), pspec(npair), pspec(2 * npair),
                  pl.BlockSpec((1, S, 2 * LANES), lambda b, j: (b, 0, lora_blk)),
                  vspec(2),
                  pl.BlockSpec((2, LANES, LANES), lambda b, j: (0, 0, j)),
                  vspec(2),
                  pl.BlockSpec((2, LANES, LANES), lambda b, j: (0, 0, j)),
                  vspec(GATE_LORA), vspec(1), vspec(1), vspec(1), vspec(1), vspec(1)],
        out_specs=pl.BlockSpec((1, S, LANES), lambda b, j: (b, 0, j)),
        out_shape=jax.ShapeDtypeStruct((B, S, RWKV_DIM), out_dtype),
        scratch_shapes=scratch,
        compiler_params=_cparams("arbitrary", "arbitrary"),
        name="rwkv7_scan",
    )(p, p, p, p, w0, wup_p, a0, aup_p, g_up.astype(BF16),
      row(k_k), row(k_a), row(r_k), row(gn_g), row(gn_b))


def _win_kernel(q_ref, k_ref, v_ref, biast_ref, sink_ref, o_ref, k4_s, vt_s, *, seq):
    S = seq
    G = WIN_GROUP
    NB = S // BLOCK
    GW = G * HEAD_DIM
    hk = pl.program_id(1)
    odd = (hk % 2) == 1
    lane = lax.broadcasted_iota(jnp.int32, (1, LANES), 1)

    def own_head_twice(full):
        mine_first = jnp.where(odd, pltpu.roll(full, HEAD_DIM, 1), full)
        return jnp.where(lane < HEAD_DIM, mine_first, pltpu.roll(mine_first, HEAD_DIM, 1))

    k2 = own_head_twice(k_ref[0].astype(F32)).astype(BF16)
    v2 = own_head_twice(v_ref[0].astype(F32))
    zero_k = jnp.zeros((BLOCK, GW), BF16)
    zero_v = jnp.zeros((HEAD_DIM, BLOCK), BF16)
    k4_s[0] = zero_k
    k4_s[NB + 1] = zero_k
    vt_s[0] = zero_v
    vt_s[NB + 1] = zero_v
    for blk in range(NB):
        rows = slice(blk * BLOCK, (blk + 1) * BLOCK)
        k4_s[blk + 1] = jnp.concatenate([k2[rows], k2[rows]], axis=1)
        vt_s[blk + 1] = v2[rows].T[:HEAD_DIM].astype(BF16)
    bias_t = jnp.concatenate([biast_ref[gi] for gi in range(G)], axis=1)
    krow = lax.broadcasted_iota(jnp.int32, (3 * BLOCK, G * BLOCK), 0)
    qcol = lax.broadcasted_iota(jnp.int32, (3 * BLOCK, G * BLOCK), 1) % BLOCK
    bias_t = jnp.where(jnp.abs(krow - BLOCK - qcol) <= WINDOW, bias_t, NEG_INF)
    cgrp = lax.broadcasted_iota(jnp.int32, (1, G * BLOCK), 1) // BLOCK
    sink_row = jnp.zeros((1, G * BLOCK), F32)
    for gi in range(G):
        sink_row = jnp.where(cgrp == gi, sink_ref[hk * G + gi], sink_row)
    lane_q = lax.broadcasted_iota(jnp.int32, (1, GW), 1) // HEAD_DIM
    U = min(WIN_UNROLL, NB)

    def body(it, carry):
        ns = [it * U + u for u in range(U)]
        sts = []
        for n in ns:
            qb = q_ref[0, pl.ds(pl.multiple_of(n * BLOCK, BLOCK), BLOCK), :].astype(F32) * (HEAD_DIM ** -0.5)
            qm = jnp.concatenate([jnp.where(lane_q == gi, qb, 0.0) for gi in range(G)], axis=0).astype(BF16)
            kband = jnp.concatenate([k4_s[n], k4_s[n + 1], k4_s[n + 2]], axis=0)
            sts.append(lax.dot_general(kband, qm, (((1,), (1,)), ((), ())), preferred_element_type=F32))
        es, dens = [], []
        for u, n in enumerate(ns):
            st = sts[u] + bias_t
            st = jnp.concatenate([st[:BLOCK] + jnp.where(n == 0, NEG_INF, 0.0), st[BLOCK:2 * BLOCK],
                                  st[2 * BLOCK:] + jnp.where(n == NB - 1, NEG_INF, 0.0)], axis=0)
            m = jnp.maximum(jnp.max(st, axis=0, keepdims=True), sink_row)
            e = jnp.exp(st - m)
            es.append(e.astype(BF16))
            dens.append(jnp.sum(e, axis=0, keepdims=True) + jnp.exp(sink_row - m))
        for u, n in enumerate(ns):
            vband = jnp.concatenate([vt_s[n], vt_s[n + 1], vt_s[n + 2]], axis=1)
            acc = jnp.dot(vband, es[u], preferred_element_type=F32) / dens[u]
            o_t = jnp.concatenate([acc[:, gi * BLOCK:(gi + 1) * BLOCK] for gi in range(G)], axis=0)
            o_ref[0, pl.ds(pl.multiple_of(n * BLOCK, BLOCK), BLOCK), :] = o_t.T.astype(o_ref.dtype)
        return carry

    lax.fori_loop(0, NB // U, body, 0)


def _window_branch(p, bias_tiles_t, sink, out_dtype):
    B, S, _ = p.shape
    gw = WIN_GROUP * HEAD_DIM
    nb = S // BLOCK
    assert nb % min(WIN_UNROLL, nb) == 0
    qoff = OFF_WIN // gw
    koff = (OFF_WIN + WIN_Q_COLS) // LANES
    voff = (OFF_WIN + WIN_Q_COLS + WIN_KV_COLS) // LANES
    return pl.pallas_call(
        functools.partial(_win_kernel, seq=S),
        grid=(B, WIN_KV_HEADS),
        in_specs=[pl.BlockSpec((1, S, gw), lambda b, h: (b, 0, qoff + h)),
                  pl.BlockSpec((1, S, LANES), lambda b, h: (b, 0, koff + h // 2)),
                  pl.BlockSpec((1, S, LANES), lambda b, h: (b, 0, voff + h // 2)),
                  pl.BlockSpec((WIN_GROUP, 3 * BLOCK, BLOCK), lambda b, h: (h, 0, 0)),
                  pl.BlockSpec(memory_space=pltpu.SMEM)],
        out_specs=pl.BlockSpec((1, S, gw), lambda b, h: (b, 0, h)),
        out_shape=jax.ShapeDtypeStruct((B, S, WIN_Q_COLS), out_dtype),
        scratch_shapes=[pltpu.VMEM((nb + 2, BLOCK, gw), BF16),
                        pltpu.VMEM((nb + 2, HEAD_DIM, BLOCK), BF16)],
        compiler_params=_cparams("arbitrary", "arbitrary"),
        name="window_attn",
    )(p, p, p, bias_tiles_t, sink)


def _diff_kernel(q_ref, k_ref, v_ref, bandt_ref, tab_ref, lam_ref, g_ref, o_ref, ka_s, vt_s, *,
                 seq, lambda_init):
    S = seq
    NB = S // BLOCK
    CB = min(DIFF_CHUNK_BLOCKS, NB)
    h = pl.program_id(1)
    it = pl.program_id(2)
    halves = q_ref.shape[1] // BLOCK
    log2e = math.log2(math.e)
    lane = lax.broadcasted_iota(jnp.int32, (1, LANES), 1)
    c0 = lane < HEAD_DIM

    @pl.when(it == 0)
    def _():
        jblk = lax.broadcasted_iota(jnp.int32, (S, LANES), 0) // BLOCK
        lj = lax.broadcasted_iota(jnp.int32, (S, LANES), 1)
        ka_s[:, :LANES] = k_ref[0].astype(BF16)
        ka_s[:, LANES:] = jnp.where((lj < 2 * NB) & (lj % NB == jblk), 1.0, 0.0).astype(BF16)
        for blk in range(NB):
            vt_s[blk] = v_ref[0, blk * BLOCK:(blk + 1) * BLOCK, :].astype(F32).T.astype(BF16)

    far_l = tab_ref[WIN_Q_HEADS + h, NUM_BUCKETS // 2 - 1] * log2e
    far_r = tab_ref[WIN_Q_HEADS + h, NUM_BUCKETS - 1] * log2e
    band_t = bandt_ref[0] * log2e
    lam = lam_ref[...].astype(F32)
    lam_full = (jnp.exp(jnp.sum(lam[0:1] * lam[1:2], axis=-1, keepdims=True))
                - jnp.exp(jnp.sum(lam[2:3] * lam[3:4], axis=-1, keepdims=True)) + lambda_init)
    ibs = [it * halves + hf for hf in range(halves)]
    zblk = jnp.zeros((BLOCK, BLOCK), F32)
    qq, bands = [], []
    for hf in range(halves):
        q = q_ref[0, hf * BLOCK:(hf + 1) * BLOCK, :].astype(F32) * (log2e * HEAD_DIM ** -0.5)
        kb = lane % NB
        val = jnp.where(kb < ibs[hf] - 1, far_l, jnp.where(kb > ibs[hf] + 1, far_r, 0.0))
        val = jnp.where(lane < 2 * NB, val, 0.0)
        hi = val.astype(BF16)
        lo = (val - hi.astype(F32)).astype(BF16)
        bcols = jnp.broadcast_to(jnp.where(lane < NB, hi, lo), (BLOCK, LANES))
        qq.append(jnp.concatenate(
            [jnp.concatenate([jnp.where(c0, q, 0.0).astype(BF16), bcols], axis=1),
             jnp.concatenate([jnp.where(c0, 0.0, q).astype(BF16), bcols], axis=1)], axis=0))
        tiles = [jnp.where(ibs[hf] > 0, band_t[:BLOCK], 0.0), band_t[BLOCK:2 * BLOCK],
                 jnp.where(ibs[hf] < NB - 1, band_t[2 * BLOCK:], 0.0)]
        col = jnp.concatenate([zblk] * hf + tiles + [zblk] * (CB - 3 - hf), axis=0)
        bands.append(jnp.concatenate([col, col], axis=1))

    m_run, l_run, acc = [None] * halves, [None] * halves, [None] * halves
    for r in range(NB // CB):
        kbs = [lax.rem(it * halves - 1 + CB * r + u + NB, NB) for u in range(CB)]
        kc = jnp.concatenate([ka_s[pl.ds(pl.multiple_of(kb * BLOCK, BLOCK), BLOCK), :] for kb in kbs], axis=0)
        vct = jnp.concatenate([vt_s[kb] for kb in kbs], axis=1)
        s = [lax.dot_general(kc, qq[hf], (((1,), (1,)), ((), ())), preferred_element_type=F32)
             for hf in range(halves)]
        if r == 0:
            s = [s[hf] + bands[hf] for hf in range(halves)]
        m_loc = [jnp.max(s[hf], axis=0, keepdims=True) for hf in range(halves)]
        for hf in range(halves):
            if r == 0:
                m_new = m_loc[hf]
                e = jnp.exp2(s[hf] - m_new)
                l_run[hf] = jnp.sum(e, axis=0, keepdims=True)
                acc[hf] = jnp.dot(vct, e.astype(BF16), preferred_element_type=F32)
            else:
                m_new = jnp.maximum(m_run[hf], m_loc[hf])
                alpha = jnp.exp2(m_run[hf] - m_new)
                e = jnp.exp2(s[hf] - m_new)
                l_run[hf] = alpha * l_run[hf] + jnp.sum(e, axis=0, keepdims=True)
                acc[hf] = alpha * acc[hf] + jnp.dot(vct, e.astype(BF16), preferred_element_type=F32)
            m_run[hf] = m_new
    for hf in range(halves):
        rinv = 1.0 / l_run[hf]
        o_t = acc[hf][:, :BLOCK] * rinv[:, :BLOCK] - acc[hf][:, BLOCK:] * (lam_full * rinv[:, BLOCK:])
        o = o_t.T
        o = o * lax.rsqrt(jnp.mean(o * o, axis=-1, keepdims=True) + 1e-5) * g_ref[...] * (1.0 - lambda_init)
        o_ref[0, hf * BLOCK:(hf + 1) * BLOCK, :] = o.astype(o_ref.dtype)


def _diff_branch(p, band_tiles, rel_bias_t, lam, subln_g, lambda_init, out_dtype):
    B, S, _ = p.shape
    qoff = OFF_DIFF // LANES
    koff = (OFF_DIFF + DIFF_QK_COLS) // LANES
    voff = (OFF_DIFF + 2 * DIFF_QK_COLS) // LANES
    nb = S // BLOCK
    cb = min(DIFF_CHUNK_BLOCKS, nb)
    tq = DIFF_Q_TILE
    while tq > (cb - 2) * BLOCK or S % tq:
        tq //= 2
    assert 2 * nb <= LANES and tq >= BLOCK and nb % cb == 0 and nb >= 4
    return pl.pallas_call(
        functools.partial(_diff_kernel, seq=S, lambda_init=lambda_init),
        grid=(B, DIFF_HEADS, S // tq),
        in_specs=[pl.BlockSpec((1, tq, LANES), lambda b, h, i: (b, i, qoff + h)),
                  pl.BlockSpec((1, S, LANES), lambda b, h, i: (b, 0, koff + h)),
                  pl.BlockSpec((1, S, LANES), lambda b, h, i: (b, 0, voff + h)),
                  pl.BlockSpec((1, 3 * BLOCK, BLOCK), lambda b, h, i: (WIN_Q_HEADS + h, 0, 0)),
                  pl.BlockSpec(memory_space=pltpu.SMEM),
                  pl.BlockSpec((4, HEAD_DIM), lambda b, h, i: (0, 0)),
                  pl.BlockSpec((1, DIFF_V_DIM), lambda b, h, i: (0, 0))],
        out_specs=pl.BlockSpec((1, tq, LANES), lambda b, h, i: (b, i, h)),
        out_shape=jax.ShapeDtypeStruct((B, S, DIFF_V_COLS), out_dtype),
        scratch_shapes=[pltpu.VMEM((S, 2 * LANES), BF16),
                        pltpu.VMEM((S // BLOCK, DIFF_V_DIM, BLOCK), BF16)],
        compiler_params=_cparams("arbitrary", "arbitrary", "arbitrary"),
        name="diff_attn",
    )(p, p, p, jnp.swapaxes(band_tiles, 1, 2), rel_bias_t, lam, subln_g.reshape(1, DIFF_V_DIM))


def _mem_kernel(q_ref, mem_ref, wkv_ref, o_ref):
    kv = _bdot(mem_ref[0], wkv_ref[...])
    kmat = kv[:, :MEM_WIDTH]
    vmat = kv[:, MEM_WIDTH:]
    q = q_ref[0].astype(F32) * (HEAD_DIM ** -0.5)
    lane = lax.broadcasted_iota(jnp.int32, (1, MEM_WIDTH), 1) // HEAD_DIM
    out = jnp.zeros(q.shape, F32)
    for hh in range(MEM_HEADS):
        s = _bdot_nt(jnp.where(lane == hh, q, 0.0), kmat)
        m = jnp.max(s, axis=-1, keepdims=True)
        e = jnp.exp(s - m)
        a = e / jnp.sum(e, axis=-1, keepdims=True)
        out = jnp.where(lane == hh, _bdot(a, vmat), out)
    o_ref[0] = out.astype(o_ref.dtype)


def _memory_branch(p, mem, w_kv, out_dtype):
    B, S, _ = p.shape
    nm, d = mem.shape[1], mem.shape[2]
    return pl.pallas_call(
        _mem_kernel,
        grid=(B,),
        in_specs=[pl.BlockSpec((1, S, MEM_WIDTH), lambda b: (b, 0, OFF_MEM // MEM_WIDTH)),
                  pl.BlockSpec((1, nm, d), lambda b: (b, 0, 0)),
                  pl.BlockSpec((d, 2 * MEM_WIDTH), lambda b: (0, 0))],
        out_specs=pl.BlockSpec((1, S, MEM_WIDTH), lambda b: (b, 0, 0)),
        out_shape=jax.ShapeDtypeStruct((B, S, MEM_WIDTH), out_dtype),
        compiler_params=_cparams("arbitrary"),
        name="memory_attn",
    )(p, mem, w_kv)


def _merge_kernel(x_ref, yr_ref, yw_ref, yd_ref, ym_ref, pg_ref, wb_ref, wo_ref, g_ref, b_ref,
                  o_ref, *, d_model, alpha):
    D = d_model
    ys = (yr_ref[...], yw_ref[...], yd_ref[...], ym_ref[...])
    off = 0
    merged = jnp.zeros((x_ref.shape[0], D), F32)
    for bi, yb in enumerate(ys):
        rows = yb.shape[1]
        proj = _bdot(yb, wb_ref[off:off + rows, :])
        merged = merged + _sigmoid(pg_ref[:, bi * D:(bi + 1) * D].astype(F32)) * proj
        off += rows
    z = alpha * x_ref[...] + _bdot(merged, wo_ref[...])
    o_ref[...] = _layer_norm(z, g_ref[...], b_ref[...])


def _merge(x2, y_rwkv, y_win, y_diff, y_mem, p2, w_branch, w_out, ln_g, ln_b, alpha, tm):
    T, D = x2.shape
    gate_blk = OFF_GATE // (N_BRANCHES * D)
    full = lambda a: pl.BlockSpec(a.shape, lambda i: (0, 0))
    tile = lambda w: pl.BlockSpec((tm, w), lambda i: (i, 0))
    return pl.pallas_call(
        functools.partial(_merge_kernel, d_model=D, alpha=alpha),
        grid=(T // tm,),
        in_specs=[tile(D), tile(y_rwkv.shape[1]), tile(y_win.shape[1]), tile(y_diff.shape[1]),
                  tile(y_mem.shape[1]),
                  pl.BlockSpec((tm, N_BRANCHES * D), lambda i: (i, gate_blk)),
                  full(w_branch), full(w_out),
                  pl.BlockSpec((1, D), lambda i: (0, 0)), pl.BlockSpec((1, D), lambda i: (0, 0))],
        out_specs=tile(D),
        out_shape=jax.ShapeDtypeStruct((T, D), F32),
        compiler_params=_cparams("arbitrary"),
        name="merge_out_ln",
    )(x2, y_rwkv, y_win, y_diff, y_mem, p2, w_branch, w_out, ln_g.reshape(1, D), ln_b.reshape(1, D))


def _route_kernel(x_ref, wr_ref, xe_ref, slot_ref, gate_ref, xb_s, lg_s, *, seq, cap):
    S = seq
    E = N_EXPERTS
    TR = min(S, 512)
    wr = wr_ref[...]
    w_hi = wr.astype(BF16)
    w_md = (wr - w_hi.astype(F32)).astype(BF16)
    dot = lambda a, b: jnp.dot(a, b, preferred_element_type=F32)

    def logit_tile(t, carry):
        rows = pl.ds(pl.multiple_of(t * TR, TR), TR)
        x = x_ref[0, rows, :]
        x_hi = x.astype(BF16)
        x_r1 = x - x_hi.astype(F32)
        x_md = x_r1.astype(BF16)
        x_lo = (x_r1 - x_md.astype(F32)).astype(BF16)
        xb_s[rows, :] = x_hi
        lg_s[rows, :] = (dot(x_hi, w_hi) + (dot(x_hi, w_md) + dot(x_md, w_hi))
                         + (dot(x_md, w_md) + dot(x_lo, w_hi)))
        return carry

    lax.fori_loop(0, S // TR, logit_tile, 0)
    col = lax.broadcasted_iota(jnp.int32, (1, LANES), 1)
    logits = jnp.where(col < E, lg_s[...], NEG_INF)
    mx = jnp.max(logits, axis=-1, keepdims=True)
    ex = jnp.exp(logits - mx)
    aff = ex / jnp.sum(ex, axis=-1, keepdims=True)
    aff_t = aff.T[0:E, :]
    bits = pltpu.bitcast(aff_t, jnp.int32)

    def bis(_, carry):
        lo_b, step = carry
        cand = lo_b + step
        cnt = jnp.sum(jnp.where(bits >= cand, 1.0, 0.0), axis=-1, keepdims=True)
        return jnp.where(cnt >= cap, cand, lo_b), step // 2

    lo0 = jnp.zeros((E, 1), jnp.int32)
    thr, _ = lax.fori_loop(0, 31, bis, (lo0, jnp.full((E, 1), 1 << 30, jnp.int32)))
    above = bits > thr
    tie = bits == thr
    n_above = jnp.sum(jnp.where(above, 1.0, 0.0), axis=-1, keepdims=True)
    PB = min(S, 256)
    tri = (lax.broadcasted_iota(jnp.int32, (PB, PB), 0)
           < lax.broadcasted_iota(jnp.int32, (PB, PB), 1)).astype(BF16)

    def prefix_count(mask_f):
        parts, run = [], jnp.zeros((E, 1), F32)
        for blk in range(S // PB):
            seg = mask_f[:, blk * PB:(blk + 1) * PB]
            parts.append(jnp.dot(seg.astype(BF16), tri, preferred_element_type=F32) + run)
            run = run + jnp.sum(seg, axis=-1, keepdims=True)
        return jnp.concatenate(parts, axis=1)

    tie_rank = prefix_count(jnp.where(tie, 1.0, 0.0))
    sel = above | (tie & (tie_rank < cap - n_above))
    pos = prefix_count(jnp.where(sel, 1.0, 0.0))
    slot = jnp.where(sel, pos, -1.0)
    slot_pad = jnp.concatenate([slot, jnp.full((LANES - E, S), -1.0, F32)], axis=0)
    slot_ref[0] = slot_pad.T
    ci = lax.broadcasted_iota(jnp.int32, (cap, S), 0).astype(F32)
    x_hi = xb_s[...]
    for e in range(E):
        onehot = slot[e:e + 1, :] == ci
        xe_ref[e, 0] = jnp.dot(jnp.where(onehot, 1.0, 0.0).astype(BF16), x_hi,
                               preferred_element_type=F32).astype(xe_ref.dtype)
        gsel = jnp.sum(jnp.where(onehot, aff_t[e:e + 1, :], 0.0), axis=-1, keepdims=True)
        gate_ref[e, 0] = jnp.broadcast_to(gsel, (cap, LANES))


def _route(x3, router_pad, cap):
    B, S, D = x3.shape
    E = N_EXPERTS
    return pl.pallas_call(
        functools.partial(_route_kernel, seq=S, cap=cap),
        grid=(B,),
        in_specs=[pl.BlockSpec((1, S, D), lambda b: (b, 0, 0)),
                  pl.BlockSpec((D, LANES), lambda b: (0, 0))],
        out_specs=[pl.BlockSpec((E, 1, cap, D), lambda b: (0, b, 0, 0)),
                   pl.BlockSpec((1, S, LANES), lambda b: (b, 0, 0)),
                   pl.BlockSpec((E, 1, cap, LANES), lambda b: (0, b, 0, 0))],
        out_shape=[jax.ShapeDtypeStruct((E, B, cap, D), BF16),
                   jax.ShapeDtypeStruct((B, S, LANES), F32),
                   jax.ShapeDtypeStruct((E, B, cap, LANES), F32)],
        scratch_shapes=[pltpu.VMEM((S, D), BF16), pltpu.VMEM((S, LANES), F32)],
        compiler_params=_cparams("arbitrary"),
        name="moe_route_gather",
    )(x3, router_pad)


def _ffn_kernel(xe_ref, g_ref, wg_ref, wu_ref, wd_ref, ye_ref, acc_s):
    f = pl.program_id(1)
    nb, cap, D = xe_ref.shape[1], xe_ref.shape[2], xe_ref.shape[3]
    xe = xe_ref[0].reshape(nb * cap, D)
    hg = jnp.dot(xe, wg_ref[0, 0].astype(BF16), preferred_element_type=F32)
    hu = jnp.dot(xe, wu_ref[0, 0].astype(BF16), preferred_element_type=F32)
    hid = (hg * _sigmoid(hg)) * hu
    part = _bdot(hid, wd_ref[0, 0])

    @pl.when(f == 0)
    def _():
        acc_s[...] = part

    @pl.when(f > 0)
    def _():
        acc_s[...] = acc_s[...] + part

    @pl.when(f == pl.num_programs(1) - 1)
    def _():
        gate = g_ref[0].reshape(nb * cap, LANES)[:, 0:1]
        ye_ref[0] = (acc_s[...] * gate).reshape(nb, cap, D).astype(ye_ref.dtype)


def _expert_ffn(xe, gate, w_gate, w_up, w_down, layer, tf):
    E, B, cap, D = xe.shape
    F = w_gate.shape[-1]
    return pl.pallas_call(
        _ffn_kernel,
        grid=(E, F // tf),
        in_specs=[pl.BlockSpec((1, B, cap, D), lambda e, f: (e, 0, 0, 0)),
                  pl.BlockSpec((1, B, cap, LANES), lambda e, f: (e, 0, 0, 0)),
                  pl.BlockSpec((1, 1, D, tf), lambda e, f: (layer, e, 0, f)),
                  pl.BlockSpec((1, 1, D, tf), lambda e, f: (layer, e, 0, f)),
                  pl.BlockSpec((1, 1, tf, D), lambda e, f: (layer, e, f, 0))],
        out_specs=pl.BlockSpec((1, B, cap, D), lambda e, f: (e, 0, 0, 0)),
        out_shape=jax.ShapeDtypeStruct((E, B, cap, D), BF16),
        scratch_shapes=[pltpu.VMEM((B * cap, D), F32)],
        compiler_params=_cparams("arbitrary", "arbitrary"),
        name="moe_expert_ffn",
    )(xe, gate, w_gate, w_up, w_down)


def _scatter_kernel(x_ref, slot_ref, ye_ref, g_ref, b_ref, o_ref, *, cap, alpha):
    ts = x_ref.shape[1]
    slot = slot_ref[0]
    ci = lax.broadcasted_iota(jnp.int32, (ts, cap), 1).astype(F32)
    acc = jnp.zeros((ts, x_ref.shape[2]), F32)
    for e in range(N_EXPERTS):
        onehot = slot[:, e:e + 1] == ci
        acc = acc + jnp.dot(jnp.where(onehot, 1.0, 0.0).astype(BF16), ye_ref[e, 0],
                            preferred_element_type=F32)
    o_ref[0] = _layer_norm(alpha * x_ref[0] + acc, g_ref[...], b_ref[...])


def _scatter_ln(x3, slot, ye, ln_g, ln_b, alpha, ts):
    B, S, D = x3.shape
    E, _, cap, _ = ye.shape
    return pl.pallas_call(
        functools.partial(_scatter_kernel, cap=cap, alpha=alpha),
        grid=(B, S // ts),
        in_specs=[pl.BlockSpec((1, ts, D), lambda b, i: (b, i, 0)),
                  pl.BlockSpec((1, ts, LANES), lambda b, i: (b, i, 0)),
                  pl.BlockSpec((E, 1, cap, D), lambda b, i: (0, b, 0, 0)),
                  pl.BlockSpec((1, D), lambda b, i: (0, 0)),
                  pl.BlockSpec((1, D), lambda b, i: (0, 0))],
        out_specs=pl.BlockSpec((1, ts, D), lambda b, i: (b, i, 0)),
        out_shape=jax.ShapeDtypeStruct((B, S, D), F32),
        compiler_params=_cparams("arbitrary", "arbitrary"),
        name="moe_scatter_ln",
    )(x3, slot, ye, ln_g.reshape(1, D), ln_b.reshape(1, D))


def kernel(x, mem, rel_bias, w_in, rwkv_mu, rwkv_w0, rwkv_w_up, rwkv_a0, rwkv_a_up, rwkv_g_up, rwkv_k_k, rwkv_k_a, rwkv_r_k, rwkv_gn_g, rwkv_gn_b, win_sink, diff_lambda, diff_subln_g, mem_w_kv, w_branch, w_out, ln1_g, ln1_b, router, exp_w_gate, exp_w_up, exp_w_down, ln2_g, ln2_b):
    B, S, D = x.shape
    assert D == D_MODEL and w_in.shape[2] == RWKV_COLS + REST_COLS
    depth = w_in.shape[0]
    alpha = (2 * depth) ** 0.25
    cap = CAPACITY_FACTOR * S // N_EXPERTS
    T = B * S
    rel_bias_t = rel_bias.T
    bias_tiles = _bias_tiles(rel_bias_t)
    bias_tiles_t = jnp.swapaxes(bias_tiles, 1, 2)
    act = BF16
    for l in range(depth):
        lambda_init = 0.8 - 0.6 * math.exp(-0.3 * l)
        gate0 = w_in.shape[2] - N_BRANCHES * D
        w_rest = jnp.concatenate([w_in[l][:, gate0:], w_in[l][:, RWKV_COLS:gate0]], axis=1).astype(BF16)
        p_rwkv, p2 = _in_proj(x.reshape(T, D).astype(BF16), w_in[l][:, :RWKV_COLS].astype(BF16), w_rest,
                              rwkv_mu[l], S, act)
        p = p2.reshape(B, S, -1)
        y_rwkv = _rwkv_branch(p_rwkv.reshape(B, S, -1), rwkv_w0[l], rwkv_w_up[l], rwkv_a0[l], rwkv_a_up[l],
                              rwkv_g_up[l], rwkv_k_k[l], rwkv_k_a[l], rwkv_r_k[l], rwkv_gn_g[l],
                              rwkv_gn_b[l], act)
        y_win = _window_branch(p, bias_tiles_t, win_sink[l], act)
        y_diff = _diff_branch(p, bias_tiles, rel_bias_t, diff_lambda[l], diff_subln_g[l], lambda_init, act)
        y_mem = _memory_branch(p, mem, mem_w_kv[l].astype(BF16), act)
        x1 = _merge(x.reshape(T, D), y_rwkv.reshape(T, -1), y_win.reshape(T, -1), y_diff.reshape(T, -1),
                    y_mem.reshape(T, -1), p2, w_branch[l].astype(BF16), w_out[l].astype(BF16),
                    ln1_g[l], ln1_b[l], alpha, 256).reshape(B, S, D)
        router_pad = jnp.concatenate([router[l], jnp.zeros((D, LANES - N_EXPERTS), F32)], axis=1)
        xe, slot, gate = _route(x1, router_pad, cap)
        ye = _expert_ffn(xe, gate, exp_w_gate, exp_w_up, exp_w_down, l, 512)
        x = _scatter_ln(x1, slot, ye, ln2_g[l], ln2_b[l], alpha, 512)
    return x
```

```python
import functools
import math

import jax
import jax.numpy as jnp
from jax import lax
from jax.experimental import pallas as pl
from jax.experimental.pallas import tpu as pltpu

F32 = jnp.float32
BF16 = jnp.bfloat16

HEAD_DIM = 64
LANES = 128
RWKV_HEADS = 16
RWKV_DIM = RWKV_HEADS * HEAD_DIM
DECAY_LORA = 64
ICLR_LORA = 64
GATE_LORA = 128
RWKV_GN_EPS = 64e-5
RWKV_COLS = 3 * RWKV_DIM + DECAY_LORA + ICLR_LORA + GATE_LORA
RWKV_CHUNK = 64
RWKV_UNROLL = 8
assert RWKV_CHUNK == HEAD_DIM
WIN_Q_HEADS = 16
WIN_KV_HEADS = 4
WIN_GROUP = WIN_Q_HEADS // WIN_KV_HEADS
WIN_Q_COLS = WIN_Q_HEADS * HEAD_DIM
WIN_KV_COLS = WIN_KV_HEADS * HEAD_DIM
WIN_COLS = WIN_Q_COLS + 2 * WIN_KV_COLS
WINDOW = 128
BLOCK = 128
WIN_UNROLL = 8
DIFF_HEADS = 8
DIFF_V_DIM = 2 * HEAD_DIM
DIFF_QK_COLS = 2 * DIFF_HEADS * HEAD_DIM
DIFF_V_COLS = DIFF_HEADS * DIFF_V_DIM
DIFF_COLS = 2 * DIFF_QK_COLS + DIFF_V_COLS
DIFF_Q_TILE = 8 * BLOCK
DIFF_CHUNK_BLOCKS = 16
MEM_HEADS = 4
MEM_WIDTH = MEM_HEADS * HEAD_DIM
NUM_BUCKETS = 32
N_EXPERTS = 16
CAPACITY_FACTOR = 2
N_BRANCHES = 4
NEG_INF = -1e30
LN_EPS = 1e-5
VMEM_LIMIT = 56 * 1024 * 1024
D_MODEL = 1024
IN_PROJ_TN_SHIFT = 256
IN_PROJ_TN = 1280

OFF_GATE = 0
OFF_WIN = N_BRANCHES * D_MODEL
OFF_DIFF = OFF_WIN + WIN_COLS
OFF_MEM = OFF_DIFF + DIFF_COLS
REST_COLS = OFF_MEM + MEM_WIDTH


def _cparams(*sem):
    return pltpu.CompilerParams(dimension_semantics=sem, vmem_limit_bytes=VMEM_LIMIT)


def _bdot(a, b):
    return jnp.dot(a.astype(BF16), b.astype(BF16), preferred_element_type=F32)


def _bdot_nt(a, b):
    return lax.dot_general(a.astype(BF16), b.astype(BF16), (((1,), (1,)), ((), ())),
                           preferred_element_type=F32)


def _split_dot(a, b_exact):
    hi = a.astype(BF16)
    lo = (a - hi.astype(F32)).astype(BF16)
    return (jnp.dot(hi, b_exact, preferred_element_type=F32)
            + jnp.dot(lo, b_exact, preferred_element_type=F32))


def _sigmoid(x):
    return 0.5 * jnp.tanh(0.5 * x) + 0.5


def _layer_norm(x, g, b):
    mu = jnp.mean(x, axis=-1, keepdims=True)
    xc = x - mu
    var = jnp.mean(xc * xc, axis=-1, keepdims=True)
    return xc * lax.rsqrt(var + LN_EPS) * g + b


def _inproj_shift_kernel(x_ref, w_ref, mu_ref, o_ref):
    acc = jnp.dot(x_ref[...], w_ref[...], preferred_element_type=F32)
    S = acc.shape[0]
    row = lax.broadcasted_iota(jnp.int32, (S, 1), 0)
    prev = jnp.where(row == 0, 0.0, pltpu.roll(acc, 1, 0))
    nxt = jnp.where(row == S - 1, 0.0, pltpu.roll(acc, S - 1, 0))
    out = acc + mu_ref[0:1, :] * (prev - acc) + mu_ref[1:2, :] * (nxt - acc)
    o_ref[...] = out.astype(o_ref.dtype)


def _inproj_plain_kernel(x_ref, w_ref, o_ref):
    o_ref[...] = jnp.dot(x_ref[...], w_ref[...], preferred_element_type=F32).astype(o_ref.dtype)


def _in_proj(x2, w_rwkv, w_rest, mu, seq, out_dtype):
    m, k = x2.shape
    x_spec = pl.BlockSpec((seq, k), lambda i, j: (i, 0))

    def call(kern, w, tn, extra, extra_specs, name):
        n = w.shape[1]
        assert n % tn == 0
        return pl.pallas_call(
            kern,
            grid=(m // seq, n // tn),
            in_specs=[x_spec, pl.BlockSpec((k, tn), lambda i, j: (0, j))] + extra_specs,
            out_specs=pl.BlockSpec((seq, tn), lambda i, j: (i, j)),
            out_shape=jax.ShapeDtypeStruct((m, n), out_dtype),
            compiler_params=_cparams("arbitrary", "arbitrary"),
            name=name,
        )(x2, w, *extra)

    pad = -w_rwkv.shape[1] % IN_PROJ_TN_SHIFT
    w_pad = jnp.concatenate([w_rwkv, jnp.zeros((k, pad), w_rwkv.dtype)], axis=1)
    mu_pad = jnp.concatenate([mu, jnp.zeros((2, pad), mu.dtype)], axis=1)
    p_rwkv = call(_inproj_shift_kernel, w_pad, IN_PROJ_TN_SHIFT, [mu_pad],
                  [pl.BlockSpec((2, IN_PROJ_TN_SHIFT), lambda i, j: (0, j))], "in_proj_shift")
    p_rest = call(_inproj_plain_kernel, w_rest, IN_PROJ_TN, [], [], "in_proj")
    return p_rwkv, p_rest


def _bias_kernel(tab_ref, o_ref):
    h = pl.program_id(0)
    qi = lax.broadcasted_iota(jnp.int32, (BLOCK, 3 * BLOCK), 0)
    kj = lax.broadcasted_iota(jnp.int32, (BLOCK, 3 * BLOCK), 1)
    rel = kj - BLOCK - qi
    n = jnp.abs(rel)
    n2 = n * n
    large = jnp.full_like(n, NUM_BUCKETS // 4)
    for kpow in range(1, 8):
        large = large + jnp.where(n2 >= 64 * (2 ** kpow), 1, 0)
    bucket = jnp.where(rel > 0, NUM_BUCKETS // 2, 0) + jnp.where(n < NUM_BUCKETS // 4, n, large)
    acc = jnp.zeros((BLOCK, 3 * BLOCK), F32)
    for bkt in range(NUM_BUCKETS):
        acc = jnp.where(bucket == bkt, tab_ref[h, bkt], acc)
    o_ref[0] = acc


def _bias_tiles(rel_bias_t):
    nh = rel_bias_t.shape[0]
    return pl.pallas_call(
        _bias_kernel,
        grid=(nh,),
        in_specs=[pl.BlockSpec(memory_space=pltpu.SMEM)],
        out_specs=pl.BlockSpec((1, BLOCK, 3 * BLOCK), lambda h: (h, 0, 0)),
        out_shape=jax.ShapeDtypeStruct((nh, BLOCK, 3 * BLOCK), F32),
        compiler_params=_cparams("arbitrary"),
        name="t5_bias_tiles",
    )(rel_bias_t)


def _rwkv_kernel(pr_ref, pk_ref, pv_ref, pl_ref,
                 w0_ref, wup_ref, a0_ref, aup_ref, gup_ref, kk_ref, ka_ref, rk_ref,
                 gng_ref, gnb_ref, o_ref,
                 kt_s, rt_s, kh_s, bh_s, cum_s, v_s, y_s, m_s, n_s, *, seq):
    S = seq
    C = RWKV_CHUNK
    nchunk = S // C
    row = lax.broadcasted_iota(jnp.int32, (S, 1), 0)
    lane = lax.broadcasted_iota(jnp.int32, (1, LANES), 1)
    head0 = lane < HEAD_DIM
    ri = lax.broadcasted_iota(jnp.int32, (LANES, LANES), 0)
    ci = lax.broadcasted_iota(jnp.int32, (LANES, LANES), 1)
    same_head = (ri < HEAD_DIM) == (ci < HEAD_DIM)
    eye = ri == ci
    seg_ones = jnp.where(same_head, 1.0, 0.0).astype(BF16)

    GRP = min(S, 4 * C)
    gi = lax.broadcasted_iota(jnp.int32, (GRP, GRP), 0)
    gj = lax.broadcasted_iota(jnp.int32, (GRP, GRP), 1)
    same_chunk = (gi // C) == (gj // C)
    scan_tri = (jnp.where(same_chunk & (gj <= gi), 1.0, 0.0).astype(BF16),
                jnp.where(same_chunk & (gj >= gi), 1.0, 0.0).astype(BF16))

    r = pr_ref[0].astype(F32)
    k = pk_ref[0].astype(F32)
    v = pv_ref[0].astype(F32)
    lo = pl_ref[0].astype(F32)
    l1 = lo[:, :LANES]
    l2 = lo[:, LANES:]
    g = _bdot(_sigmoid(l2), gup_ref[...])
    kkr = k * kk_ref[...]
    kk = kkr * lax.rsqrt(jnp.maximum(_split_dot(kkr * kkr, seg_ones), 1e-12))
    v_s[...] = v
    tanh_l1 = jnp.tanh(l1)
    kd_sum = jnp.zeros((S, LANES), F32)

    for z in range(2):
        wl = w0_ref[z:z + 1, :] + _bdot(tanh_l1, wup_ref[z])
        logw = -math.exp(-0.5) * _sigmoid(wl)
        a = _sigmoid(a0_ref[z:z + 1, :] + _bdot(l1, aup_ref[z]))
        kd = k * (1.0 + (a - 1.0) * ka_ref[...])
        bb = a * kk
        kd_sum = kd_sum + kd
        lw_hi = logw.astype(BF16)
        lw_lo = (logw - lw_hi.astype(F32)).astype(BF16)
        tri = scan_tri[z]
        cum = jnp.concatenate(
            [jnp.dot(tri, lw_hi[g0:g0 + GRP], preferred_element_type=F32)
             + jnp.dot(tri, lw_lo[g0:g0 + GRP], preferred_element_type=F32)
             for g0 in range(0, S, GRP)], axis=0)
        e_out = jnp.exp(-cum)
        kt_s[z] = kk * jnp.exp(cum - logw)
        rt_s[z] = r * jnp.exp(cum)
        kh_s[z] = kd * e_out
        bh_s[z] = bb * e_out
        cum_s[z] = cum

    Q = 4 * HEAD_DIM
    qi = lax.broadcasted_iota(jnp.int32, (Q, Q), 0)
    qj = lax.broadcasted_iota(jnp.int32, (Q, Q), 1)
    same_blk = (qi // HEAD_DIM) == (qj // HEAD_DIM)
    ct = lax.broadcasted_iota(jnp.int32, (C, Q), 0)
    cj = lax.broadcasted_iota(jnp.int32, (C, Q), 1)
    dt = ((cj % C) - ct) * jnp.where(cj < 2 * C, 1, -1)
    strict_m = dt < 0
    incl_m = dt <= 0

    def both(ref, rws2):
        return jnp.concatenate([ref[0, rws2[0], :], ref[1, rws2[1], :]], axis=1)

    def bd(a):
        return jnp.where(same_blk, jnp.concatenate([a.astype(BF16)] * 4, axis=0), 0.0)

    def dot(a, b):
        return jnp.dot(a.astype(BF16), b, preferred_element_type=F32)

    def dot_nt(a, b):
        return lax.dot_general(a.astype(BF16), b, (((1,), (1,)), ((), ())), preferred_element_type=F32)

    U = min(RWKV_UNROLL, nchunk)

    def chunk_of(g, u, z):
        c = g * U + u
        return c if z == 0 else nchunk - 1 - c

    def rows_of(c):
        return pl.ds(c * C, C) if isinstance(c, int) else pl.ds(pl.multiple_of(c * C, C), C)

    def phase1(g, between=lambda k: None):
        ur = range(U)
        cs = [(chunk_of(g, u, 0), chunk_of(g, u, 1)) for u in ur]
        rws = [(rows_of(c0_), rows_of(c1_)) for c0_, c1_ in cs]
        kt = [both(kt_s, r_) for r_ in rws]
        rt = [both(rt_s, r_) for r_ in rws]
        vbd = [bd(jnp.concatenate([v_s[r_[0], :], v_s[r_[1], :]], axis=1)) for r_ in rws]
        lhs = [jnp.concatenate([kt[u], rt[u]], axis=0) for u in ur]
        gb = [dot_nt(lhs[u], bd(both(bh_s, rws[u]))) for u in ur]
        gk = [dot_nt(lhs[u], bd(both(kh_s, rws[u]))) for u in ur]
        between(0)
        a_rb = [jnp.where(incl_m, gb[u][C:], 0.0) for u in ur]
        a_kr = [jnp.concatenate([jnp.where(strict_m, gk[u][:C], 0.0),
                                 jnp.where(incl_m, gk[u][C:], 0.0)], axis=0) for u in ur]
        yy = [jnp.where(strict_m, -gb[u][:C], 0.0) for u in ur]
        x = list(yy)
        p = [dot(yy[u], bd(yy[u])) for u in ur]
        yy = p
        between(1)
        for step in range(1, 6):
            last = step == 5
            p = [dot(x[u] if last else jnp.concatenate([yy[u], x[u]], axis=0), bd(yy[u])) for u in ur]
            x = [x[u] + yy[u] + (p[u] if last else p[u][C:]) for u in ur]
            if not last:
                yy = [p[u][:C] for u in ur]
            between(1 + step)
        avr = [dot(a_kr[u], vbd[u]) for u in ur]
        between(7)
        tz = []
        for u in ur:
            av = avr[u][:C]
            zz = jnp.concatenate([kt[u], av], axis=1)
            tz.append(zz + dot(x[u], jnp.concatenate([bd(kt[u]), bd(av)], axis=1)))
        cor = [dot(a_rb[u], jnp.concatenate([bd(tz[u][:, :Q]), bd(tz[u][:, Q:])], axis=1)) for u in ur]
        for u in ur:
            rr = rt[u] - cor[u][:, :Q]
            yl = avr[u][C:] - cor[u][:, Q:]
            for z in range(2):
                zl = slice(z * LANES, (z + 1) * LANES)
                rz = rws[u][z]
                y_s[z, rz, :] = yl[:, zl]
                rt_s[z, rz, :] = rr[:, zl]
                far = C - 1 if z == 0 else 0
                pt = jnp.exp(cum_s[z, rz, :][far:far + 1, :])
                ends = jnp.concatenate([bh_s[z, rz, :] * pt, kh_s[z, rz, :] * pt], axis=0)
                rhs = jnp.concatenate(
                    [jnp.concatenate([-tz[u][:, zl], -tz[u][:, Q + z * LANES:Q + (z + 1) * LANES]], axis=1),
                     jnp.concatenate([jnp.zeros((C, LANES), F32), v_s[rz, :]], axis=1)], axis=0)
                mn = _bdot(ends.T, rhs)
                m_s[z, cs[u][z]] = jnp.where(eye, jnp.broadcast_to(pt, (LANES, LANES)), 0.0) \
                    + jnp.where(same_head, mn[:, :LANES], 0.0)
                n_s[z, cs[u][z]] = jnp.where(same_head, mn[:, LANES:], 0.0)

    def carry_step(g, u, hs):
        for z in range(2):
            c = chunk_of(g, u, z)
            rows = rows_of(c)
            y_s[z, rows, :] = y_s[z, rows, :] + dot(rt_s[z, rows, :], hs[z])
            hs[z] = (dot(m_s[z, c], hs[z]) + n_s[z, c]).astype(BF16)

    def phase2(g, hs):
        hs = list(hs)
        for u in range(U):
            carry_step(g, u, hs)
        return tuple(hs)

    def lagged(g, hs):
        hs = list(hs)
        steps_per_gap = -(-U // 8)

        def between(k):
            for u in range(k * steps_per_gap, min(U, (k + 1) * steps_per_gap)):
                carry_step(g - 1, u, hs)

        phase1(g, between)
        return tuple(hs)

    ngroup = nchunk // U
    h0 = jnp.zeros((LANES, LANES), BF16)
    phase1(0)
    hs = lax.fori_loop(1, ngroup, lagged, (h0, h0))
    phase2(ngroup - 1, hs)

    y = y_s[0] + y_s[1]
    inv_n = 1.0 / HEAD_DIM
    mu_y = _split_dot(y, seg_ones) * inv_n
    yc = y - mu_y
    var_y = _split_dot(yc * yc, seg_ones) * inv_n
    yn = yc * lax.rsqrt(var_y + RWKV_GN_EPS) * gng_ref[...] + gnb_ref[...]
    bonus = _split_dot(r * kd_sum * rk_ref[...], seg_ones) * v
    o_ref[0] = ((yn + bonus) * g).astype(o_ref.dtype)


def _rwkv_branch(p, w0, w_up, a0, a_up, g_up, k_k, k_a, r_k, gn_g, gn_b, out_dtype):
    B, S, _ = p.shape
    npair = RWKV_DIM // LANES
    zpad = jnp.zeros((2, LANES - DECAY_LORA, RWKV_DIM), F32)
    wup_p = jnp.concatenate([w_up, zpad], axis=1).astype(BF16)
    aup_p = jnp.concatenate([zpad, a_up], axis=1).astype(BF16)
    row = lambda a: a.reshape(1, RWKV_DIM)
    lora_blk = 3 * RWKV_DIM // (2 * LANES)

    def pspec(off):
        return pl.BlockSpec((1, S, LANES), lambda b, j: (b, 0, off + j))

    def vspec(rows, off=0):
        return pl.BlockSpec((rows, LANES), lambda b, j: (0, off + j))

    scratch = [pltpu.VMEM((2, S, LANES), F32) for _ in range(5)] + [
        pltpu.VMEM((S, LANES), F32), pltpu.VMEM((2, S, LANES), F32),
        pltpu.VMEM((2, S // RWKV_CHUNK, LANES, LANES), F32),
        pltpu.VMEM((2, S // RWKV_CHUNK, LANES, LANES), F32)]
    return pl.pallas_call(
        functools.partial(_rwkv_kernel, seq=S),
        grid=(B, npair),
        in_specs=[pspec(0), pspec(npair), pspec(2 * npair),
                  pl.BlockSpec((1, S, 2 * LANES), lambda b, j: (b, 0, lora_blk)),
                  vspec(2),
                  pl.BlockSpec((2, LANES, LANES), lambda b, j: (0, 0, j)),
                  vspec(2),
                  pl.BlockSpec((2, LANES, LANES), lambda b, j: (0, 0, j)),
                  vspec(GATE_LORA), vspec(1), vspec(1), vspec(1), vspec(1), vspec(1)],
        out_specs=pl.BlockSpec((1, S, LANES), lambda b, j: (b, 0, j)),
        out_shape=jax.ShapeDtypeStruct((B, S, RWKV_DIM), out_dtype),
        scratch_shapes=scratch,
        compiler_params=_cparams("arbitrary", "arbitrary"),
        name="rwkv7_scan",
    )(p, p, p, p, w0, wup_p, a0, aup_p, g_up.astype(BF16),
      row(k_k), row(k_a), row(r_k), row(gn_g), row(gn_b))


def _win_kernel(q_ref, k_ref, v_ref, biast_ref, sink_ref, o_ref, k4_s, vt_s, *, seq):
    S = seq
    G = WIN_GROUP
    NB = S // BLOCK
    GW = G * HEAD_DIM
    hk = pl.program_id(1)
    odd = (hk % 2) == 1
    lane = lax.broadcasted_iota(jnp.int32, (1, LANES), 1)

    def own_head_twice(full):
        mine_first = jnp.where(odd, pltpu.roll(full, HEAD_DIM, 1), full)
        return jnp.where(lane < HEAD_DIM, mine_first, pltpu.roll(mine_first, HEAD_DIM, 1))

    k2 = own_head_twice(k_ref[0].astype(F32)).astype(BF16)
    v2 = own_head_twice(v_ref[0].astype(F32))
    zero_k = jnp.zeros((BLOCK, GW), BF16)
    zero_v = jnp.zeros((HEAD_DIM, BLOCK), BF16)
    k4_s[0] = zero_k
    k4_s[NB + 1] = zero_k
    vt_s[0] = zero_v
    vt_s[NB + 1] = zero_v
    for blk in range(NB):
        rows = slice(blk * BLOCK, (blk + 1) * BLOCK)
        k4_s[blk + 1] = jnp.concatenate([k2[rows], k2[rows]], axis=1)
        vt_s[blk + 1] = v2[rows].T[:HEAD_DIM].astype(BF16)
    bias_t = jnp.concatenate([biast_ref[gi] for gi in range(G)], axis=1)
    krow = lax.broadcasted_iota(jnp.int32, (3 * BLOCK, G * BLOCK), 0)
    qcol = lax.broadcasted_iota(jnp.int32, (3 * BLOCK, G * BLOCK), 1) % BLOCK
    bias_t = jnp.where(jnp.abs(krow - BLOCK - qcol) <= WINDOW, bias_t, NEG_INF)
    cgrp = lax.broadcasted_iota(jnp.int32, (1, G * BLOCK), 1) // BLOCK
    sink_row = jnp.zeros((1, G * BLOCK), F32)
    for gi in range(G):
        sink_row = jnp.where(cgrp == gi, sink_ref[hk * G + gi], sink_row)
    lane_q = lax.broadcasted_iota(jnp.int32, (1, GW), 1) // HEAD_DIM
    U = min(WIN_UNROLL, NB)

    def body(it, carry):
        ns = [it * U + u for u in range(U)]
        sts = []
        for n in ns:
            qb = q_ref[0, pl.ds(pl.multiple_of(n * BLOCK, BLOCK), BLOCK), :].astype(F32) * (HEAD_DIM ** -0.5)
            qm = jnp.concatenate([jnp.where(lane_q == gi, qb, 0.0) for gi in range(G)], axis=0).astype(BF16)
            kband = jnp.concatenate([k4_s[n], k4_s[n + 1], k4_s[n + 2]], axis=0)
            sts.append(lax.dot_general(kband, qm, (((1,), (1,)), ((), ())), preferred_element_type=F32))
        es, dens = [], []
        for u, n in enumerate(ns):
            st = sts[u] + bias_t
            st = jnp.concatenate([st[:BLOCK] + jnp.where(n == 0, NEG_INF, 0.0), st[BLOCK:2 * BLOCK],
                                  st[2 * BLOCK:] + jnp.where(n == NB - 1, NEG_INF, 0.0)], axis=0)
            m = jnp.maximum(jnp.max(st, axis=0, keepdims=True), sink_row)
            e = jnp.exp(st - m)
            es.append(e.astype(BF16))
            dens.append(jnp.sum(e, axis=0, keepdims=True) + jnp.exp(sink_row - m))
        for u, n in enumerate(ns):
            vband = jnp.concatenate([vt_s[n], vt_s[n + 1], vt_s[n + 2]], axis=1)
            acc = jnp.dot(vband, es[u], preferred_element_type=F32) / dens[u]
            o_t = jnp.concatenate([acc[:, gi * BLOCK:(gi + 1) * BLOCK] for gi in range(G)], axis=0)
            o_ref[0, pl.ds(pl.multiple_of(n * BLOCK, BLOCK), BLOCK), :] = o_t.T.astype(o_ref.dtype)
        return carry

    lax.fori_loop(0, NB // U, body, 0)


def _window_branch(p, bias_tiles_t, sink, out_dtype):
    B, S, _ = p.shape
    gw = WIN_GROUP * HEAD_DIM
    nb = S // BLOCK
    assert nb % min(WIN_UNROLL, nb) == 0
    qoff = OFF_WIN // gw
    koff = (OFF_WIN + WIN_Q_COLS) // LANES
    voff = (OFF_WIN + WIN_Q_COLS + WIN_KV_COLS) // LANES
    return pl.pallas_call(
        functools.partial(_win_kernel, seq=S),
        grid=(B, WIN_KV_HEADS),
        in_specs=[pl.BlockSpec((1, S, gw), lambda b, h: (b, 0, qoff + h)),
                  pl.BlockSpec((1, S, LANES), lambda b, h: (b, 0, koff + h // 2)),
                  pl.BlockSpec((1, S, LANES), lambda b, h: (b, 0, voff + h // 2)),
                  pl.BlockSpec((WIN_GROUP, 3 * BLOCK, BLOCK), lambda b, h: (h, 0, 0)),
                  pl.BlockSpec(memory_space=pltpu.SMEM)],
        out_specs=pl.BlockSpec((1, S, gw), lambda b, h: (b, 0, h)),
        out_shape=jax.ShapeDtypeStruct((B, S, WIN_Q_COLS), out_dtype),
        scratch_shapes=[pltpu.VMEM((nb + 2, BLOCK, gw), BF16),
                        pltpu.VMEM((nb + 2, HEAD_DIM, BLOCK), BF16)],
        compiler_params=_cparams("arbitrary", "arbitrary"),
        name="window_attn",
    )(p, p, p, bias_tiles_t, sink)


def _diff_kernel(q_ref, k_ref, v_ref, bandt_ref, tab_ref, lam_ref, g_ref, o_ref, ka_s, vt_s, *,
                 seq, lambda_init):
    S = seq
    NB = S // BLOCK
    CB = min(DIFF_CHUNK_BLOCKS, NB)
    h = pl.program_id(1)
    it = pl.program_id(2)
    halves = q_ref.shape[1] // BLOCK
    log2e = math.log2(math.e)
    lane = lax.broadcasted_iota(jnp.int32, (1, LANES), 1)
    c0 = lane < HEAD_DIM

    @pl.when(it == 0)
    def _():
        jblk = lax.broadcasted_iota(jnp.int32, (S, LANES), 0) // BLOCK
        lj = lax.broadcasted_iota(jnp.int32, (S, LANES), 1)
        ka_s[:, :LANES] = k_ref[0].astype(BF16)
        ka_s[:, LANES:] = jnp.where((lj < 2 * NB) & (lj % NB == jblk), 1.0, 0.0).astype(BF16)
        for blk in range(NB):
            vt_s[blk] = v_ref[0, blk * BLOCK:(blk + 1) * BLOCK, :].astype(F32).T.astype(BF16)

    far_l = tab_ref[WIN_Q_HEADS + h, NUM_BUCKETS // 2 - 1] * log2e
    far_r = tab_ref[WIN_Q_HEADS + h, NUM_BUCKETS - 1] * log2e
    band_t = bandt_ref[0] * log2e
    lam = lam_ref[...].astype(F32)
    lam_full = (jnp.exp(jnp.sum(lam[0:1] * lam[1:2], axis=-1, keepdims=True))
                - jnp.exp(jnp.sum(lam[2:3] * lam[3:4], axis=-1, keepdims=True)) + lambda_init)
    ibs = [it * halves + hf for hf in range(halves)]
    zblk = jnp.zeros((BLOCK, BLOCK), F32)
    qq, bands = [], []
    for hf in range(halves):
        q = q_ref[0, hf * BLOCK:(hf + 1) * BLOCK, :].astype(F32) * (log2e * HEAD_DIM ** -0.5)
        kb = lane % NB
        val = jnp.where(kb < ibs[hf] - 1, far_l, jnp.where(kb > ibs[hf] + 1, far_r, 0.0))
        val = jnp.where(lane < 2 * NB, val, 0.0)
        hi = val.astype(BF16)
        lo = (val - hi.astype(F32)).astype(BF16)
        bcols = jnp.broadcast_to(jnp.where(lane < NB, hi, lo), (BLOCK, LANES))
        qq.append(jnp.concatenate(
            [jnp.concatenate([jnp.where(c0, q, 0.0).astype(BF16), bcols], axis=1),
             jnp.concatenate([jnp.where(c0, 0.0, q).astype(BF16), bcols], axis=1)], axis=0))
        tiles = [jnp.where(ibs[hf] > 0, band_t[:BLOCK], 0.0), band_t[BLOCK:2 * BLOCK],
                 jnp.where(ibs[hf] < NB - 1, band_t[2 * BLOCK:], 0.0)]
        col = jnp.concatenate([zblk] * hf + tiles + [zblk] * (CB - 3 - hf), axis=0)
        bands.append(jnp.concatenate([col, col], axis=1))

    m_run, l_run, acc = [None] * halves, [None] * halves, [None] * halves
    for r in range(NB // CB):
        kbs = [lax.rem(it * halves - 1 + CB * r + u + NB, NB) for u in range(CB)]
        kc = jnp.concatenate([ka_s[pl.ds(pl.multiple_of(kb * BLOCK, BLOCK), BLOCK), :] for kb in kbs], axis=0)
        vct = jnp.concatenate([vt_s[kb] for kb in kbs], axis=1)
        s = [lax.dot_general(kc, qq[hf], (((1,), (1,)), ((), ())), preferred_element_type=F32)
             for hf in range(halves)]
        if r == 0:
            s = [s[hf] + bands[hf] for hf in range(halves)]
        m_loc = [jnp.max(s[hf], axis=0, keepdims=True) for hf in range(halves)]
        for hf in range(halves):
            if r == 0:
                m_new = m_loc[hf]
                e = jnp.exp2(s[hf] - m_new)
                l_run[hf] = jnp.sum(e, axis=0, keepdims=True)
                acc[hf] = jnp.dot(vct, e.astype(BF16), preferred_element_type=F32)
            else:
                m_new = jnp.maximum(m_run[hf], m_loc[hf])
                alpha = jnp.exp2(m_run[hf] - m_new)
                e = jnp.exp2(s[hf] - m_new)
                l_run[hf] = alpha * l_run[hf] + jnp.sum(e, axis=0, keepdims=True)
                acc[hf] = alpha * acc[hf] + jnp.dot(vct, e.astype(BF16), preferred_element_type=F32)
            m_run[hf] = m_new
    for hf in range(halves):
        rinv = 1.0 / l_run[hf]
        o_t = acc[hf][:, :BLOCK] * rinv[:, :BLOCK] - acc[hf][:, BLOCK:] * (lam_full * rinv[:, BLOCK:])
        o = o_t.T
        o = o * lax.rsqrt(jnp.mean(o * o, axis=-1, keepdims=True) + 1e-5) * g_ref[...] * (1.0 - lambda_init)
        o_ref[0, hf * BLOCK:(hf + 1) * BLOCK, :] = o.astype(o_ref.dtype)


def _diff_branch(p, band_tiles, rel_bias_t, lam, subln_g, lambda_init, out_dtype):
    B, S, _ = p.shape
    qoff = OFF_DIFF // LANES
    koff = (OFF_DIFF + DIFF_QK_COLS) // LANES
    voff = (OFF_DIFF + 2 * DIFF_QK_COLS) // LANES
    nb = S // BLOCK
    cb = min(DIFF_CHUNK_BLOCKS, nb)
    tq = DIFF_Q_TILE
    while tq > (cb - 2) * BLOCK or S % tq:
        tq //= 2
    assert 2 * nb <= LANES and tq >= BLOCK and nb % cb == 0 and nb >= 4
    return pl.pallas_call(
        functools.partial(_diff_kernel, seq=S, lambda_init=lambda_init),
        grid=(B, DIFF_HEADS, S // tq),
        in_specs=[pl.BlockSpec((1, tq, LANES), lambda b, h, i: (b, i, qoff + h)),
                  pl.BlockSpec((1, S, LANES), lambda b, h, i: (b, 0, koff + h)),
                  pl.BlockSpec((1, S, LANES), lambda b, h, i: (b, 0, voff + h)),
                  pl.BlockSpec((1, 3 * BLOCK, BLOCK), lambda b, h, i: (WIN_Q_HEADS + h, 0, 0)),
                  pl.BlockSpec(memory_space=pltpu.SMEM),
                  pl.BlockSpec((4, HEAD_DIM), lambda b, h, i: (0, 0)),
                  pl.BlockSpec((1, DIFF_V_DIM), lambda b, h, i: (0, 0))],
        out_specs=pl.BlockSpec((1, tq, LANES), lambda b, h, i: (b, i, h)),
        out_shape=jax.ShapeDtypeStruct((B, S, DIFF_V_COLS), out_dtype),
        scratch_shapes=[pltpu.VMEM((S, 2 * LANES), BF16),
                        pltpu.VMEM((S // BLOCK, DIFF_V_DIM, BLOCK), BF16)],
        compiler_params=_cparams("arbitrary", "arbitrary", "arbitrary"),
        name="diff_attn",
    )(p, p, p, jnp.swapaxes(band_tiles, 1, 2), rel_bias_t, lam, subln_g.reshape(1, DIFF_V_DIM))


def _mem_kernel(q_ref, mem_ref, wkv_ref, o_ref):
    kv = _bdot(mem_ref[0], wkv_ref[...])
    kmat = kv[:, :MEM_WIDTH]
    vmat = kv[:, MEM_WIDTH:]
    q = q_ref[0].astype(F32) * (HEAD_DIM ** -0.5)
    lane = lax.broadcasted_iota(jnp.int32, (1, MEM_WIDTH), 1) // HEAD_DIM
    out = jnp.zeros(q.shape, F32)
    for hh in range(MEM_HEADS):
        s = _bdot_nt(jnp.where(lane == hh, q, 0.0), kmat)
        m = jnp.max(s, axis=-1, keepdims=True)
        e = jnp.exp(s - m)
        a = e / jnp.sum(e, axis=-1, keepdims=True)
        out = jnp.where(lane == hh, _bdot(a, vmat), out)
    o_ref[0] = out.astype(o_ref.dtype)


def _memory_branch(p, mem, w_kv, out_dtype):
    B, S, _ = p.shape
    nm, d = mem.shape[1], mem.shape[2]
    return pl.pallas_call(
        _mem_kernel,
        grid=(B,),
        in_specs=[pl.BlockSpec((1, S, MEM_WIDTH), lambda b: (b, 0, OFF_MEM // MEM_WIDTH)),
                  pl.BlockSpec((1, nm, d), lambda b: (b, 0, 0)),
                  pl.BlockSpec((d, 2 * MEM_WIDTH), lambda b: (0, 0))],
        out_specs=pl.BlockSpec((1, S, MEM_WIDTH), lambda b: (b, 0, 0)),
        out_shape=jax.ShapeDtypeStruct((B, S, MEM_WIDTH), out_dtype),
        compiler_params=_cparams("arbitrary"),
        name="memory_attn",
    )(p, mem, w_kv)


def _merge_kernel(x_ref, yr_ref, yw_ref, yd_ref, ym_ref, pg_ref, wb_ref, wo_ref, g_ref, b_ref,
                  o_ref, *, d_model, alpha):
    D = d_model
    ys = (yr_ref[...], yw_ref[...], yd_ref[...], ym_ref[...])
    off = 0
    merged = jnp.zeros((x_ref.shape[0], D), F32)
    for bi, yb in enumerate(ys):
        rows = yb.shape[1]
        proj = _bdot(yb, wb_ref[off:off + rows, :])
        merged = merged + _sigmoid(pg_ref[:, bi * D:(bi + 1) * D].astype(F32)) * proj
        off += rows
    z = alpha * x_ref[...] + _bdot(merged, wo_ref[...])
    o_ref[...] = _layer_norm(z, g_ref[...], b_ref[...])


def _merge(x2, y_rwkv, y_win, y_diff, y_mem, p2, w_branch, w_out, ln_g, ln_b, alpha, tm):
    T, D = x2.shape
    gate_blk = OFF_GATE // (N_BRANCHES * D)
    full = lambda a: pl.BlockSpec(a.shape, lambda i: (0, 0))
    tile = lambda w: pl.BlockSpec((tm, w), lambda i: (i, 0))
    return pl.pallas_call(
        functools.partial(_merge_kernel, d_model=D, alpha=alpha),
        grid=(T // tm,),
        in_specs=[tile(D), tile(y_rwkv.shape[1]), tile(y_win.shape[1]), tile(y_diff.shape[1]),
                  tile(y_mem.shape[1]),
                  pl.BlockSpec((tm, N_BRANCHES * D), lambda i: (i, gate_blk)),
                  full(w_branch), full(w_out),
                  pl.BlockSpec((1, D), lambda i: (0, 0)), pl.BlockSpec((1, D), lambda i: (0, 0))],
        out_specs=tile(D),
        out_shape=jax.ShapeDtypeStruct((T, D), F32),
        compiler_params=_cparams("arbitrary"),
        name="merge_out_ln",
    )(x2, y_rwkv, y_win, y_diff, y_mem, p2, w_branch, w_out, ln_g.reshape(1, D), ln_b.reshape(1, D))


def _route_kernel(x_ref, wr_ref, xe_ref, slot_ref, gate_ref, xb_s, lg_s, *, seq, cap):
    S = seq
    E = N_EXPERTS
    TR = min(S, 512)
    wr = wr_ref[...]
    w_hi = wr.astype(BF16)
    w_md = (wr - w_hi.astype(F32)).astype(BF16)
    dot = lambda a, b: jnp.dot(a, b, preferred_element_type=F32)

    def logit_tile(t, carry):
        rows = pl.ds(pl.multiple_of(t * TR, TR), TR)
        x = x_ref[0, rows, :]
        x_hi = x.astype(BF16)
        x_r1 = x - x_hi.astype(F32)
        x_md = x_r1.astype(BF16)
        x_lo = (x_r1 - x_md.astype(F32)).astype(BF16)
        xb_s[rows, :] = x_hi
        lg_s[rows, :] = (dot(x_hi, w_hi) + (dot(x_hi, w_md) + dot(x_md, w_hi))
                         + (dot(x_md, w_md) + dot(x_lo, w_hi)))
        return carry

    lax.fori_loop(0, S // TR, logit_tile, 0)
    col = lax.broadcasted_iota(jnp.int32, (1, LANES), 1)
    logits = jnp.where(col < E, lg_s[...], NEG_INF)
    mx = jnp.max(logits, axis=-1, keepdims=True)
    ex = jnp.exp(logits - mx)
    aff = ex / jnp.sum(ex, axis=-1, keepdims=True)
    aff_t = aff.T[0:E, :]
    bits = pltpu.bitcast(aff_t, jnp.int32)

    def bis(_, carry):
        lo_b, step = carry
        cand = lo_b + step
        cnt = jnp.sum(jnp.where(bits >= cand, 1.0, 0.0), axis=-1, keepdims=True)
        return jnp.where(cnt >= cap, cand, lo_b), step // 2

    lo0 = jnp.zeros((E, 1), jnp.int32)
    thr, _ = lax.fori_loop(0, 31, bis, (lo0, jnp.full((E, 1), 1 << 30, jnp.int32)))
    above = bits > thr
    tie = bits == thr
    n_above = jnp.sum(jnp.where(above, 1.0, 0.0), axis=-1, keepdims=True)
    PB = min(S, 256)
    tri = (lax.broadcasted_iota(jnp.int32, (PB, PB), 0)
           < lax.broadcasted_iota(jnp.int32, (PB, PB), 1)).astype(BF16)

    def prefix_count(mask_f):
        parts, run = [], jnp.zeros((E, 1), F32)
        for blk in range(S // PB):
            seg = mask_f[:, blk * PB:(blk + 1) * PB]
            parts.append(jnp.dot(seg.astype(BF16), tri, preferred_element_type=F32) + run)
            run = run + jnp.sum(seg, axis=-1, keepdims=True)
        return jnp.concatenate(parts, axis=1)

    tie_rank = prefix_count(jnp.where(tie, 1.0, 0.0))
    sel = above | (tie & (tie_rank < cap - n_above))
    pos = prefix_count(jnp.where(sel, 1.0, 0.0))
    slot = jnp.where(sel, pos, -1.0)
    slot_pad = jnp.concatenate([slot, jnp.full((LANES - E, S), -1.0, F32)], axis=0)
    slot_ref[0] = slot_pad.T
    ci = lax.broadcasted_iota(jnp.int32, (cap, S), 0).astype(F32)
    x_hi = xb_s[...]
    for e in range(E):
        onehot = slot[e:e + 1, :] == ci
        xe_ref[e, 0] = jnp.dot(jnp.where(onehot, 1.0, 0.0).astype(BF16), x_hi,
                               preferred_element_type=F32).astype(xe_ref.dtype)
        gsel = jnp.sum(jnp.where(onehot, aff_t[e:e + 1, :], 0.0), axis=-1, keepdims=True)
        gate_ref[e, 0] = jnp.broadcast_to(gsel, (cap, LANES))


def _route(x3, router_pad, cap):
    B, S, D = x3.shape
    E = N_EXPERTS
    return pl.pallas_call(
        functools.partial(_route_kernel, seq=S, cap=cap),
        grid=(B,),
        in_specs=[pl.BlockSpec((1, S, D), lambda b: (b, 0, 0)),
                  pl.BlockSpec((D, LANES), lambda b: (0, 0))],
        out_specs=[pl.BlockSpec((E, 1, cap, D), lambda b: (0, b, 0, 0)),
                   pl.BlockSpec((1, S, LANES), lambda b: (b, 0, 0)),
                   pl.BlockSpec((E, 1, cap, LANES), lambda b: (0, b, 0, 0))],
        out_shape=[jax.ShapeDtypeStruct((E, B, cap, D), BF16),
                   jax.ShapeDtypeStruct((B, S, LANES), F32),
                   jax.ShapeDtypeStruct((E, B, cap, LANES), F32)],
        scratch_shapes=[pltpu.VMEM((S, D), BF16), pltpu.VMEM((S, LANES), F32)],
        compiler_params=_cparams("arbitrary"),
        name="moe_route_gather",
    )(x3, router_pad)


def _ffn_kernel(xe_ref, g_ref, wg_ref, wu_ref, wd_ref, ye_ref, acc_s):
    f = pl.program_id(1)
    nb, cap, D = xe_ref.shape[1], xe_ref.shape[2], xe_ref.shape[3]

    @pl.when(f == 0)
    def _():
        acc_s[...] = jnp.zeros(acc_s.shape, F32)

    xe = xe_ref[0].reshape(nb * cap, D)
    hg = jnp.dot(xe, wg_ref[0, 0].astype(BF16), preferred_element_type=F32)
    hu = jnp.dot(xe, wu_ref[0, 0].astype(BF16), preferred_element_type=F32)
    hid = (hg * _sigmoid(hg)) * hu
    acc_s[...] = acc_s[...] + _bdot(hid, wd_ref[0, 0])

    @pl.when(f == pl.num_programs(1) - 1)
    def _():
        gate = g_ref[0].reshape(nb * cap, LANES)[:, 0:1]
        ye_ref[0] = (acc_s[...] * gate).reshape(nb, cap, D).astype(ye_ref.dtype)


def _expert_ffn(xe, gate, w_gate, w_up, w_down, layer, tf):
    E, B, cap, D = xe.shape
    F = w_gate.shape[-1]
    return pl.pallas_call(
        _ffn_kernel,
        grid=(E, F // tf),
        in_specs=[pl.BlockSpec((1, B, cap, D), lambda e, f: (e, 0, 0, 0)),
                  pl.BlockSpec((1, B, cap, LANES), lambda e, f: (e, 0, 0, 0)),
                  pl.BlockSpec((1, 1, D, tf), lambda e, f: (layer, e, 0, f)),
                  pl.BlockSpec((1, 1, D, tf), lambda e, f: (layer, e, 0, f)),
                  pl.BlockSpec((1, 1, tf, D), lambda e, f: (layer, e, f, 0))],
        out_specs=pl.BlockSpec((1, B, cap, D), lambda e, f: (e, 0, 0, 0)),
        out_shape=jax.ShapeDtypeStruct((E, B, cap, D), BF16),
        scratch_shapes=[pltpu.VMEM((B * cap, D), F32)],
        compiler_params=_cparams("arbitrary", "arbitrary"),
        name="moe_expert_ffn",
    )(xe, gate, w_gate, w_up, w_down)


def _scatter_kernel(x_ref, slot_ref, ye_ref, g_ref, b_ref, o_ref, ob_ref, *, cap, alpha):
    ts = x_ref.shape[1]
    slot = slot_ref[0]
    ci = lax.broadcasted_iota(jnp.int32, (ts, cap), 1).astype(F32)
    acc = jnp.zeros((ts, x_ref.shape[2]), F32)
    for e in range(N_EXPERTS):
        onehot = slot[:, e:e + 1] == ci
        acc = acc + jnp.dot(jnp.where(onehot, 1.0, 0.0).astype(BF16), ye_ref[e, 0],
                            preferred_element_type=F32)
    out = _layer_norm(alpha * x_ref[0] + acc, g_ref[...], b_ref[...])
    o_ref[0] = out
    ob_ref[0] = out.astype(ob_ref.dtype)


def _scatter_ln(x3, slot, ye, ln_g, ln_b, alpha, ts):
    B, S, D = x3.shape
    E, _, cap, _ = ye.shape
    return pl.pallas_call(
        functools.partial(_scatter_kernel, cap=cap, alpha=alpha),
        grid=(B, S // ts),
        in_specs=[pl.BlockSpec((1, ts, D), lambda b, i: (b, i, 0)),
                  pl.BlockSpec((1, ts, LANES), lambda b, i: (b, i, 0)),
                  pl.BlockSpec((E, 1, cap, D), lambda b, i: (0, b, 0, 0)),
                  pl.BlockSpec((1, D), lambda b, i: (0, 0)),
                  pl.BlockSpec((1, D), lambda b, i: (0, 0))],
        out_specs=[pl.BlockSpec((1, ts, D), lambda b, i: (b, i, 0)),
                   pl.BlockSpec((1, ts, D), lambda b, i: (b, i, 0))],
        out_shape=[jax.ShapeDtypeStruct((B, S, D), F32), jax.ShapeDtypeStruct((B, S, D), BF16)],
        compiler_params=_cparams("arbitrary", "arbitrary"),
        name="moe_scatter_ln",
    )(x3, slot, ye, ln_g.reshape(1, D), ln_b.reshape(1, D))


def kernel(x, mem, rel_bias, w_in, rwkv_mu, rwkv_w0, rwkv_w_up, rwkv_a0, rwkv_a_up, rwkv_g_up, rwkv_k_k, rwkv_k_a, rwkv_r_k, rwkv_gn_g, rwkv_gn_b, win_sink, diff_lambda, diff_subln_g, mem_w_kv, w_branch, w_out, ln1_g, ln1_b, router, exp_w_gate, exp_w_up, exp_w_down, ln2_g, ln2_b):
    B, S, D = x.shape
    assert D == D_MODEL and w_in.shape[2] == RWKV_COLS + REST_COLS
    depth = w_in.shape[0]
    alpha = (2 * depth) ** 0.25
    cap = CAPACITY_FACTOR * S // N_EXPERTS
    T = B * S
    rel_bias_t = rel_bias.T
    bias_tiles = _bias_tiles(rel_bias_t)
    bias_tiles_t = jnp.swapaxes(bias_tiles, 1, 2)
    act = BF16
    xb = x.astype(BF16)
    for l in range(depth):
        lambda_init = 0.8 - 0.6 * math.exp(-0.3 * l)
        gate0 = w_in.shape[2] - N_BRANCHES * D
        w_rest = jnp.concatenate([w_in[l][:, gate0:], w_in[l][:, RWKV_COLS:gate0]], axis=1).astype(BF16)
        p_rwkv, p2 = _in_proj(xb.reshape(T, D), w_in[l][:, :RWKV_COLS].astype(BF16), w_rest,
                              rwkv_mu[l], S, act)
        p = p2.reshape(B, S, -1)
        y_rwkv = _rwkv_branch(p_rwkv.reshape(B, S, -1), rwkv_w0[l], rwkv_w_up[l], rwkv_a0[l], rwkv_a_up[l],
                              rwkv_g_up[l], rwkv_k_k[l], rwkv_k_a[l], rwkv_r_k[l], rwkv_gn_g[l],
                              rwkv_gn_b[l], act)
        y_win = _window_branch(p, bias_tiles_t, win_sink[l], act)
        y_diff = _diff_branch(p, bias_tiles, rel_bias_t, diff_lambda[l], diff_subln_g[l], lambda_init, act)
        y_mem = _memory_branch(p, mem, mem_w_kv[l].astype(BF16), act)
        x1 = _merge(x.reshape(T, D), y_rwkv.reshape(T, -1), y_win.reshape(T, -1), y_diff.reshape(T, -1),
                    y_mem.reshape(T, -1), p2, w_branch[l].astype(BF16), w_out[l].astype(BF16),
                    ln1_g[l], ln1_b[l], alpha, 256).reshape(B, S, D)
        router_pad = jnp.concatenate([router[l], jnp.zeros((D, LANES - N_EXPERTS), F32)], axis=1)
        xe, slot, gate = _route(x1, router_pad, cap)
        ye = _expert_ffn(xe, gate, exp_w_gate, exp_w_up, exp_w_down, l, 512)
        x, xb = _scatter_ln(x1, slot, ye, ln2_g[l], ln2_b[l], alpha, 512)
    return x
```

```python
import functools
import math

import jax
import jax.numpy as jnp
from jax import lax
from jax.experimental import pallas as pl
from jax.experimental.pallas import tpu as pltpu

F32 = jnp.float32
BF16 = jnp.bfloat16

HEAD_DIM = 64
LANES = 128
RWKV_HEADS = 16
RWKV_DIM = RWKV_HEADS * HEAD_DIM
DECAY_LORA = 64
ICLR_LORA = 64
GATE_LORA = 128
RWKV_GN_EPS = 64e-5
RWKV_COLS = 3 * RWKV_DIM + DECAY_LORA + ICLR_LORA + GATE_LORA
RWKV_CHUNK = 64
RWKV_UNROLL = 8
assert RWKV_CHUNK == HEAD_DIM
WIN_Q_HEADS = 16
WIN_KV_HEADS = 4
WIN_GROUP = WIN_Q_HEADS // WIN_KV_HEADS
WIN_Q_COLS = WIN_Q_HEADS * HEAD_DIM
WIN_KV_COLS = WIN_KV_HEADS * HEAD_DIM
WIN_COLS = WIN_Q_COLS + 2 * WIN_KV_COLS
WINDOW = 128
BLOCK = 128
WIN_UNROLL = 8
DIFF_HEADS = 8
DIFF_V_DIM = 2 * HEAD_DIM
DIFF_QK_COLS = 2 * DIFF_HEADS * HEAD_DIM
DIFF_V_COLS = DIFF_HEADS * DIFF_V_DIM
DIFF_COLS = 2 * DIFF_QK_COLS + DIFF_V_COLS
DIFF_Q_TILE = 8 * BLOCK
DIFF_CHUNK_BLOCKS = 16
MEM_HEADS = 4
MEM_WIDTH = MEM_HEADS * HEAD_DIM
NUM_BUCKETS = 32
N_EXPERTS = 16
CAPACITY_FACTOR = 2
N_BRANCHES = 4
NEG_INF = -1e30
LN_EPS = 1e-5
VMEM_LIMIT = 56 * 1024 * 1024
D_MODEL = 1024
IN_PROJ_TN_SHIFT = 256
IN_PROJ_TN = 1280

OFF_GATE = 0
OFF_WIN = N_BRANCHES * D_MODEL
OFF_DIFF = OFF_WIN + WIN_COLS
OFF_MEM = OFF_DIFF + DIFF_COLS
REST_COLS = OFF_MEM + MEM_WIDTH


def _cparams(*sem):
    return pltpu.CompilerParams(dimension_semantics=sem, vmem_limit_bytes=VMEM_LIMIT)


def _bdot(a, b):
    return jnp.dot(a.astype(BF16), b.astype(BF16), preferred_element_type=F32)


def _bdot_nt(a, b):
    return lax.dot_general(a.astype(BF16), b.astype(BF16), (((1,), (1,)), ((), ())),
                           preferred_element_type=F32)


def _split_dot(a, b_exact):
    hi = a.astype(BF16)
    lo = (a - hi.astype(F32)).astype(BF16)
    return (jnp.dot(hi, b_exact, preferred_element_type=F32)
            + jnp.dot(lo, b_exact, preferred_element_type=F32))


def _sigmoid(x):
    return 0.5 * jnp.tanh(0.5 * x) + 0.5


def _layer_norm(x, g, b):
    mu = jnp.mean(x, axis=-1, keepdims=True)
    xc = x - mu
    var = jnp.mean(xc * xc, axis=-1, keepdims=True)
    return xc * lax.rsqrt(var + LN_EPS) * g + b


def _inproj_shift_kernel(x_ref, w_ref, mu_ref, o_ref):
    acc = jnp.dot(x_ref[...], w_ref[...], preferred_element_type=F32)
    S = acc.shape[0]
    row = lax.broadcasted_iota(jnp.int32, (S, 1), 0)
    prev = jnp.where(row == 0, 0.0, pltpu.roll(acc, 1, 0))
    nxt = jnp.where(row == S - 1, 0.0, pltpu.roll(acc, S - 1, 0))
    out = acc + mu_ref[0:1, :] * (prev - acc) + mu_ref[1:2, :] * (nxt - acc)
    o_ref[...] = out.astype(o_ref.dtype)


def _inproj_plain_kernel(x_ref, w_ref, o_ref):
    o_ref[...] = jnp.dot(x_ref[...], w_ref[...], preferred_element_type=F32).astype(o_ref.dtype)


def _in_proj(x2, w_rwkv, w_rest, mu, seq, out_dtype):
    m, k = x2.shape
    x_spec = pl.BlockSpec((seq, k), lambda i, j: (i, 0))

    def call(kern, w, tn, extra, extra_specs, name):
        n = w.shape[1]
        assert n % tn == 0
        return pl.pallas_call(
            kern,
            grid=(m // seq, n // tn),
            in_specs=[x_spec, pl.BlockSpec((k, tn), lambda i, j: (0, j))] + extra_specs,
            out_specs=pl.BlockSpec((seq, tn), lambda i, j: (i, j)),
            out_shape=jax.ShapeDtypeStruct((m, n), out_dtype),
            compiler_params=_cparams("arbitrary", "arbitrary"),
            name=name,
        )(x2, w, *extra)

    pad = -w_rwkv.shape[1] % IN_PROJ_TN_SHIFT
    w_pad = jnp.concatenate([w_rwkv, jnp.zeros((k, pad), w_rwkv.dtype)], axis=1)
    mu_pad = jnp.concatenate([mu, jnp.zeros((2, pad), mu.dtype)], axis=1)
    p_rwkv = call(_inproj_shift_kernel, w_pad, IN_PROJ_TN_SHIFT, [mu_pad],
                  [pl.BlockSpec((2, IN_PROJ_TN_SHIFT), lambda i, j: (0, j))], "in_proj_shift")
    p_rest = call(_inproj_plain_kernel, w_rest, IN_PROJ_TN, [], [], "in_proj")
    return p_rwkv, p_rest


def _bias_kernel(tab_ref, o_ref):
    h = pl.program_id(0)
    qi = lax.broadcasted_iota(jnp.int32, (BLOCK, 3 * BLOCK), 0)
    kj = lax.broadcasted_iota(jnp.int32, (BLOCK, 3 * BLOCK), 1)
    rel = kj - BLOCK - qi
    n = jnp.abs(rel)
    n2 = n * n
    large = jnp.full_like(n, NUM_BUCKETS // 4)
    for kpow in range(1, 8):
        large = large + jnp.where(n2 >= 64 * (2 ** kpow), 1, 0)
    bucket = jnp.where(rel > 0, NUM_BUCKETS // 2, 0) + jnp.where(n < NUM_BUCKETS // 4, n, large)
    acc = jnp.zeros((BLOCK, 3 * BLOCK), F32)
    for bkt in range(NUM_BUCKETS):
        acc = jnp.where(bucket == bkt, tab_ref[h, bkt], acc)
    o_ref[0] = acc


def _bias_tiles(rel_bias_t):
    nh = rel_bias_t.shape[0]
    return pl.pallas_call(
        _bias_kernel,
        grid=(nh,),
        in_specs=[pl.BlockSpec(memory_space=pltpu.SMEM)],
        out_specs=pl.BlockSpec((1, BLOCK, 3 * BLOCK), lambda h: (h, 0, 0)),
        out_shape=jax.ShapeDtypeStruct((nh, BLOCK, 3 * BLOCK), F32),
        compiler_params=_cparams("arbitrary"),
        name="t5_bias_tiles",
    )(rel_bias_t)


def _rwkv_kernel(pr_ref, pk_ref, pv_ref, pl_ref,
                 w0_ref, wup_ref, a0_ref, aup_ref, gup_ref, kk_ref, ka_ref, rk_ref,
                 gng_ref, gnb_ref, o_ref,
                 kt_s, rt_s, kh_s, bh_s, cum_s, v_s, y_s, m_s, n_s, *, seq):
    S = seq
    C = RWKV_CHUNK
    nchunk = S // C
    row = lax.broadcasted_iota(jnp.int32, (S, 1), 0)
    lane = lax.broadcasted_iota(jnp.int32, (1, LANES), 1)
    head0 = lane < HEAD_DIM
    ri = lax.broadcasted_iota(jnp.int32, (LANES, LANES), 0)
    ci = lax.broadcasted_iota(jnp.int32, (LANES, LANES), 1)
    same_head = (ri < HEAD_DIM) == (ci < HEAD_DIM)
    eye = ri == ci
    seg_ones = jnp.where(same_head, 1.0, 0.0).astype(BF16)

    GRP = min(S, 4 * C)
    gi = lax.broadcasted_iota(jnp.int32, (GRP, GRP), 0)
    gj = lax.broadcasted_iota(jnp.int32, (GRP, GRP), 1)
    same_chunk = (gi // C) == (gj // C)
    scan_tri = (jnp.where(same_chunk & (gj <= gi), 1.0, 0.0).astype(BF16),
                jnp.where(same_chunk & (gj >= gi), 1.0, 0.0).astype(BF16))

    r = pr_ref[0].astype(F32)
    k = pk_ref[0].astype(F32)
    v = pv_ref[0].astype(F32)
    lo = pl_ref[0].astype(F32)
    l1 = lo[:, :LANES]
    l2 = lo[:, LANES:]
    g = _bdot(_sigmoid(l2), gup_ref[...])
    kkr = k * kk_ref[...]
    kk = kkr * lax.rsqrt(jnp.maximum(_split_dot(kkr * kkr, seg_ones), 1e-12))
    v_s[...] = v
    tanh_l1 = jnp.tanh(l1)
    kd_sum = jnp.zeros((S, LANES), F32)

    for z in range(2):
        wl = w0_ref[z:z + 1, :] + _bdot(tanh_l1, wup_ref[z])
        half_c = 0.5 * math.exp(-0.5) * math.log2(math.e)
        logw = -half_c * jnp.tanh(0.5 * wl) - half_c
        a = _sigmoid(a0_ref[z:z + 1, :] + _bdot(l1, aup_ref[z]))
        kd = k * (1.0 + (a - 1.0) * ka_ref[...])
        bb = a * kk
        kd_sum = kd_sum + kd
        lw_hi = logw.astype(BF16)
        lw_lo = (logw - lw_hi.astype(F32)).astype(BF16)
        tri = scan_tri[z]
        cum = jnp.concatenate(
            [jnp.dot(tri, lw_hi[g0:g0 + GRP], preferred_element_type=F32)
             + jnp.dot(tri, lw_lo[g0:g0 + GRP], preferred_element_type=F32)
             for g0 in range(0, S, GRP)], axis=0)
        e_out = jnp.exp2(-cum)
        kt_s[z] = kk * jnp.exp2(cum - logw)
        rt_s[z] = r * jnp.exp2(cum)
        kh_s[z] = kd * e_out
        bh_s[z] = bb * e_out
        cum_s[z] = cum

    Q = 4 * HEAD_DIM
    qi = lax.broadcasted_iota(jnp.int32, (Q, Q), 0)
    qj = lax.broadcasted_iota(jnp.int32, (Q, Q), 1)
    same_blk = (qi // HEAD_DIM) == (qj // HEAD_DIM)
    ct = lax.broadcasted_iota(jnp.int32, (C, Q), 0)
    cj = lax.broadcasted_iota(jnp.int32, (C, Q), 1)
    dt = ((cj % C) - ct) * jnp.where(cj < 2 * C, 1, -1)
    strict_m = dt < 0
    incl_m = dt <= 0

    def both(ref, rws2):
        return jnp.concatenate([ref[0, rws2[0], :], ref[1, rws2[1], :]], axis=1)

    def bd(a):
        return jnp.where(same_blk, jnp.concatenate([a.astype(BF16)] * 4, axis=0), 0.0)

    def dot(a, b):
        return jnp.dot(a.astype(BF16), b, preferred_element_type=F32)

    def dot_nt(a, b):
        return lax.dot_general(a.astype(BF16), b, (((1,), (1,)), ((), ())), preferred_element_type=F32)

    U = min(RWKV_UNROLL, nchunk)

    def chunk_of(g, u, z):
        c = g * U + u
        return c if z == 0 else nchunk - 1 - c

    def rows_of(c):
        return pl.ds(c * C, C) if isinstance(c, int) else pl.ds(pl.multiple_of(c * C, C), C)

    def phase1(g, between=lambda k: None):
        ur = range(U)
        cs = [(chunk_of(g, u, 0), chunk_of(g, u, 1)) for u in ur]
        rws = [(rows_of(c0_), rows_of(c1_)) for c0_, c1_ in cs]
        kt = [both(kt_s, r_) for r_ in rws]
        rt = [both(rt_s, r_) for r_ in rws]
        vbd = [bd(jnp.concatenate([v_s[r_[0], :], v_s[r_[1], :]], axis=1)) for r_ in rws]
        lhs = [jnp.concatenate([kt[u], rt[u]], axis=0) for u in ur]
        gb = [dot_nt(lhs[u], bd(both(bh_s, rws[u]))) for u in ur]
        gk = [dot_nt(lhs[u], bd(both(kh_s, rws[u]))) for u in ur]
        between(0)
        a_rb = [jnp.where(incl_m, gb[u][C:], 0.0) for u in ur]
        a_kr = [jnp.concatenate([jnp.where(strict_m, gk[u][:C], 0.0),
                                 jnp.where(incl_m, gk[u][C:], 0.0)], axis=0) for u in ur]
        yy = [jnp.where(strict_m, -gb[u][:C], 0.0) for u in ur]
        x = list(yy)
        p = [dot(yy[u], bd(yy[u])) for u in ur]
        yy = p
        between(1)
        for step in range(1, 6):
            last = step == 5
            p = [dot(x[u] if last else jnp.concatenate([yy[u], x[u]], axis=0), bd(yy[u])) for u in ur]
            x = [x[u] + yy[u] + (p[u] if last else p[u][C:]) for u in ur]
            if not last:
                yy = [p[u][:C] for u in ur]
            between(1 + step)
        avr = [dot(a_kr[u], vbd[u]) for u in ur]
        between(7)
        tz = []
        for u in ur:
            av = avr[u][:C]
            zz = jnp.concatenate([kt[u], av], axis=1)
            tz.append(zz + dot(x[u], jnp.concatenate([bd(kt[u]), bd(av)], axis=1)))
        cor = [dot(a_rb[u], jnp.concatenate([bd(tz[u][:, :Q]), bd(tz[u][:, Q:])], axis=1)) for u in ur]
        for u in ur:
            rr = rt[u] - cor[u][:, :Q]
            yl = avr[u][C:] - cor[u][:, Q:]
            for z in range(2):
                zl = slice(z * LANES, (z + 1) * LANES)
                rz = rws[u][z]
                y_s[z, rz, :] = yl[:, zl]
                rt_s[z, rz, :] = rr[:, zl]
                far = C - 1 if z == 0 else 0
                pt = jnp.exp2(cum_s[z, rz, :][far:far + 1, :])
                ends = jnp.concatenate([bh_s[z, rz, :] * pt, kh_s[z, rz, :] * pt], axis=0)
                rhs = jnp.concatenate(
                    [jnp.concatenate([-tz[u][:, zl], -tz[u][:, Q + z * LANES:Q + (z + 1) * LANES]], axis=1),
                     jnp.concatenate([jnp.zeros((C, LANES), F32), v_s[rz, :]], axis=1)], axis=0)
                mn = _bdot(ends.T, rhs)
                m_s[z, cs[u][z]] = jnp.where(eye, jnp.broadcast_to(pt, (LANES, LANES)), 0.0) \
                    + jnp.where(same_head, mn[:, :LANES], 0.0)
                n_s[z, cs[u][z]] = jnp.where(same_head, mn[:, LANES:], 0.0)

    def carry_step(g, u, hs):
        for z in range(2):
            c = chunk_of(g, u, z)
            rows = rows_of(c)
            y_s[z, rows, :] = y_s[z, rows, :] + dot(rt_s[z, rows, :], hs[z])
            hs[z] = (dot(m_s[z, c], hs[z]) + n_s[z, c]).astype(BF16)

    def phase2(g, hs):
        hs = list(hs)
        for u in range(U):
            carry_step(g, u, hs)
        return tuple(hs)

    def lagged(g, hs):
        hs = list(hs)
        steps_per_gap = -(-U // 8)

        def between(k):
            for u in range(k * steps_per_gap, min(U, (k + 1) * steps_per_gap)):
                carry_step(g - 1, u, hs)

        phase1(g, between)
        return tuple(hs)

    ngroup = nchunk // U
    h0 = jnp.zeros((LANES, LANES), BF16)
    phase1(0)
    hs = lax.fori_loop(1, ngroup, lagged, (h0, h0))
    phase2(ngroup - 1, hs)

    y = y_s[0] + y_s[1]
    inv_n = 1.0 / HEAD_DIM
    mu_y = _split_dot(y, seg_ones) * inv_n
    yc = y - mu_y
    var_y = _split_dot(yc * yc, seg_ones) * inv_n
    yn = yc * lax.rsqrt(var_y + RWKV_GN_EPS) * gng_ref[...] + gnb_ref[...]
    bonus = _split_dot(r * kd_sum * rk_ref[...], seg_ones) * v
    o_ref[0] = ((yn + bonus) * g).astype(o_ref.dtype)


def _rwkv_branch(p, w0, w_up, a0, a_up, g_up, k_k, k_a, r_k, gn_g, gn_b, out_dtype):
    B, S, _ = p.shape
    npair = RWKV_DIM // LANES
    zpad = jnp.zeros((2, LANES - DECAY_LORA, RWKV_DIM), F32)
    wup_p = jnp.concatenate([w_up, zpad], axis=1).astype(BF16)
    aup_p = jnp.concatenate([zpad, a_up], axis=1).astype(BF16)
    row = lambda a: a.reshape(1, RWKV_DIM)
    lora_blk = 3 * RWKV_DIM // (2 * LANES)

    def pspec(off):
        return pl.BlockSpec((1, S, LANES), lambda b, j: (b, 0, off + j))

    def vspec(rows, off=0):
        return pl.BlockSpec((rows, LANES), lambda b, j: (0, off + j))

    scratch = [pltpu.VMEM((2, S, LANES), F32) for _ in range(5)] + [
        pltpu.VMEM((S, LANES), F32), pltpu.VMEM((2, S, LANES), F32),
        pltpu.VMEM((2, S // RWKV_CHUNK, LANES, LANES), F32),
        pltpu.VMEM((2, S // RWKV_CHUNK, LANES, LANES), F32)]
    return pl.pallas_call(
        functools.partial(_rwkv_kernel, seq=S),
        grid=(B, npair),
        in_specs=[pspec(0), pspec(npair), pspec(2 * npair),
                  pl.BlockSpec((1, S, 2 * LANES), lambda b, j: (b, 0, lora_blk)),
                  vspec(2),
                  pl.BlockSpec((2, LANES, LANES), lambda b, j: (0, 0, j)),
                  vspec(2),
                  pl.BlockSpec((2, LANES, LANES), lambda b, j: (0, 0, j)),
                  vspec(GATE_LORA), vspec(1), vspec(1), vspec(1), vspec(1), vspec(1)],
        out_specs=pl.BlockSpec((1, S, LANES), lambda b, j: (b, 0, j)),
        out_shape=jax.ShapeDtypeStruct((B, S, RWKV_DIM), out_dtype),
        scratch_shapes=scratch,
        compiler_params=_cparams("arbitrary", "arbitrary"),
        name="rwkv7_scan",
    )(p, p, p, p, w0, wup_p, a0, aup_p, g_up.astype(BF16),
      row(k_k), row(k_a), row(r_k), row(gn_g), row(gn_b))


def _win_kernel(q_ref, k_ref, v_ref, biast_ref, sink_ref, o_ref, k4_s, vt_s, *, seq):
    S = seq
    G = WIN_GROUP
    NB = S // BLOCK
    GW = G * HEAD_DIM
    hk = pl.program_id(1)
    odd = (hk % 2) == 1
    lane = lax.broadcasted_iota(jnp.int32, (1, LANES), 1)

    def own_head_twice(full):
        mine_first = jnp.where(odd, pltpu.roll(full, HEAD_DIM, 1), full)
        return jnp.where(lane < HEAD_DIM, mine_first, pltpu.roll(mine_first, HEAD_DIM, 1))

    k2 = own_head_twice(k_ref[0].astype(F32)).astype(BF16)
    v2 = own_head_twice(v_ref[0].astype(F32))
    zero_k = jnp.zeros((BLOCK, GW), BF16)
    zero_v = jnp.zeros((HEAD_DIM, BLOCK), BF16)
    k4_s[0] = zero_k
    k4_s[NB + 1] = zero_k
    vt_s[0] = zero_v
    vt_s[NB + 1] = zero_v
    for blk in range(NB):
        rows = slice(blk * BLOCK, (blk + 1) * BLOCK)
        k4_s[blk + 1] = jnp.concatenate([k2[rows], k2[rows]], axis=1)
        vt_s[blk + 1] = v2[rows].T[:HEAD_DIM].astype(BF16)
    log2e = math.log2(math.e)
    bias_t = jnp.concatenate([biast_ref[gi] for gi in range(G)], axis=1) * log2e
    krow = lax.broadcasted_iota(jnp.int32, (3 * BLOCK, G * BLOCK), 0)
    qcol = lax.broadcasted_iota(jnp.int32, (3 * BLOCK, G * BLOCK), 1) % BLOCK
    bias_t = jnp.where(jnp.abs(krow - BLOCK - qcol) <= WINDOW, bias_t, NEG_INF)
    cgrp = lax.broadcasted_iota(jnp.int32, (1, G * BLOCK), 1) // BLOCK
    sink_row = jnp.zeros((1, G * BLOCK), F32)
    for gi in range(G):
        sink_row = jnp.where(cgrp == gi, sink_ref[hk * G + gi] * log2e, sink_row)
    lane_q = lax.broadcasted_iota(jnp.int32, (1, GW), 1) // HEAD_DIM
    U = min(WIN_UNROLL, NB)

    def body(it, carry):
        ns = [it * U + u for u in range(U)]
        sts = []
        for n in ns:
            qb = q_ref[0, pl.ds(pl.multiple_of(n * BLOCK, BLOCK), BLOCK), :].astype(F32) \
                * (log2e * HEAD_DIM ** -0.5)
            qm = jnp.concatenate([jnp.where(lane_q == gi, qb, 0.0) for gi in range(G)], axis=0).astype(BF16)
            kband = jnp.concatenate([k4_s[n], k4_s[n + 1], k4_s[n + 2]], axis=0)
            sts.append(lax.dot_general(kband, qm, (((1,), (1,)), ((), ())), preferred_element_type=F32))
        es, dens = [], []
        for u, n in enumerate(ns):
            st = sts[u] + bias_t
            st = jnp.concatenate([st[:BLOCK] + jnp.where(n == 0, NEG_INF, 0.0), st[BLOCK:2 * BLOCK],
                                  st[2 * BLOCK:] + jnp.where(n == NB - 1, NEG_INF, 0.0)], axis=0)
            m = jnp.maximum(jnp.max(st, axis=0, keepdims=True), sink_row)
            e = jnp.exp2(st - m)
            es.append(e.astype(BF16))
            dens.append(jnp.sum(e, axis=0, keepdims=True) + jnp.exp2(sink_row - m))
        for u, n in enumerate(ns):
            vband = jnp.concatenate([vt_s[n], vt_s[n + 1], vt_s[n + 2]], axis=1)
            acc = jnp.dot(vband, es[u], preferred_element_type=F32) / dens[u]
            o_t = jnp.concatenate([acc[:, gi * BLOCK:(gi + 1) * BLOCK] for gi in range(G)], axis=0)
            o_ref[0, pl.ds(pl.multiple_of(n * BLOCK, BLOCK), BLOCK), :] = o_t.T.astype(o_ref.dtype)
        return carry

    lax.fori_loop(0, NB // U, body, 0)


def _window_branch(p, bias_tiles_t, sink, out_dtype):
    B, S, _ = p.shape
    gw = WIN_GROUP * HEAD_DIM
    nb = S // BLOCK
    assert nb % min(WIN_UNROLL, nb) == 0
    qoff = OFF_WIN // gw
    koff = (OFF_WIN + WIN_Q_COLS) // LANES
    voff = (OFF_WIN + WIN_Q_COLS + WIN_KV_COLS) // LANES
    return pl.pallas_call(
        functools.partial(_win_kernel, seq=S),
        grid=(B, WIN_KV_HEADS),
        in_specs=[pl.BlockSpec((1, S, gw), lambda b, h: (b, 0, qoff + h)),
                  pl.BlockSpec((1, S, LANES), lambda b, h: (b, 0, koff + h // 2)),
                  pl.BlockSpec((1, S, LANES), lambda b, h: (b, 0, voff + h // 2)),
                  pl.BlockSpec((WIN_GROUP, 3 * BLOCK, BLOCK), lambda b, h: (h, 0, 0)),
                  pl.BlockSpec(memory_space=pltpu.SMEM)],
        out_specs=pl.BlockSpec((1, S, gw), lambda b, h: (b, 0, h)),
        out_shape=jax.ShapeDtypeStruct((B, S, WIN_Q_COLS), out_dtype),
        scratch_shapes=[pltpu.VMEM((nb + 2, BLOCK, gw), BF16),
                        pltpu.VMEM((nb + 2, HEAD_DIM, BLOCK), BF16)],
        compiler_params=_cparams("arbitrary", "arbitrary"),
        name="window_attn",
    )(p, p, p, bias_tiles_t, sink)


def _diff_kernel(q_ref, k_ref, v_ref, bandt_ref, tab_ref, lam_ref, g_ref, o_ref, ka_s, vt_s, *,
                 seq, lambda_init):
    S = seq
    NB = S // BLOCK
    CB = min(DIFF_CHUNK_BLOCKS, NB)
    h = pl.program_id(1)
    it = pl.program_id(2)
    halves = q_ref.shape[1] // BLOCK
    log2e = math.log2(math.e)
    lane = lax.broadcasted_iota(jnp.int32, (1, LANES), 1)
    c0 = lane < HEAD_DIM

    @pl.when(it == 0)
    def _():
        jblk = lax.broadcasted_iota(jnp.int32, (S, LANES), 0) // BLOCK
        lj = lax.broadcasted_iota(jnp.int32, (S, LANES), 1)
        ka_s[:, :LANES] = k_ref[0].astype(BF16)
        ka_s[:, LANES:] = jnp.where((lj < 2 * NB) & (lj % NB == jblk), 1.0, 0.0).astype(BF16)
        for blk in range(NB):
            vt_s[blk] = v_ref[0, blk * BLOCK:(blk + 1) * BLOCK, :].astype(F32).T.astype(BF16)

    far_l = tab_ref[WIN_Q_HEADS + h, NUM_BUCKETS // 2 - 1] * log2e
    far_r = tab_ref[WIN_Q_HEADS + h, NUM_BUCKETS - 1] * log2e
    band_t = bandt_ref[0] * log2e
    lam = lam_ref[...].astype(F32)
    lam_full = (jnp.exp(jnp.sum(lam[0:1] * lam[1:2], axis=-1, keepdims=True))
                - jnp.exp(jnp.sum(lam[2:3] * lam[3:4], axis=-1, keepdims=True)) + lambda_init)
    ibs = [it * halves + hf for hf in range(halves)]
    zblk = jnp.zeros((BLOCK, BLOCK), F32)
    qq, bands = [], []
    for hf in range(halves):
        q = q_ref[0, hf * BLOCK:(hf + 1) * BLOCK, :].astype(F32) * (log2e * HEAD_DIM ** -0.5)
        kb = lane % NB
        val = jnp.where(kb < ibs[hf] - 1, far_l, jnp.where(kb > ibs[hf] + 1, far_r, 0.0))
        val = jnp.where(lane < 2 * NB, val, 0.0)
        hi = val.astype(BF16)
        lo = (val - hi.astype(F32)).astype(BF16)
        bcols = jnp.broadcast_to(jnp.where(lane < NB, hi, lo), (BLOCK, LANES))
        qq.append(jnp.concatenate(
            [jnp.concatenate([jnp.where(c0, q, 0.0).astype(BF16), bcols], axis=1),
             jnp.concatenate([jnp.where(c0, 0.0, q).astype(BF16), bcols], axis=1)], axis=0))
        tiles = [jnp.where(ibs[hf] > 0, band_t[:BLOCK], 0.0), band_t[BLOCK:2 * BLOCK],
                 jnp.where(ibs[hf] < NB - 1, band_t[2 * BLOCK:], 0.0)]
        col = jnp.concatenate([zblk] * hf + tiles + [zblk] * (CB - 3 - hf), axis=0)
        bands.append(jnp.concatenate([col, col], axis=1))

    m_run, l_run, acc = [None] * halves, [None] * halves, [None] * halves
    for r in range(NB // CB):
        kbs = [lax.rem(it * halves - 1 + CB * r + u + NB, NB) for u in range(CB)]
        kc = jnp.concatenate([ka_s[pl.ds(pl.multiple_of(kb * BLOCK, BLOCK), BLOCK), :] for kb in kbs], axis=0)
        vct = jnp.concatenate([vt_s[kb] for kb in kbs], axis=1)
        s = [lax.dot_general(kc, qq[hf], (((1,), (1,)), ((), ())), preferred_element_type=F32)
             for hf in range(halves)]
        if r == 0:
            s = [s[hf] + bands[hf] for hf in range(halves)]
        m_loc = [jnp.max(s[hf], axis=0, keepdims=True) for hf in range(halves)]
        for hf in range(halves):
            if r == 0:
                m_new = m_loc[hf]
                e = jnp.exp2(s[hf] - m_new)
                l_run[hf] = jnp.sum(e, axis=0, keepdims=True)
                acc[hf] = jnp.dot(vct, e.astype(BF16), preferred_element_type=F32)
            else:
                m_new = jnp.maximum(m_run[hf], m_loc[hf])
                alpha = jnp.exp2(m_run[hf] - m_new)
                e = jnp.exp2(s[hf] - m_new)
                l_run[hf] = alpha * l_run[hf] + jnp.sum(e, axis=0, keepdims=True)
                acc[hf] = alpha * acc[hf] + jnp.dot(vct, e.astype(BF16), preferred_element_type=F32)
            m_run[hf] = m_new
    for hf in range(halves):
        rinv = 1.0 / l_run[hf]
        o_t = acc[hf][:, :BLOCK] * rinv[:, :BLOCK] - acc[hf][:, BLOCK:] * (lam_full * rinv[:, BLOCK:])
        o = o_t.T
        o = o * lax.rsqrt(jnp.mean(o * o, axis=-1, keepdims=True) + 1e-5) * g_ref[...] * (1.0 - lambda_init)
        o_ref[0, hf * BLOCK:(hf + 1) * BLOCK, :] = o.astype(o_ref.dtype)


def _diff_branch(p, band_tiles, rel_bias_t, lam, subln_g, lambda_init, out_dtype):
    B, S, _ = p.shape
    qoff = OFF_DIFF // LANES
    koff = (OFF_DIFF + DIFF_QK_COLS) // LANES
    voff = (OFF_DIFF + 2 * DIFF_QK_COLS) // LANES
    nb = S // BLOCK
    cb = min(DIFF_CHUNK_BLOCKS, nb)
    tq = DIFF_Q_TILE
    while tq > (cb - 2) * BLOCK or S % tq:
        tq //= 2
    assert 2 * nb <= LANES and tq >= BLOCK and nb % cb == 0 and nb >= 4
    return pl.pallas_call(
        functools.partial(_diff_kernel, seq=S, lambda_init=lambda_init),
        grid=(B, DIFF_HEADS, S // tq),
        in_specs=[pl.BlockSpec((1, tq, LANES), lambda b, h, i: (b, i, qoff + h)),
                  pl.BlockSpec((1, S, LANES), lambda b, h, i: (b, 0, koff + h)),
                  pl.BlockSpec((1, S, LANES), lambda b, h, i: (b, 0, voff + h)),
                  pl.BlockSpec((1, 3 * BLOCK, BLOCK), lambda b, h, i: (WIN_Q_HEADS + h, 0, 0)),
                  pl.BlockSpec(memory_space=pltpu.SMEM),
                  pl.BlockSpec((4, HEAD_DIM), lambda b, h, i: (0, 0)),
                  pl.BlockSpec((1, DIFF_V_DIM), lambda b, h, i: (0, 0))],
        out_specs=pl.BlockSpec((1, tq, LANES), lambda b, h, i: (b, i, h)),
        out_shape=jax.ShapeDtypeStruct((B, S, DIFF_V_COLS), out_dtype),
        scratch_shapes=[pltpu.VMEM((S, 2 * LANES), BF16),
                        pltpu.VMEM((S // BLOCK, DIFF_V_DIM, BLOCK), BF16)],
        compiler_params=_cparams("arbitrary", "arbitrary", "arbitrary"),
        name="diff_attn",
    )(p, p, p, jnp.swapaxes(band_tiles, 1, 2), rel_bias_t, lam, subln_g.reshape(1, DIFF_V_DIM))


def _mem_kernel(q_ref, mem_ref, wkv_ref, o_ref):
    kv = _bdot(mem_ref[0], wkv_ref[...])
    kmat = kv[:, :MEM_WIDTH]
    vmat = kv[:, MEM_WIDTH:]
    q = q_ref[0].astype(F32) * (HEAD_DIM ** -0.5)
    lane = lax.broadcasted_iota(jnp.int32, (1, MEM_WIDTH), 1) // HEAD_DIM
    out = jnp.zeros(q.shape, F32)
    for hh in range(MEM_HEADS):
        s = _bdot_nt(jnp.where(lane == hh, q, 0.0), kmat)
        m = jnp.max(s, axis=-1, keepdims=True)
        e = jnp.exp(s - m)
        a = e / jnp.sum(e, axis=-1, keepdims=True)
        out = jnp.where(lane == hh, _bdot(a, vmat), out)
    o_ref[0] = out.astype(o_ref.dtype)


def _memory_branch(p, mem, w_kv, out_dtype):
    B, S, _ = p.shape
    nm, d = mem.shape[1], mem.shape[2]
    return pl.pallas_call(
        _mem_kernel,
        grid=(B,),
        in_specs=[pl.BlockSpec((1, S, MEM_WIDTH), lambda b: (b, 0, OFF_MEM // MEM_WIDTH)),
                  pl.BlockSpec((1, nm, d), lambda b: (b, 0, 0)),
                  pl.BlockSpec((d, 2 * MEM_WIDTH), lambda b: (0, 0))],
        out_specs=pl.BlockSpec((1, S, MEM_WIDTH), lambda b: (b, 0, 0)),
        out_shape=jax.ShapeDtypeStruct((B, S, MEM_WIDTH), out_dtype),
        compiler_params=_cparams("arbitrary"),
        name="memory_attn",
    )(p, mem, w_kv)


def _merge_kernel(x_ref, yr_ref, yw_ref, yd_ref, ym_ref, pg_ref, wb_ref, wo_ref, g_ref, b_ref,
                  o_ref, *, d_model, alpha):
    D = d_model
    ys = (yr_ref[...], yw_ref[...], yd_ref[...], ym_ref[...])
    off = 0
    merged = jnp.zeros((x_ref.shape[0], D), F32)
    for bi, yb in enumerate(ys):
        rows = yb.shape[1]
        proj = _bdot(yb, wb_ref[off:off + rows, :])
        merged = merged + _sigmoid(pg_ref[:, bi * D:(bi + 1) * D].astype(F32)) * proj
        off += rows
    z = alpha * x_ref[...] + _bdot(merged, wo_ref[...])
    o_ref[...] = _layer_norm(z, g_ref[...], b_ref[...])


def _merge(x2, y_rwkv, y_win, y_diff, y_mem, p2, w_branch, w_out, ln_g, ln_b, alpha, tm):
    T, D = x2.shape
    gate_blk = OFF_GATE // (N_BRANCHES * D)
    full = lambda a: pl.BlockSpec(a.shape, lambda i: (0, 0))
    tile = lambda w: pl.BlockSpec((tm, w), lambda i: (i, 0))
    return pl.pallas_call(
        functools.partial(_merge_kernel, d_model=D, alpha=alpha),
        grid=(T // tm,),
        in_specs=[tile(D), tile(y_rwkv.shape[1]), tile(y_win.shape[1]), tile(y_diff.shape[1]),
                  tile(y_mem.shape[1]),
                  pl.BlockSpec((tm, N_BRANCHES * D), lambda i: (i, gate_blk)),
                  full(w_branch), full(w_out),
                  pl.BlockSpec((1, D), lambda i: (0, 0)), pl.BlockSpec((1, D), lambda i: (0, 0))],
        out_specs=tile(D),
        out_shape=jax.ShapeDtypeStruct((T, D), F32),
        compiler_params=_cparams("arbitrary"),
        name="merge_out_ln",
    )(x2, y_rwkv, y_win, y_diff, y_mem, p2, w_branch, w_out, ln_g.reshape(1, D), ln_b.reshape(1, D))


def _route_kernel(x_ref, wr_ref, xe_ref, slot_ref, gate_ref, xb_s, lg_s, *, seq, cap):
    S = seq
    E = N_EXPERTS
    TR = min(S, 512)
    wr = wr_ref[...]
    w_hi = wr.astype(BF16)
    w_md = (wr - w_hi.astype(F32)).astype(BF16)
    dot = lambda a, b: jnp.dot(a, b, preferred_element_type=F32)

    def logit_tile(t, carry):
        rows = pl.ds(pl.multiple_of(t * TR, TR), TR)
        x = x_ref[0, rows, :]
        x_hi = x.astype(BF16)
        x_r1 = x - x_hi.astype(F32)
        x_md = x_r1.astype(BF16)
        x_lo = (x_r1 - x_md.astype(F32)).astype(BF16)
        xb_s[rows, :] = x_hi
        lg_s[rows, :] = (dot(x_hi, w_hi) + (dot(x_hi, w_md) + dot(x_md, w_hi))
                         + (dot(x_md, w_md) + dot(x_lo, w_hi)))
        return carry

    lax.fori_loop(0, S // TR, logit_tile, 0)
    col = lax.broadcasted_iota(jnp.int32, (1, LANES), 1)
    logits = jnp.where(col < E, lg_s[...], NEG_INF)
    mx = jnp.max(logits, axis=-1, keepdims=True)
    ex = jnp.exp(logits - mx)
    aff = ex / jnp.sum(ex, axis=-1, keepdims=True)
    aff_t = aff.T[0:E, :]
    bits = pltpu.bitcast(aff_t, jnp.int32)

    def bis(_, carry):
        lo_b, step = carry
        cand = lo_b + step
        cnt = jnp.sum(jnp.where(bits >= cand, 1.0, 0.0), axis=-1, keepdims=True)
        return jnp.where(cnt >= cap, cand, lo_b), step // 2

    lo0 = jnp.zeros((E, 1), jnp.int32)
    thr, _ = lax.fori_loop(0, 31, bis, (lo0, jnp.full((E, 1), 1 << 30, jnp.int32)))
    above = bits > thr
    tie = bits == thr
    n_above = jnp.sum(jnp.where(above, 1.0, 0.0), axis=-1, keepdims=True)
    PB = min(S, 256)
    tri = (lax.broadcasted_iota(jnp.int32, (PB, PB), 0)
           < lax.broadcasted_iota(jnp.int32, (PB, PB), 1)).astype(BF16)

    def prefix_count(mask_f):
        parts, run = [], jnp.zeros((E, 1), F32)
        for blk in range(S // PB):
            seg = mask_f[:, blk * PB:(blk + 1) * PB]
            parts.append(jnp.dot(seg.astype(BF16), tri, preferred_element_type=F32) + run)
            run = run + jnp.sum(seg, axis=-1, keepdims=True)
        return jnp.concatenate(parts, axis=1)

    tie_rank = prefix_count(jnp.where(tie, 1.0, 0.0))
    sel = above | (tie & (tie_rank < cap - n_above))
    pos = prefix_count(jnp.where(sel, 1.0, 0.0))
    slot = jnp.where(sel, pos, -1.0)
    slot_pad = jnp.concatenate([slot, jnp.full((LANES - E, S), -1.0, F32)], axis=0)
    slot_ref[0] = slot_pad.T
    ci = lax.broadcasted_iota(jnp.int32, (cap, S), 0).astype(F32)
    x_hi = xb_s[...]
    for e in range(E):
        onehot = slot[e:e + 1, :] == ci
        xe_ref[e, 0] = jnp.dot(jnp.where(onehot, 1.0, 0.0).astype(BF16), x_hi,
                               preferred_element_type=F32).astype(xe_ref.dtype)
        gsel = jnp.sum(jnp.where(onehot, aff_t[e:e + 1, :], 0.0), axis=-1, keepdims=True)
        gate_ref[e, 0] = jnp.broadcast_to(gsel, (cap, LANES))


def _route(x3, router_pad, cap):
    B, S, D = x3.shape
    E = N_EXPERTS
    return pl.pallas_call(
        functools.partial(_route_kernel, seq=S, cap=cap),
        grid=(B,),
        in_specs=[pl.BlockSpec((1, S, D), lambda b: (b, 0, 0)),
                  pl.BlockSpec((D, LANES), lambda b: (0, 0))],
        out_specs=[pl.BlockSpec((E, 1, cap, D), lambda b: (0, b, 0, 0)),
                   pl.BlockSpec((1, S, LANES), lambda b: (b, 0, 0)),
                   pl.BlockSpec((E, 1, cap, LANES), lambda b: (0, b, 0, 0))],
        out_shape=[jax.ShapeDtypeStruct((E, B, cap, D), BF16),
                   jax.ShapeDtypeStruct((B, S, LANES), F32),
                   jax.ShapeDtypeStruct((E, B, cap, LANES), F32)],
        scratch_shapes=[pltpu.VMEM((S, D), BF16), pltpu.VMEM((S, LANES), F32)],
        compiler_params=_cparams("arbitrary"),
        name="moe_route_gather",
    )(x3, router_pad)


def _ffn_kernel(xe_ref, g_ref, wg_ref, wu_ref, wd_ref, ye_ref, acc_s):
    f = pl.program_id(1)
    nb, cap, D = xe_ref.shape[1], xe_ref.shape[2], xe_ref.shape[3]

    @pl.when(f == 0)
    def _():
        acc_s[...] = jnp.zeros(acc_s.shape, F32)

    xe = xe_ref[0].reshape(nb * cap, D)
    hg = jnp.dot(xe, wg_ref[0, 0].astype(BF16), preferred_element_type=F32)
    hu = jnp.dot(xe, wu_ref[0, 0].astype(BF16), preferred_element_type=F32)
    hid = (hg * _sigmoid(hg)) * hu
    acc_s[...] = acc_s[...] + _bdot(hid, wd_ref[0, 0])

    @pl.when(f == pl.num_programs(1) - 1)
    def _():
        gate = g_ref[0].reshape(nb * cap, LANES)[:, 0:1]
        ye_ref[0] = (acc_s[...] * gate).reshape(nb, cap, D).astype(ye_ref.dtype)


def _expert_ffn(xe, gate, w_gate, w_up, w_down, layer, tf):
    E, B, cap, D = xe.shape
    F = w_gate.shape[-1]
    return pl.pallas_call(
        _ffn_kernel,
        grid=(E, F // tf),
        in_specs=[pl.BlockSpec((1, B, cap, D), lambda e, f: (e, 0, 0, 0)),
                  pl.BlockSpec((1, B, cap, LANES), lambda e, f: (e, 0, 0, 0)),
                  pl.BlockSpec((1, 1, D, tf), lambda e, f: (layer, e, 0, f)),
                  pl.BlockSpec((1, 1, D, tf), lambda e, f: (layer, e, 0, f)),
                  pl.BlockSpec((1, 1, tf, D), lambda e, f: (layer, e, f, 0))],
        out_specs=pl.BlockSpec((1, B, cap, D), lambda e, f: (e, 0, 0, 0)),
        out_shape=jax.ShapeDtypeStruct((E, B, cap, D), BF16),
        scratch_shapes=[pltpu.VMEM((B * cap, D), F32)],
        compiler_params=_cparams("arbitrary", "arbitrary"),
        name="moe_expert_ffn",
    )(xe, gate, w_gate, w_up, w_down)


def _scatter_kernel(x_ref, slot_ref, ye_ref, g_ref, b_ref, o_ref, ob_ref, *, cap, alpha):
    ts = x_ref.shape[1]
    slot = slot_ref[0]
    ci = lax.broadcasted_iota(jnp.int32, (ts, cap), 1).astype(F32)
    acc = jnp.zeros((ts, x_ref.shape[2]), F32)
    for e in range(N_EXPERTS):
        onehot = slot[:, e:e + 1] == ci
        acc = acc + jnp.dot(jnp.where(onehot, 1.0, 0.0).astype(BF16), ye_ref[e, 0],
                            preferred_element_type=F32)
    out = _layer_norm(alpha * x_ref[0] + acc, g_ref[...], b_ref[...])
    o_ref[0] = out
    ob_ref[0] = out.astype(ob_ref.dtype)


def _scatter_ln(x3, slot, ye, ln_g, ln_b, alpha, ts):
    B, S, D = x3.shape
    E, _, cap, _ = ye.shape
    return pl.pallas_call(
        functools.partial(_scatter_kernel, cap=cap, alpha=alpha),
        grid=(B, S // ts),
        in_specs=[pl.BlockSpec((1, ts, D), lambda b, i: (b, i, 0)),
                  pl.BlockSpec((1, ts, LANES), lambda b, i: (b, i, 0)),
                  pl.BlockSpec((E, 1, cap, D), lambda b, i: (0, b, 0, 0)),
                  pl.BlockSpec((1, D), lambda b, i: (0, 0)),
                  pl.BlockSpec((1, D), lambda b, i: (0, 0))],
        out_specs=[pl.BlockSpec((1, ts, D), lambda b, i: (b, i, 0)),
                   pl.BlockSpec((1, ts, D), lambda b, i: (b, i, 0))],
        out_shape=[jax.ShapeDtypeStruct((B, S, D), F32), jax.ShapeDtypeStruct((B, S, D), BF16)],
        compiler_params=_cparams("arbitrary", "arbitrary"),
        name="moe_scatter_ln",
    )(x3, slot, ye, ln_g.reshape(1, D), ln_b.reshape(1, D))


def kernel(x, mem, rel_bias, w_in, rwkv_mu, rwkv_w0, rwkv_w_up, rwkv_a0, rwkv_a_up, rwkv_g_up, rwkv_k_k, rwkv_k_a, rwkv_r_k, rwkv_gn_g, rwkv_gn_b, win_sink, diff_lambda, diff_subln_g, mem_w_kv, w_branch, w_out, ln1_g, ln1_b, router, exp_w_gate, exp_w_up, exp_w_down, ln2_g, ln2_b):
    B, S, D = x.shape
    assert D == D_MODEL and w_in.shape[2] == RWKV_COLS + REST_COLS
    depth = w_in.shape[0]
    alpha = (2 * depth) ** 0.25
    cap = CAPACITY_FACTOR * S // N_EXPERTS
    T = B * S
    rel_bias_t = rel_bias.T
    bias_tiles = _bias_tiles(rel_bias_t)
    bias_tiles_t = jnp.swapaxes(bias_tiles, 1, 2)
    act = BF16
    xb = x.astype(BF16)
    for l in range(depth):
        lambda_init = 0.8 - 0.6 * math.exp(-0.3 * l)
        gate0 = w_in.shape[2] - N_BRANCHES * D
        w_rest = jnp.concatenate([w_in[l][:, gate0:], w_in[l][:, RWKV_COLS:gate0]], axis=1).astype(BF16)
        p_rwkv, p2 = _in_proj(xb.reshape(T, D), w_in[l][:, :RWKV_COLS].astype(BF16), w_rest,
                              rwkv_mu[l], S, act)
        p = p2.reshape(B, S, -1)
        y_rwkv = _rwkv_branch(p_rwkv.reshape(B, S, -1), rwkv_w0[l], rwkv_w_up[l], rwkv_a0[l], rwkv_a_up[l],
                              rwkv_g_up[l], rwkv_k_k[l], rwkv_k_a[l], rwkv_r_k[l], rwkv_gn_g[l],
                              rwkv_gn_b[l], act)
        y_win = _window_branch(p, bias_tiles_t, win_sink[l], act)
        y_diff = _diff_branch(p, bias_tiles, rel_bias_t, diff_lambda[l], diff_subln_g[l], lambda_init, act)
        y_mem = _memory_branch(p, mem, mem_w_kv[l].astype(BF16), act)
        x1 = _merge(x.reshape(T, D), y_rwkv.reshape(T, -1), y_win.reshape(T, -1), y_diff.reshape(T, -1),
                    y_mem.reshape(T, -1), p2, w_branch[l].astype(BF16), w_out[l].astype(BF16),
                    ln1_g[l], ln1_b[l], alpha, 512).reshape(B, S, D)
        router_pad = jnp.concatenate([router[l], jnp.zeros((D, LANES - N_EXPERTS), F32)], axis=1)
        xe, slot, gate = _route(x1, router_pad, cap)
        ye = _expert_ffn(xe, gate, exp_w_gate, exp_w_up, exp_w_down, l, 512)
        x, xb = _scatter_ln(x1, slot, ye, ln2_g[l], ln2_b[l], alpha, 512)
    return x
```

```python
import functools
import math

import jax
import jax.numpy as jnp
from jax import lax
from jax.experimental import pallas as pl
from jax.experimental.pallas import tpu as pltpu

F32 = jnp.float32
BF16 = jnp.bfloat16

HEAD_DIM = 64
LANES = 128
RWKV_HEADS = 16
RWKV_DIM = RWKV_HEADS * HEAD_DIM
DECAY_LORA = 64
ICLR_LORA = 64
GATE_LORA = 128
RWKV_GN_EPS = 64e-5
RWKV_COLS = 3 * RWKV_DIM + DECAY_LORA + ICLR_LORA + GATE_LORA
RWKV_CHUNK = 64
RWKV_UNROLL = 8
assert RWKV_CHUNK == HEAD_DIM
WIN_Q_HEADS = 16
WIN_KV_HEADS = 4
WIN_GROUP = WIN_Q_HEADS // WIN_KV_HEADS
WIN_Q_COLS = WIN_Q_HEADS * HEAD_DIM
WIN_KV_COLS = WIN_KV_HEADS * HEAD_DIM
WIN_COLS = WIN_Q_COLS + 2 * WIN_KV_COLS
WINDOW = 128
BLOCK = 128
WIN_UNROLL = 16
DIFF_HEADS = 8
DIFF_V_DIM = 2 * HEAD_DIM
DIFF_QK_COLS = 2 * DIFF_HEADS * HEAD_DIM
DIFF_V_COLS = DIFF_HEADS * DIFF_V_DIM
DIFF_COLS = 2 * DIFF_QK_COLS + DIFF_V_COLS
DIFF_Q_TILE = 8 * BLOCK
DIFF_CHUNK_BLOCKS = 16
MEM_HEADS = 4
MEM_WIDTH = MEM_HEADS * HEAD_DIM
NUM_BUCKETS = 32
N_EXPERTS = 16
CAPACITY_FACTOR = 2
N_BRANCHES = 4
NEG_INF = -1e30
LN_EPS = 1e-5
VMEM_LIMIT = 56 * 1024 * 1024
D_MODEL = 1024
IN_PROJ_TN_SHIFT = 256
IN_PROJ_TN = 1280

OFF_GATE = 0
OFF_WIN = N_BRANCHES * D_MODEL
OFF_DIFF = OFF_WIN + WIN_COLS
OFF_MEM = OFF_DIFF + DIFF_COLS
REST_COLS = OFF_MEM + MEM_WIDTH


def _cparams(*sem):
    return pltpu.CompilerParams(dimension_semantics=sem, vmem_limit_bytes=VMEM_LIMIT)


def _bdot(a, b):
    return jnp.dot(a.astype(BF16), b.astype(BF16), preferred_element_type=F32)


def _bdot_nt(a, b):
    return lax.dot_general(a.astype(BF16), b.astype(BF16), (((1,), (1,)), ((), ())),
                           preferred_element_type=F32)


def _split_dot(a, b_exact):
    hi = a.astype(BF16)
    lo = (a - hi.astype(F32)).astype(BF16)
    return (jnp.dot(hi, b_exact, preferred_element_type=F32)
            + jnp.dot(lo, b_exact, preferred_element_type=F32))


def _sigmoid(x):
    return 0.5 * jnp.tanh(0.5 * x) + 0.5


def _layer_norm(x, g, b):
    mu = jnp.mean(x, axis=-1, keepdims=True)
    xc = x - mu
    var = jnp.mean(xc * xc, axis=-1, keepdims=True)
    return xc * lax.rsqrt(var + LN_EPS) * g + b


def _inproj_shift_kernel(x_ref, w_ref, mu_ref, o_ref):
    acc = jnp.dot(x_ref[...], w_ref[...], preferred_element_type=F32)
    S = acc.shape[0]
    row = lax.broadcasted_iota(jnp.int32, (S, 1), 0)
    prev = jnp.where(row == 0, 0.0, pltpu.roll(acc, 1, 0))
    nxt = jnp.where(row == S - 1, 0.0, pltpu.roll(acc, S - 1, 0))
    out = acc + mu_ref[0:1, :] * (prev - acc) + mu_ref[1:2, :] * (nxt - acc)
    o_ref[...] = out.astype(o_ref.dtype)


def _inproj_plain_kernel(x_ref, w_ref, o_ref):
    o_ref[...] = jnp.dot(x_ref[...], w_ref[...], preferred_element_type=F32).astype(o_ref.dtype)


def _in_proj(x2, w_rwkv, w_rest, mu, seq, out_dtype):
    m, k = x2.shape
    x_spec = pl.BlockSpec((seq, k), lambda i, j: (i, 0))

    def call(kern, w, tn, extra, extra_specs, name):
        n = w.shape[1]
        assert n % tn == 0
        return pl.pallas_call(
            kern,
            grid=(m // seq, n // tn),
            in_specs=[x_spec, pl.BlockSpec((k, tn), lambda i, j: (0, j))] + extra_specs,
            out_specs=pl.BlockSpec((seq, tn), lambda i, j: (i, j)),
            out_shape=jax.ShapeDtypeStruct((m, n), out_dtype),
            compiler_params=_cparams("arbitrary", "arbitrary"),
            name=name,
        )(x2, w, *extra)

    pad = -w_rwkv.shape[1] % IN_PROJ_TN_SHIFT
    w_pad = jnp.concatenate([w_rwkv, jnp.zeros((k, pad), w_rwkv.dtype)], axis=1)
    mu_pad = jnp.concatenate([mu, jnp.zeros((2, pad), mu.dtype)], axis=1)
    p_rwkv = call(_inproj_shift_kernel, w_pad, IN_PROJ_TN_SHIFT, [mu_pad],
                  [pl.BlockSpec((2, IN_PROJ_TN_SHIFT), lambda i, j: (0, j))], "in_proj_shift")
    p_rest = call(_inproj_plain_kernel, w_rest, IN_PROJ_TN, [], [], "in_proj")
    return p_rwkv, p_rest


def _bias_kernel(tab_ref, o_ref):
    h = pl.program_id(0)
    qi = lax.broadcasted_iota(jnp.int32, (BLOCK, 3 * BLOCK), 0)
    kj = lax.broadcasted_iota(jnp.int32, (BLOCK, 3 * BLOCK), 1)
    rel = kj - BLOCK - qi
    n = jnp.abs(rel)
    n2 = n * n
    large = jnp.full_like(n, NUM_BUCKETS // 4)
    for kpow in range(1, 8):
        large = large + jnp.where(n2 >= 64 * (2 ** kpow), 1, 0)
    bucket = jnp.where(rel > 0, NUM_BUCKETS // 2, 0) + jnp.where(n < NUM_BUCKETS // 4, n, large)
    acc = jnp.zeros((BLOCK, 3 * BLOCK), F32)
    for bkt in range(NUM_BUCKETS):
        acc = jnp.where(bucket == bkt, tab_ref[h, bkt], acc)
    o_ref[0] = acc


def _bias_tiles(rel_bias_t):
    nh = rel_bias_t.shape[0]
    return pl.pallas_call(
        _bias_kernel,
        grid=(nh,),
        in_specs=[pl.BlockSpec(memory_space=pltpu.SMEM)],
        out_specs=pl.BlockSpec((1, BLOCK, 3 * BLOCK), lambda h: (h, 0, 0)),
        out_shape=jax.ShapeDtypeStruct((nh, BLOCK, 3 * BLOCK), F32),
        compiler_params=_cparams("arbitrary"),
        name="t5_bias_tiles",
    )(rel_bias_t)


def _rwkv_kernel(pr_ref, pk_ref, pv_ref, pl_ref,
                 w0_ref, wup_ref, a0_ref, aup_ref, gup_ref, kk_ref, ka_ref, rk_ref,
                 gng_ref, gnb_ref, o_ref,
                 al_s, lw_s, cum_s, rr_s, k_s, kk_s, r_s, v_s, y_s, m_s, n_s, *, seq):
    S = seq
    C = RWKV_CHUNK
    nchunk = S // C
    row = lax.broadcasted_iota(jnp.int32, (S, 1), 0)
    lane = lax.broadcasted_iota(jnp.int32, (1, LANES), 1)
    head0 = lane < HEAD_DIM
    ri = lax.broadcasted_iota(jnp.int32, (LANES, LANES), 0)
    ci = lax.broadcasted_iota(jnp.int32, (LANES, LANES), 1)
    same_head = (ri < HEAD_DIM) == (ci < HEAD_DIM)
    eye = ri == ci
    seg_ones = jnp.where(same_head, 1.0, 0.0).astype(BF16)

    GRP = min(S, 4 * C)
    gi = lax.broadcasted_iota(jnp.int32, (GRP, GRP), 0)
    gj = lax.broadcasted_iota(jnp.int32, (GRP, GRP), 1)
    same_chunk = (gi // C) == (gj // C)
    scan_tri = (jnp.where(same_chunk & (gj <= gi), 1.0, 0.0).astype(BF16),
                jnp.where(same_chunk & (gj >= gi), 1.0, 0.0).astype(BF16))

    r = pr_ref[0].astype(F32)
    k = pk_ref[0].astype(F32)
    v = pv_ref[0].astype(F32)
    lo = pl_ref[0].astype(F32)
    l1 = lo[:, :LANES]
    l2 = lo[:, LANES:]
    g = _bdot(_sigmoid(l2), gup_ref[...])
    kkr = k * kk_ref[...]
    kk = kkr * lax.rsqrt(jnp.maximum(_split_dot(kkr * kkr, seg_ones), 1e-12))
    v_s[...] = v
    k_s[...] = k
    kk_s[...] = kk
    r_s[...] = r
    tanh_l1 = jnp.tanh(l1)

    for z in range(2):
        wl = w0_ref[z:z + 1, :] + _bdot(tanh_l1, wup_ref[z])
        half_c = 0.5 * math.exp(-0.5) * math.log2(math.e)
        logw = -half_c * jnp.tanh(0.5 * wl) - half_c
        al_s[z] = a0_ref[z:z + 1, :] + _bdot(l1, aup_ref[z])
        lw_hi = logw.astype(BF16)
        lw_lo = (logw - lw_hi.astype(F32)).astype(BF16)
        tri = scan_tri[z]
        cum = jnp.concatenate(
            [jnp.dot(tri, lw_hi[g0:g0 + GRP], preferred_element_type=F32)
             + jnp.dot(tri, lw_lo[g0:g0 + GRP], preferred_element_type=F32)
             for g0 in range(0, S, GRP)], axis=0)
        lw_s[z] = logw
        cum_s[z] = cum

    Q = 4 * HEAD_DIM
    qi = lax.broadcasted_iota(jnp.int32, (Q, Q), 0)
    qj = lax.broadcasted_iota(jnp.int32, (Q, Q), 1)
    same_blk = (qi // HEAD_DIM) == (qj // HEAD_DIM)
    ct = lax.broadcasted_iota(jnp.int32, (C, Q), 0)
    cj = lax.broadcasted_iota(jnp.int32, (C, Q), 1)
    dt = ((cj % C) - ct) * jnp.where(cj < 2 * C, 1, -1)
    strict_m = dt < 0
    incl_m = dt <= 0

    def scaled(z, rz):
        a = _sigmoid(al_s[z, rz, :])
        al_s[z, rz, :] = a
        kkc = kk_s[rz, :]
        cum = cum_s[z, rz, :]
        e_out = jnp.exp2(-cum)
        kd = k_s[rz, :] * (1.0 + (a - 1.0) * ka_ref[...])
        return (kkc * jnp.exp2(cum - lw_s[z, rz, :]), r_s[rz, :] * jnp.exp2(cum), kd * e_out, (a * kkc) * e_out)

    def bd(a):
        return jnp.where(same_blk, jnp.concatenate([a.astype(BF16)] * 4, axis=0), 0.0)

    def dot(a, b):
        return jnp.dot(a.astype(BF16), b, preferred_element_type=F32)

    def dot_nt(a, b):
        return lax.dot_general(a.astype(BF16), b, (((1,), (1,)), ((), ())), preferred_element_type=F32)

    U = min(RWKV_UNROLL, nchunk)

    def chunk_of(g, u, z):
        c = g * U + u
        return c if z == 0 else nchunk - 1 - c

    def rows_of(c):
        return pl.ds(c * C, C) if isinstance(c, int) else pl.ds(pl.multiple_of(c * C, C), C)

    def phase1(g, between=lambda k: None):
        ur = range(U)
        cs = [(chunk_of(g, u, 0), chunk_of(g, u, 1)) for u in ur]
        rws = [(rows_of(c0_), rows_of(c1_)) for c0_, c1_ in cs]
        parts = [[scaled(z, r_[z]) for z in range(2)] for r_ in rws]
        both = lambda u, i: jnp.concatenate([parts[u][0][i], parts[u][1][i]], axis=1)
        kt = [both(u, 0) for u in ur]
        rt = [both(u, 1) for u in ur]
        vbd = [bd(jnp.concatenate([v_s[r_[0], :], v_s[r_[1], :]], axis=1)) for r_ in rws]
        lhs = [jnp.concatenate([kt[u], rt[u]], axis=0) for u in ur]
        gb = [dot_nt(lhs[u], bd(both(u, 3))) for u in ur]
        gk = [dot_nt(lhs[u], bd(both(u, 2))) for u in ur]
        between(0)
        a_rb = [jnp.where(incl_m, gb[u][C:], 0.0) for u in ur]
        a_kr = [jnp.concatenate([jnp.where(strict_m, gk[u][:C], 0.0),
                                 jnp.where(incl_m, gk[u][C:], 0.0)], axis=0) for u in ur]
        yy = [jnp.where(strict_m, -gb[u][:C], 0.0) for u in ur]
        x = list(yy)
        p = [dot(yy[u], bd(yy[u])) for u in ur]
        yy = p
        between(1)
        for step in range(1, 6):
            last = step == 5
            p = [dot(x[u] if last else jnp.concatenate([yy[u], x[u]], axis=0), bd(yy[u])) for u in ur]
            x = [x[u] + yy[u] + (p[u] if last else p[u][C:]) for u in ur]
            if not last:
                yy = [p[u][:C] for u in ur]
            between(1 + step)
        avr = [dot(a_kr[u], vbd[u]) for u in ur]
        between(7)
        tz = []
        for u in ur:
            av = avr[u][:C]
            zz = jnp.concatenate([kt[u], av], axis=1)
            tz.append(zz + dot(x[u], jnp.concatenate([bd(kt[u]), bd(av)], axis=1)))
        cor = [dot(a_rb[u], jnp.concatenate([bd(tz[u][:, :Q]), bd(tz[u][:, Q:])], axis=1)) for u in ur]
        for u in ur:
            rr = rt[u] - cor[u][:, :Q]
            yl = avr[u][C:] - cor[u][:, Q:]
            for z in range(2):
                zl = slice(z * LANES, (z + 1) * LANES)
                rz = rws[u][z]
                y_s[z, rz, :] = yl[:, zl]
                rr_s[z, rz, :] = rr[:, zl]
                far = C - 1 if z == 0 else 0
                pt = jnp.exp2(cum_s[z, rz, :][far:far + 1, :])
                ends = jnp.concatenate([parts[u][z][3] * pt, parts[u][z][2] * pt], axis=0)
                rhs = jnp.concatenate(
                    [jnp.concatenate([-tz[u][:, zl], -tz[u][:, Q + z * LANES:Q + (z + 1) * LANES]], axis=1),
                     jnp.concatenate([jnp.zeros((C, LANES), F32), v_s[rz, :]], axis=1)], axis=0)
                mn = _bdot(ends.T, rhs)
                m_s[z, cs[u][z]] = jnp.where(eye, jnp.broadcast_to(pt, (LANES, LANES)), 0.0) \
                    + jnp.where(same_head, mn[:, :LANES], 0.0)
                n_s[z, cs[u][z]] = jnp.where(same_head, mn[:, LANES:], 0.0)

    def carry_step(g, u, hs):
        for z in range(2):
            c = chunk_of(g, u, z)
            rows = rows_of(c)
            y_s[z, rows, :] = y_s[z, rows, :] + dot(rr_s[z, rows, :], hs[z])
            hs[z] = (dot(m_s[z, c], hs[z]) + n_s[z, c]).astype(BF16)

    def phase2(g, hs):
        hs = list(hs)
        for u in range(U):
            carry_step(g, u, hs)
        return tuple(hs)

    def lagged(g, hs):
        hs = list(hs)
        steps_per_gap = -(-U // 8)

        def between(k):
            for u in range(k * steps_per_gap, min(U, (k + 1) * steps_per_gap)):
                carry_step(g - 1, u, hs)

        phase1(g, between)
        return tuple(hs)

    ngroup = nchunk // U
    h0 = jnp.zeros((LANES, LANES), BF16)
    phase1(0)
    hs = lax.fori_loop(1, ngroup, lagged, (h0, h0))
    phase2(ngroup - 1, hs)

    y = y_s[0] + y_s[1]
    inv_n = 1.0 / HEAD_DIM
    mu_y = _split_dot(y, seg_ones) * inv_n
    yc = y - mu_y
    var_y = _split_dot(yc * yc, seg_ones) * inv_n
    yn = yc * lax.rsqrt(var_y + RWKV_GN_EPS) * gng_ref[...] + gnb_ref[...]
    kd_sum = k_s[...] * (2.0 + (al_s[0] + al_s[1] - 2.0) * ka_ref[...])
    bonus = _split_dot(r_s[...] * kd_sum * rk_ref[...], seg_ones) * v_s[...]
    o_ref[0] = ((yn + bonus) * g).astype(o_ref.dtype)


def _rwkv_branch(p, w0, w_up, a0, a_up, g_up, k_k, k_a, r_k, gn_g, gn_b, out_dtype):
    B, S, _ = p.shape
    npair = RWKV_DIM // LANES
    zpad = jnp.zeros((2, LANES - DECAY_LORA, RWKV_DIM), F32)
    wup_p = jnp.concatenate([w_up, zpad], axis=1).astype(BF16)
    aup_p = jnp.concatenate([zpad, a_up], axis=1).astype(BF16)
    row = lambda a: a.reshape(1, RWKV_DIM)
    lora_blk = 3 * RWKV_DIM // (2 * LANES)

    def pspec(off):
        return pl.BlockSpec((1, S, LANES), lambda b, j: (b, 0, off + j))

    def vspec(rows, off=0):
        return pl.BlockSpec((rows, LANES), lambda b, j: (0, off + j))

    scratch = [pltpu.VMEM((2, S, LANES), F32) for _ in range(4)] + [
        pltpu.VMEM((S, LANES), F32) for _ in range(4)] + [pltpu.VMEM((2, S, LANES), F32),
        pltpu.VMEM((2, S // RWKV_CHUNK, LANES, LANES), F32),
        pltpu.VMEM((2, S // RWKV_CHUNK, LANES, LANES), F32)]
    return pl.pallas_call(
        functools.partial(_rwkv_kernel, seq=S),
        grid=(B, npair),
        in_specs=[pspec(0), pspec(npair), pspec(2 * npair),
                  pl.BlockSpec((1, S, 2 * LANES), lambda b, j: (b, 0, lora_blk)),
                  vspec(2),
                  pl.BlockSpec((2, LANES, LANES), lambda b, j: (0, 0, j)),
                  vspec(2),
                  pl.BlockSpec((2, LANES, LANES), lambda b, j: (0, 0, j)),
                  vspec(GATE_LORA), vspec(1), vspec(1), vspec(1), vspec(1), vspec(1)],
        out_specs=pl.BlockSpec((1, S, LANES), lambda b, j: (b, 0, j)),
        out_shape=jax.ShapeDtypeStruct((B, S, RWKV_DIM), out_dtype),
        scratch_shapes=scratch,
        compiler_params=_cparams("arbitrary", "arbitrary"),
        name="rwkv7_scan",
    )(p, p, p, p, w0, wup_p, a0, aup_p, g_up.astype(BF16),
      row(k_k), row(k_a), row(r_k), row(gn_g), row(gn_b))


def _win_kernel(q_ref, k_ref, v_ref, biast_ref, sink_ref, o_ref, k4_s, vt_s, *, seq):
    S = seq
    G = WIN_GROUP
    NB = S // BLOCK
    GW = G * HEAD_DIM
    hk = pl.program_id(1)
    odd = (hk % 2) == 1
    lane = lax.broadcasted_iota(jnp.int32, (1, LANES), 1)

    def own_head_twice(full):
        mine_first = jnp.where(odd, pltpu.roll(full, HEAD_DIM, 1), full)
        return jnp.where(lane < HEAD_DIM, mine_first, pltpu.roll(mine_first, HEAD_DIM, 1))

    k2 = own_head_twice(k_ref[0].astype(F32)).astype(BF16)
    v2 = own_head_twice(v_ref[0].astype(F32))
    zero_k = jnp.zeros((BLOCK, GW), BF16)
    zero_v = jnp.zeros((HEAD_DIM, BLOCK), BF16)
    k4_s[0] = zero_k
    k4_s[NB + 1] = zero_k
    vt_s[0] = zero_v
    vt_s[NB + 1] = zero_v
    for blk in range(NB):
        rows = slice(blk * BLOCK, (blk + 1) * BLOCK)
        k4_s[blk + 1] = jnp.concatenate([k2[rows], k2[rows]], axis=1)
        vt_s[blk + 1] = v2[rows].T[:HEAD_DIM].astype(BF16)
    log2e = math.log2(math.e)
    bias_t = jnp.concatenate([biast_ref[gi] for gi in range(G)], axis=1) * log2e
    krow = lax.broadcasted_iota(jnp.int32, (3 * BLOCK, G * BLOCK), 0)
    qcol = lax.broadcasted_iota(jnp.int32, (3 * BLOCK, G * BLOCK), 1) % BLOCK
    bias_t = jnp.where(jnp.abs(krow - BLOCK - qcol) <= WINDOW, bias_t, NEG_INF)
    cgrp = lax.broadcasted_iota(jnp.int32, (1, G * BLOCK), 1) // BLOCK
    sink_row = jnp.zeros((1, G * BLOCK), F32)
    for gi in range(G):
        sink_row = jnp.where(cgrp == gi, sink_ref[hk * G + gi] * log2e, sink_row)
    lane_q = lax.broadcasted_iota(jnp.int32, (1, GW), 1) // HEAD_DIM
    U = min(WIN_UNROLL, NB)

    def body(it, carry):
        ns = [it * U + u for u in range(U)]
        sts = []
        for n in ns:
            qb = q_ref[0, pl.ds(pl.multiple_of(n * BLOCK, BLOCK), BLOCK), :].astype(F32) \
                * (log2e * HEAD_DIM ** -0.5)
            qm = jnp.concatenate([jnp.where(lane_q == gi, qb, 0.0) for gi in range(G)], axis=0).astype(BF16)
            kband = jnp.concatenate([k4_s[n], k4_s[n + 1], k4_s[n + 2]], axis=0)
            sts.append(lax.dot_general(kband, qm, (((1,), (1,)), ((), ())), preferred_element_type=F32))
        es, dens = [], []
        for u, n in enumerate(ns):
            st = sts[u] + bias_t
            st = jnp.concatenate([st[:BLOCK] + jnp.where(n == 0, NEG_INF, 0.0), st[BLOCK:2 * BLOCK],
                                  st[2 * BLOCK:] + jnp.where(n == NB - 1, NEG_INF, 0.0)], axis=0)
            m = jnp.maximum(jnp.max(st, axis=0, keepdims=True), sink_row)
            e = jnp.exp2(st - m)
            es.append(e.astype(BF16))
            dens.append(jnp.sum(e, axis=0, keepdims=True) + jnp.exp2(sink_row - m))
        for u, n in enumerate(ns):
            vband = jnp.concatenate([vt_s[n], vt_s[n + 1], vt_s[n + 2]], axis=1)
            acc = jnp.dot(vband, es[u], preferred_element_type=F32) / dens[u]
            o_t = jnp.concatenate([acc[:, gi * BLOCK:(gi + 1) * BLOCK] for gi in range(G)], axis=0)
            o_ref[0, pl.ds(pl.multiple_of(n * BLOCK, BLOCK), BLOCK), :] = o_t.T.astype(o_ref.dtype)
        return carry

    lax.fori_loop(0, NB // U, body, 0)


def _window_branch(p, bias_tiles_t, sink, out_dtype):
    B, S, _ = p.shape
    gw = WIN_GROUP * HEAD_DIM
    nb = S // BLOCK
    assert nb % min(WIN_UNROLL, nb) == 0
    qoff = OFF_WIN // gw
    koff = (OFF_WIN + WIN_Q_COLS) // LANES
    voff = (OFF_WIN + WIN_Q_COLS + WIN_KV_COLS) // LANES
    return pl.pallas_call(
        functools.partial(_win_kernel, seq=S),
        grid=(B, WIN_KV_HEADS),
        in_specs=[pl.BlockSpec((1, S, gw), lambda b, h: (b, 0, qoff + h)),
                  pl.BlockSpec((1, S, LANES), lambda b, h: (b, 0, koff + h // 2)),
                  pl.BlockSpec((1, S, LANES), lambda b, h: (b, 0, voff + h // 2)),
                  pl.BlockSpec((WIN_GROUP, 3 * BLOCK, BLOCK), lambda b, h: (h, 0, 0)),
                  pl.BlockSpec(memory_space=pltpu.SMEM)],
        out_specs=pl.BlockSpec((1, S, gw), lambda b, h: (b, 0, h)),
        out_shape=jax.ShapeDtypeStruct((B, S, WIN_Q_COLS), out_dtype),
        scratch_shapes=[pltpu.VMEM((nb + 2, BLOCK, gw), BF16),
                        pltpu.VMEM((nb + 2, HEAD_DIM, BLOCK), BF16)],
        compiler_params=_cparams("arbitrary", "arbitrary"),
        name="window_attn",
    )(p, p, p, bias_tiles_t, sink)


def _diff_kernel(q_ref, k_ref, v_ref, bandt_ref, tab_ref, lam_ref, g_ref, o_ref, ka_s, vt_s, *,
                 seq, lambda_init):
    S = seq
    NB = S // BLOCK
    CB = min(DIFF_CHUNK_BLOCKS, NB)
    h = pl.program_id(1)
    it = pl.program_id(2)
    halves = q_ref.shape[1] // BLOCK
    log2e = math.log2(math.e)
    lane = lax.broadcasted_iota(jnp.int32, (1, LANES), 1)
    c0 = lane < HEAD_DIM

    @pl.when(it == 0)
    def _():
        jblk = lax.broadcasted_iota(jnp.int32, (S, LANES), 0) // BLOCK
        lj = lax.broadcasted_iota(jnp.int32, (S, LANES), 1)
        ka_s[:, :LANES] = k_ref[0].astype(BF16)
        ka_s[:, LANES:] = jnp.where((lj < 2 * NB) & (lj % NB == jblk), 1.0, 0.0).astype(BF16)
        for blk in range(NB):
            vt_s[blk] = v_ref[0, blk * BLOCK:(blk + 1) * BLOCK, :].astype(F32).T.astype(BF16)

    far_l = tab_ref[WIN_Q_HEADS + h, NUM_BUCKETS // 2 - 1] * log2e
    far_r = tab_ref[WIN_Q_HEADS + h, NUM_BUCKETS - 1] * log2e
    band_t = bandt_ref[0] * log2e
    lam = lam_ref[...].astype(F32)
    lam_full = (jnp.exp(jnp.sum(lam[0:1] * lam[1:2], axis=-1, keepdims=True))
                - jnp.exp(jnp.sum(lam[2:3] * lam[3:4], axis=-1, keepdims=True)) + lambda_init)
    ibs = [it * halves + hf for hf in range(halves)]
    zblk = jnp.zeros((BLOCK, BLOCK), F32)
    qq, bands = [], []
    for hf in range(halves):
        q = q_ref[0, hf * BLOCK:(hf + 1) * BLOCK, :].astype(F32) * (log2e * HEAD_DIM ** -0.5)
        kb = lane % NB
        val = jnp.where(kb < ibs[hf] - 1, far_l, jnp.where(kb > ibs[hf] + 1, far_r, 0.0))
        val = jnp.where(lane < 2 * NB, val, 0.0)
        hi = val.astype(BF16)
        lo = (val - hi.astype(F32)).astype(BF16)
        bcols = jnp.broadcast_to(jnp.where(lane < NB, hi, lo), (BLOCK, LANES))
        qq.append(jnp.concatenate(
            [jnp.concatenate([jnp.where(c0, q, 0.0).astype(BF16), bcols], axis=1),
             jnp.concatenate([jnp.where(c0, 0.0, q).astype(BF16), bcols], axis=1)], axis=0))
        tiles = [jnp.where(ibs[hf] > 0, band_t[:BLOCK], 0.0), band_t[BLOCK:2 * BLOCK],
                 jnp.where(ibs[hf] < NB - 1, band_t[2 * BLOCK:], 0.0)]
        col = jnp.concatenate([zblk] * hf + tiles + [zblk] * (CB - 3 - hf), axis=0)
        bands.append(jnp.concatenate([col, col], axis=1))

    m_run, l_run, acc = [None] * halves, [None] * halves, [None] * halves
    for r in range(NB // CB):
        kbs = [lax.rem(it * halves - 1 + CB * r + u + NB, NB) for u in range(CB)]
        kc = jnp.concatenate([ka_s[pl.ds(pl.multiple_of(kb * BLOCK, BLOCK), BLOCK), :] for kb in kbs], axis=0)
        vct = jnp.concatenate([vt_s[kb] for kb in kbs], axis=1)
        s = [lax.dot_general(kc, qq[hf], (((1,), (1,)), ((), ())), preferred_element_type=F32)
             for hf in range(halves)]
        if r == 0:
            s = [s[hf] + bands[hf] for hf in range(halves)]
        m_loc = [jnp.max(s[hf], axis=0, keepdims=True) for hf in range(halves)]
        for hf in range(halves):
            if r == 0:
                m_new = m_loc[hf]
                e = jnp.exp2(s[hf] - m_new)
                l_run[hf] = jnp.sum(e, axis=0, keepdims=True)
                acc[hf] = jnp.dot(vct, e.astype(BF16), preferred_element_type=F32)
            else:
                m_new = jnp.maximum(m_run[hf], m_loc[hf])
                alpha = jnp.exp2(m_run[hf] - m_new)
                e = jnp.exp2(s[hf] - m_new)
                l_run[hf] = alpha * l_run[hf] + jnp.sum(e, axis=0, keepdims=True)
                acc[hf] = alpha * acc[hf] + jnp.dot(vct, e.astype(BF16), preferred_element_type=F32)
            m_run[hf] = m_new
    for hf in range(halves):
        rinv = 1.0 / l_run[hf]
        o_t = acc[hf][:, :BLOCK] * rinv[:, :BLOCK] - acc[hf][:, BLOCK:] * (lam_full * rinv[:, BLOCK:])
        o = o_t.T
        o = o * lax.rsqrt(jnp.mean(o * o, axis=-1, keepdims=True) + 1e-5) * g_ref[...] * (1.0 - lambda_init)
        o_ref[0, hf * BLOCK:(hf + 1) * BLOCK, :] = o.astype(o_ref.dtype)


def _diff_branch(p, band_tiles, rel_bias_t, lam, subln_g, lambda_init, out_dtype):
    B, S, _ = p.shape
    qoff = OFF_DIFF // LANES
    koff = (OFF_DIFF + DIFF_QK_COLS) // LANES
    voff = (OFF_DIFF + 2 * DIFF_QK_COLS) // LANES
    nb = S // BLOCK
    cb = min(DIFF_CHUNK_BLOCKS, nb)
    tq = DIFF_Q_TILE
    while tq > (cb - 2) * BLOCK or S % tq:
        tq //= 2
    assert 2 * nb <= LANES and tq >= BLOCK and nb % cb == 0 and nb >= 4
    return pl.pallas_call(
        functools.partial(_diff_kernel, seq=S, lambda_init=lambda_init),
        grid=(B, DIFF_HEADS, S // tq),
        in_specs=[pl.BlockSpec((1, tq, LANES), lambda b, h, i: (b, i, qoff + h)),
                  pl.BlockSpec((1, S, LANES), lambda b, h, i: (b, 0, koff + h)),
                  pl.BlockSpec((1, S, LANES), lambda b, h, i: (b, 0, voff + h)),
                  pl.BlockSpec((1, 3 * BLOCK, BLOCK), lambda b, h, i: (WIN_Q_HEADS + h, 0, 0)),
                  pl.BlockSpec(memory_space=pltpu.SMEM),
                  pl.BlockSpec((4, HEAD_DIM), lambda b, h, i: (0, 0)),
                  pl.BlockSpec((1, DIFF_V_DIM), lambda b, h, i: (0, 0))],
        out_specs=pl.BlockSpec((1, tq, LANES), lambda b, h, i: (b, i, h)),
        out_shape=jax.ShapeDtypeStruct((B, S, DIFF_V_COLS), out_dtype),
        scratch_shapes=[pltpu.VMEM((S, 2 * LANES), BF16),
                        pltpu.VMEM((S // BLOCK, DIFF_V_DIM, BLOCK), BF16)],
        compiler_params=_cparams("arbitrary", "arbitrary", "arbitrary"),
        name="diff_attn",
    )(p, p, p, jnp.swapaxes(band_tiles, 1, 2), rel_bias_t, lam, subln_g.reshape(1, DIFF_V_DIM))


def _mem_kernel(q_ref, mem_ref, wkv_ref, o_ref):
    kv = _bdot(mem_ref[0], wkv_ref[...])
    kmat = kv[:, :MEM_WIDTH]
    vmat = kv[:, MEM_WIDTH:]
    q = q_ref[0].astype(F32) * (HEAD_DIM ** -0.5)
    lane = lax.broadcasted_iota(jnp.int32, (1, MEM_WIDTH), 1) // HEAD_DIM
    out = jnp.zeros(q.shape, F32)
    for hh in range(MEM_HEADS):
        s = _bdot_nt(jnp.where(lane == hh, q, 0.0), kmat)
        m = jnp.max(s, axis=-1, keepdims=True)
        e = jnp.exp(s - m)
        a = e / jnp.sum(e, axis=-1, keepdims=True)
        out = jnp.where(lane == hh, _bdot(a, vmat), out)
    o_ref[0] = out.astype(o_ref.dtype)


def _memory_branch(p, mem, w_kv, out_dtype):
    B, S, _ = p.shape
    nm, d = mem.shape[1], mem.shape[2]
    return pl.pallas_call(
        _mem_kernel,
        grid=(B,),
        in_specs=[pl.BlockSpec((1, S, MEM_WIDTH), lambda b: (b, 0, OFF_MEM // MEM_WIDTH)),
                  pl.BlockSpec((1, nm, d), lambda b: (b, 0, 0)),
                  pl.BlockSpec((d, 2 * MEM_WIDTH), lambda b: (0, 0))],
        out_specs=pl.BlockSpec((1, S, MEM_WIDTH), lambda b: (b, 0, 0)),
        out_shape=jax.ShapeDtypeStruct((B, S, MEM_WIDTH), out_dtype),
        compiler_params=_cparams("arbitrary"),
        name="memory_attn",
    )(p, mem, w_kv)


def _merge_kernel(x_ref, yr_ref, yw_ref, yd_ref, ym_ref, pg_ref, wb_ref, wo_ref, g_ref, b_ref,
                  o_ref, *, d_model, alpha):
    D = d_model
    ys = (yr_ref[...], yw_ref[...], yd_ref[...], ym_ref[...])
    off = 0
    merged = jnp.zeros((x_ref.shape[0], D), F32)
    for bi, yb in enumerate(ys):
        rows = yb.shape[1]
        proj = _bdot(yb, wb_ref[off:off + rows, :])
        merged = merged + _sigmoid(pg_ref[:, bi * D:(bi + 1) * D].astype(F32)) * proj
        off += rows
    z = alpha * x_ref[...] + _bdot(merged, wo_ref[...])
    o_ref[...] = _layer_norm(z, g_ref[...], b_ref[...])


def _merge(x2, y_rwkv, y_win, y_diff, y_mem, p2, w_branch, w_out, ln_g, ln_b, alpha, tm):
    T, D = x2.shape
    gate_blk = OFF_GATE // (N_BRANCHES * D)
    full = lambda a: pl.BlockSpec(a.shape, lambda i: (0, 0))
    tile = lambda w: pl.BlockSpec((tm, w), lambda i: (i, 0))
    return pl.pallas_call(
        functools.partial(_merge_kernel, d_model=D, alpha=alpha),
        grid=(T // tm,),
        in_specs=[tile(D), tile(y_rwkv.shape[1]), tile(y_win.shape[1]), tile(y_diff.shape[1]),
                  tile(y_mem.shape[1]),
                  pl.BlockSpec((tm, N_BRANCHES * D), lambda i: (i, gate_blk)),
                  full(w_branch), full(w_out),
                  pl.BlockSpec((1, D), lambda i: (0, 0)), pl.BlockSpec((1, D), lambda i: (0, 0))],
        out_specs=tile(D),
        out_shape=jax.ShapeDtypeStruct((T, D), F32),
        compiler_params=_cparams("arbitrary"),
        name="merge_out_ln",
    )(x2, y_rwkv, y_win, y_diff, y_mem, p2, w_branch, w_out, ln_g.reshape(1, D), ln_b.reshape(1, D))


def _route_kernel(x_ref, wr_ref, xe_ref, slot_ref, gate_ref, xb_s, lg_s, *, seq, cap):
    S = seq
    E = N_EXPERTS
    TR = min(S, 512)
    wr = wr_ref[...]
    w_hi = wr.astype(BF16)
    w_md = (wr - w_hi.astype(F32)).astype(BF16)
    dot = lambda a, b: jnp.dot(a, b, preferred_element_type=F32)

    def logit_tile(t, carry):
        rows = pl.ds(pl.multiple_of(t * TR, TR), TR)
        x = x_ref[0, rows, :]
        x_hi = x.astype(BF16)
        x_r1 = x - x_hi.astype(F32)
        x_md = x_r1.astype(BF16)
        x_lo = (x_r1 - x_md.astype(F32)).astype(BF16)
        xb_s[rows, :] = x_hi
        lg_s[rows, :] = (dot(x_hi, w_hi) + (dot(x_hi, w_md) + dot(x_md, w_hi))
                         + (dot(x_md, w_md) + dot(x_lo, w_hi)))
        return carry

    lax.fori_loop(0, S // TR, logit_tile, 0)
    col = lax.broadcasted_iota(jnp.int32, (1, LANES), 1)
    logits = jnp.where(col < E, lg_s[...], NEG_INF)
    mx = jnp.max(logits, axis=-1, keepdims=True)
    ex = jnp.exp(logits - mx)
    aff = ex / jnp.sum(ex, axis=-1, keepdims=True)
    aff_t = aff.T[0:E, :]
    bits = pltpu.bitcast(aff_t, jnp.int32)

    def bis(_, carry):
        lo_b, step = carry
        cand = lo_b + step
        cnt = jnp.sum(jnp.where(bits >= cand, 1.0, 0.0), axis=-1, keepdims=True)
        return jnp.where(cnt >= cap, cand, lo_b), step // 2

    lo0 = jnp.zeros((E, 1), jnp.int32)
    thr, _ = lax.fori_loop(0, 31, bis, (lo0, jnp.full((E, 1), 1 << 30, jnp.int32)))
    above = bits > thr
    tie = bits == thr
    n_above = jnp.sum(jnp.where(above, 1.0, 0.0), axis=-1, keepdims=True)
    PB = min(S, 256)
    tri = (lax.broadcasted_iota(jnp.int32, (PB, PB), 0)
           < lax.broadcasted_iota(jnp.int32, (PB, PB), 1)).astype(BF16)

    def prefix_count(mask_f):
        parts, run = [], jnp.zeros((E, 1), F32)
        for blk in range(S // PB):
            seg = mask_f[:, blk * PB:(blk + 1) * PB]
            parts.append(jnp.dot(seg.astype(BF16), tri, preferred_element_type=F32) + run)
            run = run + jnp.sum(seg, axis=-1, keepdims=True)
        return jnp.concatenate(parts, axis=1)

    tie_rank = prefix_count(jnp.where(tie, 1.0, 0.0))
    sel = above | (tie & (tie_rank < cap - n_above))
    pos = prefix_count(jnp.where(sel, 1.0, 0.0))
    slot = jnp.where(sel, pos, -1.0)
    slot_pad = jnp.concatenate([slot, jnp.full((LANES - E, S), -1.0, F32)], axis=0)
    slot_ref[0] = slot_pad.T
    ci = lax.broadcasted_iota(jnp.int32, (cap, S), 0).astype(F32)
    x_hi = xb_s[...]
    for e in range(E):
        onehot = slot[e:e + 1, :] == ci
        xe_ref[e, 0] = jnp.dot(jnp.where(onehot, 1.0, 0.0).astype(BF16), x_hi,
                               preferred_element_type=F32).astype(xe_ref.dtype)
        gsel = jnp.sum(jnp.where(onehot, aff_t[e:e + 1, :], 0.0), axis=-1, keepdims=True)
        gate_ref[e, 0] = jnp.broadcast_to(gsel, (cap, LANES))


def _route(x3, router_pad, cap):
    B, S, D = x3.shape
    E = N_EXPERTS
    return pl.pallas_call(
        functools.partial(_route_kernel, seq=S, cap=cap),
        grid=(B,),
        in_specs=[pl.BlockSpec((1, S, D), lambda b: (b, 0, 0)),
                  pl.BlockSpec((D, LANES), lambda b: (0, 0))],
        out_specs=[pl.BlockSpec((E, 1, cap, D), lambda b: (0, b, 0, 0)),
                   pl.BlockSpec((1, S, LANES), lambda b: (b, 0, 0)),
                   pl.BlockSpec((E, 1, cap, LANES), lambda b: (0, b, 0, 0))],
        out_shape=[jax.ShapeDtypeStruct((E, B, cap, D), BF16),
                   jax.ShapeDtypeStruct((B, S, LANES), F32),
                   jax.ShapeDtypeStruct((E, B, cap, LANES), F32)],
        scratch_shapes=[pltpu.VMEM((S, D), BF16), pltpu.VMEM((S, LANES), F32)],
        compiler_params=_cparams("arbitrary"),
        name="moe_route_gather",
    )(x3, router_pad)


def _ffn_kernel(xe_ref, g_ref, wg_ref, wu_ref, wd_ref, ye_ref, acc_s):
    f = pl.program_id(1)
    nb, cap, D = xe_ref.shape[1], xe_ref.shape[2], xe_ref.shape[3]

    @pl.when(f == 0)
    def _():
        acc_s[...] = jnp.zeros(acc_s.shape, F32)

    xe = xe_ref[0].reshape(nb * cap, D)
    hg = jnp.dot(xe, wg_ref[0, 0].astype(BF16), preferred_element_type=F32)
    hu = jnp.dot(xe, wu_ref[0, 0].astype(BF16), preferred_element_type=F32)
    hid = (hg * _sigmoid(hg)) * hu
    acc_s[...] = acc_s[...] + _bdot(hid, wd_ref[0, 0])

    @pl.when(f == pl.num_programs(1) - 1)
    def _():
        gate = g_ref[0].reshape(nb * cap, LANES)[:, 0:1]
        ye_ref[0] = (acc_s[...] * gate).reshape(nb, cap, D).astype(ye_ref.dtype)


def _expert_ffn(xe, gate, w_gate, w_up, w_down, layer, tf):
    E, B, cap, D = xe.shape
    F = w_gate.shape[-1]
    return pl.pallas_call(
        _ffn_kernel,
        grid=(E, F // tf),
        in_specs=[pl.BlockSpec((1, B, cap, D), lambda e, f: (e, 0, 0, 0)),
                  pl.BlockSpec((1, B, cap, LANES), lambda e, f: (e, 0, 0, 0)),
                  pl.BlockSpec((1, 1, D, tf), lambda e, f: (layer, e, 0, f)),
                  pl.BlockSpec((1, 1, D, tf), lambda e, f: (layer, e, 0, f)),
                  pl.BlockSpec((1, 1, tf, D), lambda e, f: (layer, e, f, 0))],
        out_specs=pl.BlockSpec((1, B, cap, D), lambda e, f: (e, 0, 0, 0)),
        out_shape=jax.ShapeDtypeStruct((E, B, cap, D), BF16),
        scratch_shapes=[pltpu.VMEM((B * cap, D), F32)],
        compiler_params=_cparams("arbitrary", "arbitrary"),
        name="moe_expert_ffn",
    )(xe, gate, w_gate, w_up, w_down)


def _scatter_kernel(x_ref, slot_ref, ye_ref, g_ref, b_ref, o_ref, ob_ref, *, cap, alpha):
    ts = x_ref.shape[1]
    slot = slot_ref[0]
    ci = lax.broadcasted_iota(jnp.int32, (ts, cap), 1).astype(F32)
    acc = jnp.zeros((ts, x_ref.shape[2]), F32)
    for e in range(N_EXPERTS):
        onehot = slot[:, e:e + 1] == ci
        acc = acc + jnp.dot(jnp.where(onehot, 1.0, 0.0).astype(BF16), ye_ref[e, 0],
                            preferred_element_type=F32)
    out = _layer_norm(alpha * x_ref[0] + acc, g_ref[...], b_ref[...])
    o_ref[0] = out
    ob_ref[0] = out.astype(ob_ref.dtype)


def _scatter_ln(x3, slot, ye, ln_g, ln_b, alpha, ts):
    B, S, D = x3.shape
    E, _, cap, _ = ye.shape
    return pl.pallas_call(
        functools.partial(_scatter_kernel, cap=cap, alpha=alpha),
        grid=(B, S // ts),
        in_specs=[pl.BlockSpec((1, ts, D), lambda b, i: (b, i, 0)),
                  pl.BlockSpec((1, ts, LANES), lambda b, i: (b, i, 0)),
                  pl.BlockSpec((E, 1, cap, D), lambda b, i: (0, b, 0, 0)),
                  pl.BlockSpec((1, D), lambda b, i: (0, 0)),
                  pl.BlockSpec((1, D), lambda b, i: (0, 0))],
        out_specs=[pl.BlockSpec((1, ts, D), lambda b, i: (b, i, 0)),
                   pl.BlockSpec((1, ts, D), lambda b, i: (b, i, 0))],
        out_shape=[jax.ShapeDtypeStruct((B, S, D), F32), jax.ShapeDtypeStruct((B, S, D), BF16)],
        compiler_params=_cparams("arbitrary", "arbitrary"),
        name="moe_scatter_ln",
    )(x3, slot, ye, ln_g.reshape(1, D), ln_b.reshape(1, D))


def kernel(x, mem, rel_bias, w_in, rwkv_mu, rwkv_w0, rwkv_w_up, rwkv_a0, rwkv_a_up, rwkv_g_up, rwkv_k_k, rwkv_k_a, rwkv_r_k, rwkv_gn_g, rwkv_gn_b, win_sink, diff_lambda, diff_subln_g, mem_w_kv, w_branch, w_out, ln1_g, ln1_b, router, exp_w_gate, exp_w_up, exp_w_down, ln2_g, ln2_b):
    B, S, D = x.shape
    assert D == D_MODEL and w_in.shape[2] == RWKV_COLS + REST_COLS
    depth = w_in.shape[0]
    alpha = (2 * depth) ** 0.25
    cap = CAPACITY_FACTOR * S // N_EXPERTS
    T = B * S
    rel_bias_t = rel_bias.T
    bias_tiles = _bias_tiles(rel_bias_t)
    bias_tiles_t = jnp.swapaxes(bias_tiles, 1, 2)
    act = BF16
    xb = x.astype(BF16)
    for l in range(depth):
        lambda_init = 0.8 - 0.6 * math.exp(-0.3 * l)
        gate0 = w_in.shape[2] - N_BRANCHES * D
        w_rest = jnp.concatenate([w_in[l][:, gate0:], w_in[l][:, RWKV_COLS:gate0]], axis=1).astype(BF16)
        p_rwkv, p2 = _in_proj(xb.reshape(T, D), w_in[l][:, :RWKV_COLS].astype(BF16), w_rest,
                              rwkv_mu[l], S, act)
        p = p2.reshape(B, S, -1)
        y_rwkv = _rwkv_branch(p_rwkv.reshape(B, S, -1), rwkv_w0[l], rwkv_w_up[l], rwkv_a0[l], rwkv_a_up[l],
                              rwkv_g_up[l], rwkv_k_k[l], rwkv_k_a[l], rwkv_r_k[l], rwkv_gn_g[l],
                              rwkv_gn_b[l], act)
        y_win = _window_branch(p, bias_tiles_t, win_sink[l], act)
        y_diff = _diff_branch(p, bias_tiles, rel_bias_t, diff_lambda[l], diff_subln_g[l], lambda_init, act)
        y_mem = _memory_branch(p, mem, mem_w_kv[l].astype(BF16), act)
        x1 = _merge(x.reshape(T, D), y_rwkv.reshape(T, -1), y_win.reshape(T, -1), y_diff.reshape(T, -1),
                    y_mem.reshape(T, -1), p2, w_branch[l].astype(BF16), w_out[l].astype(BF16),
                    ln1_g[l], ln1_b[l], alpha, 512).reshape(B, S, D)
        router_pad = jnp.concatenate([router[l], jnp.zeros((D, LANES - N_EXPERTS), F32)], axis=1)
        xe, slot, gate = _route(x1, router_pad, cap)
        ye = _expert_ffn(xe, gate, exp_w_gate, exp_w_up, exp_w_down, l, 512)
        x, xb = _scatter_ln(x1, slot, ye, ln2_g[l], ln2_b[l], alpha, 512)
    return x
```

```python
import functools
import math

import jax
import jax.numpy as jnp
from jax import lax
from jax.experimental import pallas as pl
from jax.experimental.pallas import tpu as pltpu

F32 = jnp.float32
BF16 = jnp.bfloat16

HEAD_DIM = 64
LANES = 128
RWKV_HEADS = 16
RWKV_DIM = RWKV_HEADS * HEAD_DIM
DECAY_LORA = 64
ICLR_LORA = 64
GATE_LORA = 128
RWKV_GN_EPS = 64e-5
RWKV_COLS = 3 * RWKV_DIM + DECAY_LORA + ICLR_LORA + GATE_LORA
RWKV_CHUNK = 64
RWKV_UNROLL = 8
assert RWKV_CHUNK == HEAD_DIM
WIN_Q_HEADS = 16
WIN_KV_HEADS = 4
WIN_GROUP = WIN_Q_HEADS // WIN_KV_HEADS
WIN_Q_COLS = WIN_Q_HEADS * HEAD_DIM
WIN_KV_COLS = WIN_KV_HEADS * HEAD_DIM
WIN_COLS = WIN_Q_COLS + 2 * WIN_KV_COLS
WINDOW = 128
BLOCK = 128
WIN_UNROLL = 16
DIFF_HEADS = 8
DIFF_V_DIM = 2 * HEAD_DIM
DIFF_QK_COLS = 2 * DIFF_HEADS * HEAD_DIM
DIFF_V_COLS = DIFF_HEADS * DIFF_V_DIM
DIFF_COLS = 2 * DIFF_QK_COLS + DIFF_V_COLS
DIFF_Q_TILE = 8 * BLOCK
DIFF_CHUNK_BLOCKS = 16
MEM_HEADS = 4
MEM_WIDTH = MEM_HEADS * HEAD_DIM
NUM_BUCKETS = 32
N_EXPERTS = 16
CAPACITY_FACTOR = 2
N_BRANCHES = 4
NEG_INF = -1e30
LN_EPS = 1e-5
VMEM_LIMIT = 56 * 1024 * 1024
D_MODEL = 1024
IN_PROJ_TN_SHIFT = 256
IN_PROJ_TN = 1280
MERGE_ROWS = 512
FFN_HIDDEN_TILE = 512
SCATTER_ROWS = 512

OFF_GATE = 0
OFF_WIN = N_BRANCHES * D_MODEL
OFF_DIFF = OFF_WIN + WIN_COLS
OFF_MEM = OFF_DIFF + DIFF_COLS
REST_COLS = OFF_MEM + MEM_WIDTH


def _cparams(*sem):
    return pltpu.CompilerParams(dimension_semantics=sem, vmem_limit_bytes=VMEM_LIMIT)


def _bdot(a, b):
    return jnp.dot(a.astype(BF16), b.astype(BF16), preferred_element_type=F32)


def _bdot_nt(a, b):
    return lax.dot_general(a.astype(BF16), b.astype(BF16), (((1,), (1,)), ((), ())),
                           preferred_element_type=F32)


def _split_dot(a, b_exact):
    hi = a.astype(BF16)
    lo = (a - hi.astype(F32)).astype(BF16)
    return (jnp.dot(hi, b_exact, preferred_element_type=F32)
            + jnp.dot(lo, b_exact, preferred_element_type=F32))


def _sigmoid(x):
    return 0.5 * jnp.tanh(0.5 * x) + 0.5


def _layer_norm(x, g, b):
    mu = jnp.mean(x, axis=-1, keepdims=True)
    xc = x - mu
    var = jnp.mean(xc * xc, axis=-1, keepdims=True)
    return xc * lax.rsqrt(var + LN_EPS) * g + b


def _inproj_shift_kernel(x_ref, w_ref, mu_ref, o_ref):
    acc = jnp.dot(x_ref[...], w_ref[...], preferred_element_type=F32)
    S = acc.shape[0]
    row = lax.broadcasted_iota(jnp.int32, (S, 1), 0)
    prev = jnp.where(row == 0, 0.0, pltpu.roll(acc, 1, 0))
    nxt = jnp.where(row == S - 1, 0.0, pltpu.roll(acc, S - 1, 0))
    out = acc + mu_ref[0:1, :] * (prev - acc) + mu_ref[1:2, :] * (nxt - acc)
    o_ref[...] = out.astype(o_ref.dtype)


def _inproj_plain_kernel(x_ref, w_ref, o_ref):
    o_ref[...] = jnp.dot(x_ref[...], w_ref[...], preferred_element_type=F32).astype(o_ref.dtype)


def _in_proj(x2, w_rwkv, w_rest, mu, seq, out_dtype):
    m, k = x2.shape
    x_spec = pl.BlockSpec((seq, k), lambda i, j: (i, 0))

    def call(kern, w, tn, extra, extra_specs, name):
        n = w.shape[1]
        assert n % tn == 0
        return pl.pallas_call(
            kern,
            grid=(m // seq, n // tn),
            in_specs=[x_spec, pl.BlockSpec((k, tn), lambda i, j: (0, j))] + extra_specs,
            out_specs=pl.BlockSpec((seq, tn), lambda i, j: (i, j)),
            out_shape=jax.ShapeDtypeStruct((m, n), out_dtype),
            compiler_params=_cparams("arbitrary", "arbitrary"),
            name=name,
        )(x2, w, *extra)

    pad = -w_rwkv.shape[1] % IN_PROJ_TN_SHIFT
    w_pad = jnp.concatenate([w_rwkv, jnp.zeros((k, pad), w_rwkv.dtype)], axis=1)
    mu_pad = jnp.concatenate([mu, jnp.zeros((2, pad), mu.dtype)], axis=1)
    p_rwkv = call(_inproj_shift_kernel, w_pad, IN_PROJ_TN_SHIFT, [mu_pad],
                  [pl.BlockSpec((2, IN_PROJ_TN_SHIFT), lambda i, j: (0, j))], "in_proj_shift")
    p_rest = call(_inproj_plain_kernel, w_rest, IN_PROJ_TN, [], [], "in_proj")
    return p_rwkv, p_rest


def _bias_kernel(tab_ref, o_ref):
    h = pl.program_id(0)
    qi = lax.broadcasted_iota(jnp.int32, (BLOCK, 3 * BLOCK), 0)
    kj = lax.broadcasted_iota(jnp.int32, (BLOCK, 3 * BLOCK), 1)
    rel = kj - BLOCK - qi
    n = jnp.abs(rel)
    n2 = n * n
    large = jnp.full_like(n, NUM_BUCKETS // 4)
    for kpow in range(1, 8):
        large = large + jnp.where(n2 >= 64 * (2 ** kpow), 1, 0)
    bucket = jnp.where(rel > 0, NUM_BUCKETS // 2, 0) + jnp.where(n < NUM_BUCKETS // 4, n, large)
    acc = jnp.zeros((BLOCK, 3 * BLOCK), F32)
    for bkt in range(NUM_BUCKETS):
        acc = jnp.where(bucket == bkt, tab_ref[h, bkt], acc)
    o_ref[0] = acc


def _bias_tiles(rel_bias_t):
    nh = rel_bias_t.shape[0]
    return pl.pallas_call(
        _bias_kernel,
        grid=(nh,),
        in_specs=[pl.BlockSpec(memory_space=pltpu.SMEM)],
        out_specs=pl.BlockSpec((1, BLOCK, 3 * BLOCK), lambda h: (h, 0, 0)),
        out_shape=jax.ShapeDtypeStruct((nh, BLOCK, 3 * BLOCK), F32),
        compiler_params=_cparams("arbitrary"),
        name="t5_bias_tiles",
    )(rel_bias_t)


def _rwkv_kernel(pr_ref, pk_ref, pv_ref, pl_ref,
                 w0_ref, wup_ref, a0_ref, aup_ref, gup_ref, kk_ref, ka_ref, rk_ref,
                 gng_ref, gnb_ref, o_ref,
                 al_s, lw_s, cum_s, rr_s, k_s, kk_s, r_s, v_s, y_s, m_s, n_s, *, seq):
    S = seq
    C = RWKV_CHUNK
    nchunk = S // C
    row = lax.broadcasted_iota(jnp.int32, (S, 1), 0)
    lane = lax.broadcasted_iota(jnp.int32, (1, LANES), 1)
    head0 = lane < HEAD_DIM
    ri = lax.broadcasted_iota(jnp.int32, (LANES, LANES), 0)
    ci = lax.broadcasted_iota(jnp.int32, (LANES, LANES), 1)
    same_head = (ri < HEAD_DIM) == (ci < HEAD_DIM)
    eye = ri == ci
    seg_ones = jnp.where(same_head, 1.0, 0.0).astype(BF16)

    GRP = min(S, 4 * C)
    gi = lax.broadcasted_iota(jnp.int32, (GRP, GRP), 0)
    gj = lax.broadcasted_iota(jnp.int32, (GRP, GRP), 1)
    same_chunk = (gi // C) == (gj // C)
    scan_tri = (jnp.where(same_chunk & (gj <= gi), 1.0, 0.0).astype(BF16),
                jnp.where(same_chunk & (gj >= gi), 1.0, 0.0).astype(BF16))

    r = pr_ref[0].astype(F32)
    k = pk_ref[0].astype(F32)
    v = pv_ref[0].astype(F32)
    lo = pl_ref[0].astype(F32)
    l1 = lo[:, :LANES]
    l2 = lo[:, LANES:]
    g = _bdot(_sigmoid(l2), gup_ref[...])
    kkr = k * kk_ref[...]
    kk = kkr * lax.rsqrt(jnp.maximum(_split_dot(kkr * kkr, seg_ones), 1e-12))
    v_s[...] = v
    k_s[...] = k
    kk_s[...] = kk
    r_s[...] = r
    tanh_l1 = jnp.tanh(l1)

    for z in range(2):
        wl = w0_ref[z:z + 1, :] + _bdot(tanh_l1, wup_ref[z])
        half_c = 0.5 * math.exp(-0.5) * math.log2(math.e)
        logw = -half_c * jnp.tanh(0.5 * wl) - half_c
        al_s[z] = a0_ref[z:z + 1, :] + _bdot(l1, aup_ref[z])
        lw_hi = logw.astype(BF16)
        lw_lo = (logw - lw_hi.astype(F32)).astype(BF16)
        tri = scan_tri[z]
        cum = jnp.concatenate(
            [jnp.dot(tri, lw_hi[g0:g0 + GRP], preferred_element_type=F32)
             + jnp.dot(tri, lw_lo[g0:g0 + GRP], preferred_element_type=F32)
             for g0 in range(0, S, GRP)], axis=0)
        lw_s[z] = logw
        cum_s[z] = cum

    Q = 4 * HEAD_DIM
    qi = lax.broadcasted_iota(jnp.int32, (Q, Q), 0)
    qj = lax.broadcasted_iota(jnp.int32, (Q, Q), 1)
    same_blk = (qi // HEAD_DIM) == (qj // HEAD_DIM)
    ct = lax.broadcasted_iota(jnp.int32, (C, Q), 0)
    cj = lax.broadcasted_iota(jnp.int32, (C, Q), 1)
    dt = ((cj % C) - ct) * jnp.where(cj < 2 * C, 1, -1)
    strict_m = dt < 0
    incl_m = dt <= 0

    def scaled(z, rz):
        a = _sigmoid(al_s[z, rz, :])
        al_s[z, rz, :] = a
        kkc = kk_s[rz, :]
        cum = cum_s[z, rz, :]
        e_out = jnp.exp2(-cum)
        kd = k_s[rz, :] * (1.0 + (a - 1.0) * ka_ref[...])
        return (kkc * jnp.exp2(cum - lw_s[z, rz, :]), r_s[rz, :] * jnp.exp2(cum), kd * e_out, (a * kkc) * e_out)

    def bd(a):
        return jnp.where(same_blk, jnp.concatenate([a.astype(BF16)] * 4, axis=0), 0.0)

    def dot(a, b):
        return jnp.dot(a.astype(BF16), b, preferred_element_type=F32)

    def dot_nt(a, b):
        return lax.dot_general(a.astype(BF16), b, (((1,), (1,)), ((), ())), preferred_element_type=F32)

    U = min(RWKV_UNROLL, nchunk)

    def chunk_of(g, u, z):
        c = g * U + u
        return c if z == 0 else nchunk - 1 - c

    def rows_of(c):
        return pl.ds(c * C, C) if isinstance(c, int) else pl.ds(pl.multiple_of(c * C, C), C)

    def phase1(g, between=lambda k: None):
        ur = range(U)
        cs = [(chunk_of(g, u, 0), chunk_of(g, u, 1)) for u in ur]
        rws = [(rows_of(c0_), rows_of(c1_)) for c0_, c1_ in cs]
        parts = [[scaled(z, r_[z]) for z in range(2)] for r_ in rws]
        both = lambda u, i: jnp.concatenate([parts[u][0][i], parts[u][1][i]], axis=1)
        kt = [both(u, 0) for u in ur]
        rt = [both(u, 1) for u in ur]
        vbd = [bd(jnp.concatenate([v_s[r_[0], :], v_s[r_[1], :]], axis=1)) for r_ in rws]
        lhs = [jnp.concatenate([kt[u], rt[u]], axis=0) for u in ur]
        gb = [dot_nt(lhs[u], bd(both(u, 3))) for u in ur]
        gk = [dot_nt(lhs[u], bd(both(u, 2))) for u in ur]
        between(0)
        a_rb = [jnp.where(incl_m, gb[u][C:], 0.0) for u in ur]
        a_kr = [jnp.concatenate([jnp.where(strict_m, gk[u][:C], 0.0),
                                 jnp.where(incl_m, gk[u][C:], 0.0)], axis=0) for u in ur]
        yy = [jnp.where(strict_m, -gb[u][:C], 0.0) for u in ur]
        x = list(yy)
        p = [dot(yy[u], bd(yy[u])) for u in ur]
        yy = p
        between(1)
        for step in range(1, 6):
            last = step == 5
            p = [dot(x[u] if last else jnp.concatenate([yy[u], x[u]], axis=0), bd(yy[u])) for u in ur]
            x = [x[u] + yy[u] + (p[u] if last else p[u][C:]) for u in ur]
            if not last:
                yy = [p[u][:C] for u in ur]
            between(1 + step)
        avr = [dot(a_kr[u], vbd[u]) for u in ur]
        between(7)
        tz = []
        for u in ur:
            av = avr[u][:C]
            zz = jnp.concatenate([kt[u], av], axis=1)
            tz.append(zz + dot(x[u], jnp.concatenate([bd(kt[u]), bd(av)], axis=1)))
        cor = [dot(a_rb[u], jnp.concatenate([bd(tz[u][:, :Q]), bd(tz[u][:, Q:])], axis=1)) for u in ur]
        for u in ur:
            rr = rt[u] - cor[u][:, :Q]
            yl = avr[u][C:] - cor[u][:, Q:]
            for z in range(2):
                zl = slice(z * LANES, (z + 1) * LANES)
                rz = rws[u][z]
                y_s[z, rz, :] = yl[:, zl]
                rr_s[z, rz, :] = rr[:, zl]
                far = C - 1 if z == 0 else 0
                pt = jnp.exp2(cum_s[z, rz, :][far:far + 1, :])
                ends = jnp.concatenate([parts[u][z][3] * pt, parts[u][z][2] * pt], axis=0)
                rhs = jnp.concatenate(
                    [jnp.concatenate([-tz[u][:, zl], -tz[u][:, Q + z * LANES:Q + (z + 1) * LANES]], axis=1),
                     jnp.concatenate([jnp.zeros((C, LANES), F32), v_s[rz, :]], axis=1)], axis=0)
                mn = _bdot(ends.T, rhs)
                m_s[z, cs[u][z]] = jnp.where(eye, jnp.broadcast_to(pt, (LANES, LANES)), 0.0) \
                    + jnp.where(same_head, mn[:, :LANES], 0.0)
                n_s[z, cs[u][z]] = jnp.where(same_head, mn[:, LANES:], 0.0)

    def carry_step(g, u, hs):
        for z in range(2):
            c = chunk_of(g, u, z)
            rows = rows_of(c)
            y_s[z, rows, :] = y_s[z, rows, :] + dot(rr_s[z, rows, :], hs[z])
            hs[z] = (dot(m_s[z, c], hs[z]) + n_s[z, c]).astype(BF16)

    def phase2(g, hs):
        hs = list(hs)
        for u in range(U):
            carry_step(g, u, hs)
        return tuple(hs)

    def lagged(g, hs):
        hs = list(hs)
        steps_per_gap = -(-U // 8)

        def between(k):
            for u in range(k * steps_per_gap, min(U, (k + 1) * steps_per_gap)):
                carry_step(g - 1, u, hs)

        phase1(g, between)
        return tuple(hs)

    ngroup = nchunk // U
    h0 = jnp.zeros((LANES, LANES), BF16)
    phase1(0)
    hs = lax.fori_loop(1, ngroup, lagged, (h0, h0))
    phase2(ngroup - 1, hs)

    y = y_s[0] + y_s[1]
    inv_n = 1.0 / HEAD_DIM
    mu_y = _split_dot(y, seg_ones) * inv_n
    yc = y - mu_y
    var_y = _split_dot(yc * yc, seg_ones) * inv_n
    yn = yc * lax.rsqrt(var_y + RWKV_GN_EPS) * gng_ref[...] + gnb_ref[...]
    kd_sum = k_s[...] * (2.0 + (al_s[0] + al_s[1] - 2.0) * ka_ref[...])
    bonus = _split_dot(r_s[...] * kd_sum * rk_ref[...], seg_ones) * v_s[...]
    o_ref[0] = ((yn + bonus) * g).astype(o_ref.dtype)


def _rwkv_branch(p, w0, w_up, a0, a_up, g_up, k_k, k_a, r_k, gn_g, gn_b, out_dtype):
    B, S, _ = p.shape
    npair = RWKV_DIM // LANES
    zpad = jnp.zeros((2, LANES - DECAY_LORA, RWKV_DIM), F32)
    wup_p = jnp.concatenate([w_up, zpad], axis=1).astype(BF16)
    aup_p = jnp.concatenate([zpad, a_up], axis=1).astype(BF16)
    row = lambda a: a.reshape(1, RWKV_DIM)
    lora_blk = 3 * RWKV_DIM // (2 * LANES)

    def pspec(off):
        return pl.BlockSpec((1, S, LANES), lambda b, j: (b, 0, off + j))

    def vspec(rows, off=0):
        return pl.BlockSpec((rows, LANES), lambda b, j: (0, off + j))

    scratch = [pltpu.VMEM((2, S, LANES), F32) for _ in range(4)] + [
        pltpu.VMEM((S, LANES), F32) for _ in range(4)] + [pltpu.VMEM((2, S, LANES), F32),
        pltpu.VMEM((2, S // RWKV_CHUNK, LANES, LANES), F32),
        pltpu.VMEM((2, S // RWKV_CHUNK, LANES, LANES), F32)]
    return pl.pallas_call(
        functools.partial(_rwkv_kernel, seq=S),
        grid=(B, npair),
        in_specs=[pspec(0), pspec(npair), pspec(2 * npair),
                  pl.BlockSpec((1, S, 2 * LANES), lambda b, j: (b, 0, lora_blk)),
                  vspec(2),
                  pl.BlockSpec((2, LANES, LANES), lambda b, j: (0, 0, j)),
                  vspec(2),
                  pl.BlockSpec((2, LANES, LANES), lambda b, j: (0, 0, j)),
                  vspec(GATE_LORA), vspec(1), vspec(1), vspec(1), vspec(1), vspec(1)],
        out_specs=pl.BlockSpec((1, S, LANES), lambda b, j: (b, 0, j)),
        out_shape=jax.ShapeDtypeStruct((B, S, RWKV_DIM), out_dtype),
        scratch_shapes=scratch,
        compiler_params=_cparams("arbitrary", "arbitrary"),
        name="rwkv7_scan",
    )(p, p, p, p, w0, wup_p, a0, aup_p, g_up.astype(BF16),
      row(k_k), row(k_a), row(r_k), row(gn_g), row(gn_b))


def _win_kernel(q_ref, k_ref, v_ref, biast_ref, sink_ref, o_ref, k4_s, vt_s, *, seq):
    S = seq
    G = WIN_GROUP
    NB = S // BLOCK
    GW = G * HEAD_DIM
    hk = pl.program_id(1)
    odd = (hk % 2) == 1
    lane = lax.broadcasted_iota(jnp.int32, (1, LANES), 1)

    def own_head_twice(full):
        mine_first = jnp.where(odd, pltpu.roll(full, HEAD_DIM, 1), full)
        return jnp.where(lane < HEAD_DIM, mine_first, pltpu.roll(mine_first, HEAD_DIM, 1))

    k2 = own_head_twice(k_ref[0].astype(F32)).astype(BF16)
    v2 = own_head_twice(v_ref[0].astype(F32))
    zero_k = jnp.zeros((BLOCK, GW), BF16)
    zero_v = jnp.zeros((HEAD_DIM, BLOCK), BF16)
    k4_s[0] = zero_k
    k4_s[NB + 1] = zero_k
    vt_s[0] = zero_v
    vt_s[NB + 1] = zero_v
    for blk in range(NB):
        rows = slice(blk * BLOCK, (blk + 1) * BLOCK)
        k4_s[blk + 1] = jnp.concatenate([k2[rows], k2[rows]], axis=1)
        vt_s[blk + 1] = v2[rows].T[:HEAD_DIM].astype(BF16)
    log2e = math.log2(math.e)
    bias_t = jnp.concatenate([biast_ref[gi] for gi in range(G)], axis=1) * log2e
    krow = lax.broadcasted_iota(jnp.int32, (3 * BLOCK, G * BLOCK), 0)
    qcol = lax.broadcasted_iota(jnp.int32, (3 * BLOCK, G * BLOCK), 1) % BLOCK
    bias_t = jnp.where(jnp.abs(krow - BLOCK - qcol) <= WINDOW, bias_t, NEG_INF)
    cgrp = lax.broadcasted_iota(jnp.int32, (1, G * BLOCK), 1) // BLOCK
    sink_row = jnp.zeros((1, G * BLOCK), F32)
    for gi in range(G):
        sink_row = jnp.where(cgrp == gi, sink_ref[hk * G + gi] * log2e, sink_row)
    lane_q = lax.broadcasted_iota(jnp.int32, (1, GW), 1) // HEAD_DIM
    U = min(WIN_UNROLL, NB)

    def body(it, carry):
        ns = [it * U + u for u in range(U)]
        sts = []
        for n in ns:
            qb = q_ref[0, pl.ds(pl.multiple_of(n * BLOCK, BLOCK), BLOCK), :].astype(F32) \
                * (log2e * HEAD_DIM ** -0.5)
            qm = jnp.concatenate([jnp.where(lane_q == gi, qb, 0.0) for gi in range(G)], axis=0).astype(BF16)
            kband = jnp.concatenate([k4_s[n], k4_s[n + 1], k4_s[n + 2]], axis=0)
            sts.append(lax.dot_general(kband, qm, (((1,), (1,)), ((), ())), preferred_element_type=F32))
        es, dens = [], []
        for u, n in enumerate(ns):
            st = sts[u] + bias_t
            st = jnp.concatenate([st[:BLOCK] + jnp.where(n == 0, NEG_INF, 0.0), st[BLOCK:2 * BLOCK],
                                  st[2 * BLOCK:] + jnp.where(n == NB - 1, NEG_INF, 0.0)], axis=0)
            m = jnp.maximum(jnp.max(st, axis=0, keepdims=True), sink_row)
            e = jnp.exp2(st - m)
            es.append(e.astype(BF16))
            dens.append(jnp.sum(e, axis=0, keepdims=True) + jnp.exp2(sink_row - m))
        for u, n in enumerate(ns):
            vband = jnp.concatenate([vt_s[n], vt_s[n + 1], vt_s[n + 2]], axis=1)
            acc = jnp.dot(vband, es[u], preferred_element_type=F32) / dens[u]
            o_t = jnp.concatenate([acc[:, gi * BLOCK:(gi + 1) * BLOCK] for gi in range(G)], axis=0)
            o_ref[0, pl.ds(pl.multiple_of(n * BLOCK, BLOCK), BLOCK), :] = o_t.T.astype(o_ref.dtype)
        return carry

    lax.fori_loop(0, NB // U, body, 0)


def _window_branch(p, bias_tiles_t, sink, out_dtype):
    B, S, _ = p.shape
    gw = WIN_GROUP * HEAD_DIM
    nb = S // BLOCK
    assert nb % min(WIN_UNROLL, nb) == 0
    qoff = OFF_WIN // gw
    koff = (OFF_WIN + WIN_Q_COLS) // LANES
    voff = (OFF_WIN + WIN_Q_COLS + WIN_KV_COLS) // LANES
    return pl.pallas_call(
        functools.partial(_win_kernel, seq=S),
        grid=(B, WIN_KV_HEADS),
        in_specs=[pl.BlockSpec((1, S, gw), lambda b, h: (b, 0, qoff + h)),
                  pl.BlockSpec((1, S, LANES), lambda b, h: (b, 0, koff + h // 2)),
                  pl.BlockSpec((1, S, LANES), lambda b, h: (b, 0, voff + h // 2)),
                  pl.BlockSpec((WIN_GROUP, 3 * BLOCK, BLOCK), lambda b, h: (h, 0, 0)),
                  pl.BlockSpec(memory_space=pltpu.SMEM)],
        out_specs=pl.BlockSpec((1, S, gw), lambda b, h: (b, 0, h)),
        out_shape=jax.ShapeDtypeStruct((B, S, WIN_Q_COLS), out_dtype),
        scratch_shapes=[pltpu.VMEM((nb + 2, BLOCK, gw), BF16),
                        pltpu.VMEM((nb + 2, HEAD_DIM, BLOCK), BF16)],
        compiler_params=_cparams("arbitrary", "arbitrary"),
        name="window_attn",
    )(p, p, p, bias_tiles_t, sink)


def _diff_kernel(q_ref, k_ref, v_ref, bandt_ref, tab_ref, lam_ref, g_ref, o_ref, ka_s, vt_s, *,
                 seq, lambda_init):
    S = seq
    NB = S // BLOCK
    CB = min(DIFF_CHUNK_BLOCKS, NB)
    h = pl.program_id(1)
    it = pl.program_id(2)
    halves = q_ref.shape[1] // BLOCK
    log2e = math.log2(math.e)
    lane = lax.broadcasted_iota(jnp.int32, (1, LANES), 1)
    c0 = lane < HEAD_DIM

    @pl.when(it == 0)
    def _():
        jblk = lax.broadcasted_iota(jnp.int32, (S, LANES), 0) // BLOCK
        lj = lax.broadcasted_iota(jnp.int32, (S, LANES), 1)
        ka_s[:, :LANES] = k_ref[0].astype(BF16)
        ka_s[:, LANES:] = jnp.where((lj < 2 * NB) & (lj % NB == jblk), 1.0, 0.0).astype(BF16)
        for blk in range(NB):
            vt_s[blk] = v_ref[0, blk * BLOCK:(blk + 1) * BLOCK, :].astype(F32).T.astype(BF16)

    far_l = tab_ref[WIN_Q_HEADS + h, NUM_BUCKETS // 2 - 1] * log2e
    far_r = tab_ref[WIN_Q_HEADS + h, NUM_BUCKETS - 1] * log2e
    band_t = bandt_ref[0] * log2e
    lam = lam_ref[...].astype(F32)
    lam_full = (jnp.exp(jnp.sum(lam[0:1] * lam[1:2], axis=-1, keepdims=True))
                - jnp.exp(jnp.sum(lam[2:3] * lam[3:4], axis=-1, keepdims=True)) + lambda_init)
    ibs = [it * halves + hf for hf in range(halves)]
    zblk = jnp.zeros((BLOCK, BLOCK), F32)
    qq, bands = [], []
    for hf in range(halves):
        q = q_ref[0, hf * BLOCK:(hf + 1) * BLOCK, :].astype(F32) * (log2e * HEAD_DIM ** -0.5)
        kb = lane % NB
        val = jnp.where(kb < ibs[hf] - 1, far_l, jnp.where(kb > ibs[hf] + 1, far_r, 0.0))
        val = jnp.where(lane < 2 * NB, val, 0.0)
        hi = val.astype(BF16)
        lo = (val - hi.astype(F32)).astype(BF16)
        bcols = jnp.broadcast_to(jnp.where(lane < NB, hi, lo), (BLOCK, LANES))
        qq.append(jnp.concatenate(
            [jnp.concatenate([jnp.where(c0, q, 0.0).astype(BF16), bcols], axis=1),
             jnp.concatenate([jnp.where(c0, 0.0, q).astype(BF16), bcols], axis=1)], axis=0))
        tiles = [jnp.where(ibs[hf] > 0, band_t[:BLOCK], 0.0), band_t[BLOCK:2 * BLOCK],
                 jnp.where(ibs[hf] < NB - 1, band_t[2 * BLOCK:], 0.0)]
        col = jnp.concatenate([zblk] * hf + tiles + [zblk] * (CB - 3 - hf), axis=0)
        bands.append(jnp.concatenate([col, col], axis=1))

    m_run, l_run, acc = [None] * halves, [None] * halves, [None] * halves
    for r in range(NB // CB):
        kbs = [lax.rem(it * halves - 1 + CB * r + u + NB, NB) for u in range(CB)]
        kc = jnp.concatenate([ka_s[pl.ds(pl.multiple_of(kb * BLOCK, BLOCK), BLOCK), :] for kb in kbs], axis=0)
        vct = jnp.concatenate([vt_s[kb] for kb in kbs], axis=1)
        s = [lax.dot_general(kc, qq[hf], (((1,), (1,)), ((), ())), preferred_element_type=F32)
             for hf in range(halves)]
        if r == 0:
            s = [s[hf] + bands[hf] for hf in range(halves)]
        m_loc = [jnp.max(s[hf], axis=0, keepdims=True) for hf in range(halves)]
        for hf in range(halves):
            if r == 0:
                m_new = m_loc[hf]
                e = jnp.exp2(s[hf] - m_new)
                l_run[hf] = jnp.sum(e, axis=0, keepdims=True)
                acc[hf] = jnp.dot(vct, e.astype(BF16), preferred_element_type=F32)
            else:
                m_new = jnp.maximum(m_run[hf], m_loc[hf])
                alpha = jnp.exp2(m_run[hf] - m_new)
                e = jnp.exp2(s[hf] - m_new)
                l_run[hf] = alpha * l_run[hf] + jnp.sum(e, axis=0, keepdims=True)
                acc[hf] = alpha * acc[hf] + jnp.dot(vct, e.astype(BF16), preferred_element_type=F32)
            m_run[hf] = m_new
    for hf in range(halves):
        rinv = 1.0 / l_run[hf]
        o_t = acc[hf][:, :BLOCK] * rinv[:, :BLOCK] - acc[hf][:, BLOCK:] * (lam_full * rinv[:, BLOCK:])
        o = o_t.T
        o = o * lax.rsqrt(jnp.mean(o * o, axis=-1, keepdims=True) + 1e-5) * g_ref[...] * (1.0 - lambda_init)
        o_ref[0, hf * BLOCK:(hf + 1) * BLOCK, :] = o.astype(o_ref.dtype)


def _diff_branch(p, band_tiles, rel_bias_t, lam, subln_g, lambda_init, out_dtype):
    B, S, _ = p.shape
    qoff = OFF_DIFF // LANES
    koff = (OFF_DIFF + DIFF_QK_COLS) // LANES
    voff = (OFF_DIFF + 2 * DIFF_QK_COLS) // LANES
    nb = S // BLOCK
    cb = min(DIFF_CHUNK_BLOCKS, nb)
    tq = DIFF_Q_TILE
    while tq > (cb - 2) * BLOCK or S % tq:
        tq //= 2
    assert 2 * nb <= LANES and tq >= BLOCK and nb % cb == 0 and nb >= 4
    return pl.pallas_call(
        functools.partial(_diff_kernel, seq=S, lambda_init=lambda_init),
        grid=(B, DIFF_HEADS, S // tq),
        in_specs=[pl.BlockSpec((1, tq, LANES), lambda b, h, i: (b, i, qoff + h)),
                  pl.BlockSpec((1, S, LANES), lambda b, h, i: (b, 0, koff + h)),
                  pl.BlockSpec((1, S, LANES), lambda b, h, i: (b, 0, voff + h)),
                  pl.BlockSpec((1, 3 * BLOCK, BLOCK), lambda b, h, i: (WIN_Q_HEADS + h, 0, 0)),
                  pl.BlockSpec(memory_space=pltpu.SMEM),
                  pl.BlockSpec((4, HEAD_DIM), lambda b, h, i: (0, 0)),
                  pl.BlockSpec((1, DIFF_V_DIM), lambda b, h, i: (0, 0))],
        out_specs=pl.BlockSpec((1, tq, LANES), lambda b, h, i: (b, i, h)),
        out_shape=jax.ShapeDtypeStruct((B, S, DIFF_V_COLS), out_dtype),
        scratch_shapes=[pltpu.VMEM((S, 2 * LANES), BF16),
                        pltpu.VMEM((S // BLOCK, DIFF_V_DIM, BLOCK), BF16)],
        compiler_params=_cparams("arbitrary", "arbitrary", "arbitrary"),
        name="diff_attn",
    )(p, p, p, jnp.swapaxes(band_tiles, 1, 2), rel_bias_t, lam, subln_g.reshape(1, DIFF_V_DIM))


def _mem_kernel(q_ref, mem_ref, wkv_ref, o_ref):
    kv = _bdot(mem_ref[0], wkv_ref[...])
    kmat = kv[:, :MEM_WIDTH]
    vmat = kv[:, MEM_WIDTH:]
    q = q_ref[0].astype(F32) * (HEAD_DIM ** -0.5)
    lane = lax.broadcasted_iota(jnp.int32, (1, MEM_WIDTH), 1) // HEAD_DIM
    out = jnp.zeros(q.shape, F32)
    for hh in range(MEM_HEADS):
        s = _bdot_nt(jnp.where(lane == hh, q, 0.0), kmat)
        m = jnp.max(s, axis=-1, keepdims=True)
        e = jnp.exp(s - m)
        a = e / jnp.sum(e, axis=-1, keepdims=True)
        out = jnp.where(lane == hh, _bdot(a, vmat), out)
    o_ref[0] = out.astype(o_ref.dtype)


def _memory_branch(p, mem, w_kv, out_dtype):
    B, S, _ = p.shape
    nm, d = mem.shape[1], mem.shape[2]
    return pl.pallas_call(
        _mem_kernel,
        grid=(B,),
        in_specs=[pl.BlockSpec((1, S, MEM_WIDTH), lambda b: (b, 0, OFF_MEM // MEM_WIDTH)),
                  pl.BlockSpec((1, nm, d), lambda b: (b, 0, 0)),
                  pl.BlockSpec((d, 2 * MEM_WIDTH), lambda b: (0, 0))],
        out_specs=pl.BlockSpec((1, S, MEM_WIDTH), lambda b: (b, 0, 0)),
        out_shape=jax.ShapeDtypeStruct((B, S, MEM_WIDTH), out_dtype),
        compiler_params=_cparams("arbitrary"),
        name="memory_attn",
    )(p, mem, w_kv)


def _merge_kernel(x_ref, yr_ref, yw_ref, yd_ref, ym_ref, pg_ref, wb_ref, wo_ref, g_ref, b_ref,
                  o_ref, *, d_model, alpha):
    D = d_model
    ys = (yr_ref[...], yw_ref[...], yd_ref[...], ym_ref[...])
    off = 0
    merged = jnp.zeros((x_ref.shape[0], D), F32)
    for bi, yb in enumerate(ys):
        rows = yb.shape[1]
        proj = _bdot(yb, wb_ref[off:off + rows, :])
        merged = merged + _sigmoid(pg_ref[:, bi * D:(bi + 1) * D].astype(F32)) * proj
        off += rows
    z = alpha * x_ref[...] + _bdot(merged, wo_ref[...])
    o_ref[...] = _layer_norm(z, g_ref[...], b_ref[...])


def _merge(x2, y_rwkv, y_win, y_diff, y_mem, p2, w_branch, w_out, ln_g, ln_b, alpha, tm):
    T, D = x2.shape
    gate_blk = OFF_GATE // (N_BRANCHES * D)
    full = lambda a: pl.BlockSpec(a.shape, lambda i: (0, 0))
    tile = lambda w: pl.BlockSpec((tm, w), lambda i: (i, 0))
    return pl.pallas_call(
        functools.partial(_merge_kernel, d_model=D, alpha=alpha),
        grid=(T // tm,),
        in_specs=[tile(D), tile(y_rwkv.shape[1]), tile(y_win.shape[1]), tile(y_diff.shape[1]),
                  tile(y_mem.shape[1]),
                  pl.BlockSpec((tm, N_BRANCHES * D), lambda i: (i, gate_blk)),
                  full(w_branch), full(w_out),
                  pl.BlockSpec((1, D), lambda i: (0, 0)), pl.BlockSpec((1, D), lambda i: (0, 0))],
        out_specs=tile(D),
        out_shape=jax.ShapeDtypeStruct((T, D), F32),
        compiler_params=_cparams("arbitrary"),
        name="merge_out_ln",
    )(x2, y_rwkv, y_win, y_diff, y_mem, p2, w_branch, w_out, ln_g.reshape(1, D), ln_b.reshape(1, D))


def _route_kernel(x_ref, wr_ref, xe_ref, slot_ref, gate_ref, xb_s, lg_s, *, seq, cap):
    S = seq
    E = N_EXPERTS
    TR = min(S, 512)
    wr = wr_ref[...]
    w_hi = wr.astype(BF16)
    w_md = (wr - w_hi.astype(F32)).astype(BF16)
    dot = lambda a, b: jnp.dot(a, b, preferred_element_type=F32)

    def logit_tile(t, carry):
        rows = pl.ds(pl.multiple_of(t * TR, TR), TR)
        x = x_ref[0, rows, :]
        x_hi = x.astype(BF16)
        x_r1 = x - x_hi.astype(F32)
        x_md = x_r1.astype(BF16)
        x_lo = (x_r1 - x_md.astype(F32)).astype(BF16)
        xb_s[rows, :] = x_hi
        lg_s[rows, :] = (dot(x_hi, w_hi) + (dot(x_hi, w_md) + dot(x_md, w_hi))
                         + (dot(x_md, w_md) + dot(x_lo, w_hi)))
        return carry

    lax.fori_loop(0, S // TR, logit_tile, 0)
    col = lax.broadcasted_iota(jnp.int32, (1, LANES), 1)
    logits = jnp.where(col < E, lg_s[...], NEG_INF)
    mx = jnp.max(logits, axis=-1, keepdims=True)
    ex = jnp.exp(logits - mx)
    aff = ex / jnp.sum(ex, axis=-1, keepdims=True)
    aff_t = aff.T[0:E, :]
    bits = pltpu.bitcast(aff_t, jnp.int32)

    def bis(_, carry):
        lo_b, step = carry
        cand = lo_b + step
        cnt = jnp.sum(jnp.where(bits >= cand, 1.0, 0.0), axis=-1, keepdims=True)
        return jnp.where(cnt >= cap, cand, lo_b), step // 2

    lo0 = jnp.zeros((E, 1), jnp.int32)
    thr, _ = lax.fori_loop(0, 31, bis, (lo0, jnp.full((E, 1), 1 << 30, jnp.int32)))
    above = bits > thr
    tie = bits == thr
    n_above = jnp.sum(jnp.where(above, 1.0, 0.0), axis=-1, keepdims=True)
    PB = min(S, 256)
    tri = (lax.broadcasted_iota(jnp.int32, (PB, PB), 0)
           < lax.broadcasted_iota(jnp.int32, (PB, PB), 1)).astype(BF16)

    def prefix_count(mask_f):
        parts, run = [], jnp.zeros((E, 1), F32)
        for blk in range(S // PB):
            seg = mask_f[:, blk * PB:(blk + 1) * PB]
            parts.append(jnp.dot(seg.astype(BF16), tri, preferred_element_type=F32) + run)
            run = run + jnp.sum(seg, axis=-1, keepdims=True)
        return jnp.concatenate(parts, axis=1)

    tie_rank = prefix_count(jnp.where(tie, 1.0, 0.0))
    sel = above | (tie & (tie_rank < cap - n_above))
    pos = prefix_count(jnp.where(sel, 1.0, 0.0))
    slot = jnp.where(sel, pos, -1.0)
    slot_pad = jnp.concatenate([slot, jnp.full((LANES - E, S), -1.0, F32)], axis=0)
    slot_ref[0] = slot_pad.T
    ci = lax.broadcasted_iota(jnp.int32, (cap, S), 0).astype(F32)
    x_hi = xb_s[...]
    for e in range(E):
        onehot = slot[e:e + 1, :] == ci
        xe_ref[e, 0] = jnp.dot(jnp.where(onehot, 1.0, 0.0).astype(BF16), x_hi,
                               preferred_element_type=F32).astype(xe_ref.dtype)
        gsel = jnp.sum(jnp.where(onehot, aff_t[e:e + 1, :], 0.0), axis=-1, keepdims=True)
        gate_ref[e, 0] = jnp.broadcast_to(gsel, (cap, LANES))


def _route(x3, router_pad, cap):
    B, S, D = x3.shape
    E = N_EXPERTS
    return pl.pallas_call(
        functools.partial(_route_kernel, seq=S, cap=cap),
        grid=(B,),
        in_specs=[pl.BlockSpec((1, S, D), lambda b: (b, 0, 0)),
                  pl.BlockSpec((D, LANES), lambda b: (0, 0))],
        out_specs=[pl.BlockSpec((E, 1, cap, D), lambda b: (0, b, 0, 0)),
                   pl.BlockSpec((1, S, LANES), lambda b: (b, 0, 0)),
                   pl.BlockSpec((E, 1, cap, LANES), lambda b: (0, b, 0, 0))],
        out_shape=[jax.ShapeDtypeStruct((E, B, cap, D), BF16),
                   jax.ShapeDtypeStruct((B, S, LANES), F32),
                   jax.ShapeDtypeStruct((E, B, cap, LANES), F32)],
        scratch_shapes=[pltpu.VMEM((S, D), BF16), pltpu.VMEM((S, LANES), F32)],
        compiler_params=_cparams("arbitrary"),
        name="moe_route_gather",
    )(x3, router_pad)


def _ffn_kernel(xe_ref, g_ref, wg_ref, wu_ref, wd_ref, ye_ref, acc_s):
    f = pl.program_id(1)
    nb, cap, D = xe_ref.shape[1], xe_ref.shape[2], xe_ref.shape[3]

    @pl.when(f == 0)
    def _():
        acc_s[...] = jnp.zeros(acc_s.shape, F32)

    xe = xe_ref[0].reshape(nb * cap, D)
    hg = jnp.dot(xe, wg_ref[0, 0].astype(BF16), preferred_element_type=F32)
    hu = jnp.dot(xe, wu_ref[0, 0].astype(BF16), preferred_element_type=F32)
    hid = (hg * _sigmoid(hg)) * hu
    acc_s[...] = acc_s[...] + _bdot(hid, wd_ref[0, 0])

    @pl.when(f == pl.num_programs(1) - 1)
    def _():
        gate = g_ref[0].reshape(nb * cap, LANES)[:, 0:1]
        ye_ref[0] = (acc_s[...] * gate).reshape(nb, cap, D).astype(ye_ref.dtype)


def _expert_ffn(xe, gate, w_gate, w_up, w_down, layer, tf):
    E, B, cap, D = xe.shape
    F = w_gate.shape[-1]
    return pl.pallas_call(
        _ffn_kernel,
        grid=(E, F // tf),
        in_specs=[pl.BlockSpec((1, B, cap, D), lambda e, f: (e, 0, 0, 0)),
                  pl.BlockSpec((1, B, cap, LANES), lambda e, f: (e, 0, 0, 0)),
                  pl.BlockSpec((1, 1, D, tf), lambda e, f: (layer, e, 0, f)),
                  pl.BlockSpec((1, 1, D, tf), lambda e, f: (layer, e, 0, f)),
                  pl.BlockSpec((1, 1, tf, D), lambda e, f: (layer, e, f, 0))],
        out_specs=pl.BlockSpec((1, B, cap, D), lambda e, f: (e, 0, 0, 0)),
        out_shape=jax.ShapeDtypeStruct((E, B, cap, D), BF16),
        scratch_shapes=[pltpu.VMEM((B * cap, D), F32)],
        compiler_params=_cparams("arbitrary", "arbitrary"),
        name="moe_expert_ffn",
    )(xe, gate, w_gate, w_up, w_down)


def _scatter_kernel(x_ref, slot_ref, ye_ref, g_ref, b_ref, o_ref, ob_ref, *, cap, alpha):
    ts = x_ref.shape[1]
    slot = slot_ref[0]
    ci = lax.broadcasted_iota(jnp.int32, (ts, cap), 1).astype(F32)
    acc = jnp.zeros((ts, x_ref.shape[2]), F32)
    for e in range(N_EXPERTS):
        onehot = slot[:, e:e + 1] == ci
        acc = acc + jnp.dot(jnp.where(onehot, 1.0, 0.0).astype(BF16), ye_ref[e, 0],
                            preferred_element_type=F32)
    out = _layer_norm(alpha * x_ref[0] + acc, g_ref[...], b_ref[...])
    o_ref[0] = out
    ob_ref[0] = out.astype(ob_ref.dtype)


def _scatter_ln(x3, slot, ye, ln_g, ln_b, alpha, ts):
    B, S, D = x3.shape
    E, _, cap, _ = ye.shape
    return pl.pallas_call(
        functools.partial(_scatter_kernel, cap=cap, alpha=alpha),
        grid=(B, S // ts),
        in_specs=[pl.BlockSpec((1, ts, D), lambda b, i: (b, i, 0)),
                  pl.BlockSpec((1, ts, LANES), lambda b, i: (b, i, 0)),
                  pl.BlockSpec((E, 1, cap, D), lambda b, i: (0, b, 0, 0)),
                  pl.BlockSpec((1, D), lambda b, i: (0, 0)),
                  pl.BlockSpec((1, D), lambda b, i: (0, 0))],
        out_specs=[pl.BlockSpec((1, ts, D), lambda b, i: (b, i, 0)),
                   pl.BlockSpec((1, ts, D), lambda b, i: (b, i, 0))],
        out_shape=[jax.ShapeDtypeStruct((B, S, D), F32), jax.ShapeDtypeStruct((B, S, D), BF16)],
        compiler_params=_cparams("arbitrary", "arbitrary"),
        name="moe_scatter_ln",
    )(x3, slot, ye, ln_g.reshape(1, D), ln_b.reshape(1, D))


def kernel(x, mem, rel_bias, w_in, rwkv_mu, rwkv_w0, rwkv_w_up, rwkv_a0, rwkv_a_up, rwkv_g_up, rwkv_k_k, rwkv_k_a, rwkv_r_k, rwkv_gn_g, rwkv_gn_b, win_sink, diff_lambda, diff_subln_g, mem_w_kv, w_branch, w_out, ln1_g, ln1_b, router, exp_w_gate, exp_w_up, exp_w_down, ln2_g, ln2_b):
    B, S, D = x.shape
    assert D == D_MODEL and w_in.shape[2] == RWKV_COLS + REST_COLS
    depth = w_in.shape[0]
    alpha = (2 * depth) ** 0.25
    cap = CAPACITY_FACTOR * S // N_EXPERTS
    T = B * S
    rel_bias_t = rel_bias.T
    bias_tiles = _bias_tiles(rel_bias_t)
    bias_tiles_t = jnp.swapaxes(bias_tiles, 1, 2)
    act = BF16
    xb = x.astype(BF16)
    for l in range(depth):
        lambda_init = 0.8 - 0.6 * math.exp(-0.3 * l)
        gate0 = w_in.shape[2] - N_BRANCHES * D
        w_rest = jnp.concatenate([w_in[l][:, gate0:], w_in[l][:, RWKV_COLS:gate0]], axis=1).astype(BF16)
        p_rwkv, p2 = _in_proj(xb.reshape(T, D), w_in[l][:, :RWKV_COLS].astype(BF16), w_rest,
                              rwkv_mu[l], S, act)
        p = p2.reshape(B, S, -1)
        y_rwkv = _rwkv_branch(p_rwkv.reshape(B, S, -1), rwkv_w0[l], rwkv_w_up[l], rwkv_a0[l], rwkv_a_up[l],
                              rwkv_g_up[l], rwkv_k_k[l], rwkv_k_a[l], rwkv_r_k[l], rwkv_gn_g[l],
                              rwkv_gn_b[l], act)
        y_win = _window_branch(p, bias_tiles_t, win_sink[l], act)
        y_diff = _diff_branch(p, bias_tiles, rel_bias_t, diff_lambda[l], diff_subln_g[l], lambda_init, act)
        y_mem = _memory_branch(p, mem, mem_w_kv[l].astype(BF16), act)
        x1 = _merge(x.reshape(T, D), y_rwkv.reshape(T, -1), y_win.reshape(T, -1), y_diff.reshape(T, -1),
                    y_mem.reshape(T, -1), p2, w_branch[l].astype(BF16), w_out[l].astype(BF16),
                    ln1_g[l], ln1_b[l], alpha, MERGE_ROWS).reshape(B, S, D)
        router_pad = jnp.concatenate([router[l], jnp.zeros((D, LANES - N_EXPERTS), F32)], axis=1)
        xe, slot, gate = _route(x1, router_pad, cap)
        ye = _expert_ffn(xe, gate, exp_w_gate, exp_w_up, exp_w_down, l, FFN_HIDDEN_TILE)
        x, xb = _scatter_ln(x1, slot, ye, ln2_g[l], ln2_b[l], alpha, SCATTER_ROWS)
    return x
```

```python
import functools
import math

import jax
import jax.numpy as jnp
from jax import lax
from jax.experimental import pallas as pl
from jax.experimental.pallas import tpu as pltpu

F32 = jnp.float32
BF16 = jnp.bfloat16

HEAD_DIM = 64
LANES = 128
RWKV_HEADS = 16
RWKV_DIM = RWKV_HEADS * HEAD_DIM
DECAY_LORA = 64
ICLR_LORA = 64
GATE_LORA = 128
RWKV_GN_EPS = 64e-5
RWKV_COLS = 3 * RWKV_DIM + DECAY_LORA + ICLR_LORA + GATE_LORA
RWKV_CHUNK = 64
RWKV_UNROLL = 8
assert RWKV_CHUNK == HEAD_DIM
WIN_Q_HEADS = 16
WIN_KV_HEADS = 4
WIN_GROUP = WIN_Q_HEADS // WIN_KV_HEADS
WIN_Q_COLS = WIN_Q_HEADS * HEAD_DIM
WIN_KV_COLS = WIN_KV_HEADS * HEAD_DIM
WIN_COLS = WIN_Q_COLS + 2 * WIN_KV_COLS
WINDOW = 128
BLOCK = 128
WIN_UNROLL = 16
DIFF_HEADS = 8
DIFF_V_DIM = 2 * HEAD_DIM
DIFF_QK_COLS = 2 * DIFF_HEADS * HEAD_DIM
DIFF_V_COLS = DIFF_HEADS * DIFF_V_DIM
DIFF_COLS = 2 * DIFF_QK_COLS + DIFF_V_COLS
DIFF_Q_TILE = 8 * BLOCK
DIFF_CHUNK_BLOCKS = 16
MEM_HEADS = 4
MEM_WIDTH = MEM_HEADS * HEAD_DIM
NUM_BUCKETS = 32
N_EXPERTS = 16
CAPACITY_FACTOR = 2
N_BRANCHES = 4
NEG_INF = -1e30
LN_EPS = 1e-5
VMEM_LIMIT = 56 * 1024 * 1024
D_MODEL = 1024
IN_PROJ_TN_SHIFT = 256
IN_PROJ_TN = 1280
MERGE_ROWS = 512
FFN_HIDDEN_TILE = 512
SCATTER_ROWS = 512
GATHER_STACK = 4

OFF_GATE = 0
OFF_WIN = N_BRANCHES * D_MODEL
OFF_DIFF = OFF_WIN + WIN_COLS
OFF_MEM = OFF_DIFF + DIFF_COLS
REST_COLS = OFF_MEM + MEM_WIDTH


def _cparams(*sem):
    return pltpu.CompilerParams(dimension_semantics=sem, vmem_limit_bytes=VMEM_LIMIT)


def _bdot(a, b):
    return jnp.dot(a.astype(BF16), b.astype(BF16), preferred_element_type=F32)


def _bdot_nt(a, b):
    return lax.dot_general(a.astype(BF16), b.astype(BF16), (((1,), (1,)), ((), ())),
                           preferred_element_type=F32)


def _split_dot(a, b_exact):
    hi = a.astype(BF16)
    lo = (a - hi.astype(F32)).astype(BF16)
    return (jnp.dot(hi, b_exact, preferred_element_type=F32)
            + jnp.dot(lo, b_exact, preferred_element_type=F32))


def _sigmoid(x):
    return 0.5 * jnp.tanh(0.5 * x) + 0.5


def _layer_norm(x, g, b):
    mu = jnp.mean(x, axis=-1, keepdims=True)
    xc = x - mu
    var = jnp.mean(xc * xc, axis=-1, keepdims=True)
    return xc * lax.rsqrt(var + LN_EPS) * g + b


def _inproj_shift_kernel(x_ref, w_ref, mu_ref, o_ref):
    acc = jnp.dot(x_ref[...], w_ref[...], preferred_element_type=F32)
    S = acc.shape[0]
    row = lax.broadcasted_iota(jnp.int32, (S, 1), 0)
    prev = jnp.where(row == 0, 0.0, pltpu.roll(acc, 1, 0))
    nxt = jnp.where(row == S - 1, 0.0, pltpu.roll(acc, S - 1, 0))
    out = acc + mu_ref[0:1, :] * (prev - acc) + mu_ref[1:2, :] * (nxt - acc)
    o_ref[...] = out.astype(o_ref.dtype)


def _inproj_plain_kernel(x_ref, w_ref, o_ref):
    o_ref[...] = jnp.dot(x_ref[...], w_ref[...], preferred_element_type=F32).astype(o_ref.dtype)


def _in_proj(x2, w_rwkv, w_rest, mu, seq, out_dtype):
    m, k = x2.shape
    x_spec = pl.BlockSpec((seq, k), lambda i, j: (i, 0))

    def call(kern, w, tn, extra, extra_specs, name):
        n = w.shape[1]
        assert n % tn == 0
        return pl.pallas_call(
            kern,
            grid=(m // seq, n // tn),
            in_specs=[x_spec, pl.BlockSpec((k, tn), lambda i, j: (0, j))] + extra_specs,
            out_specs=pl.BlockSpec((seq, tn), lambda i, j: (i, j)),
            out_shape=jax.ShapeDtypeStruct((m, n), out_dtype),
            compiler_params=_cparams("arbitrary", "arbitrary"),
            name=name,
        )(x2, w, *extra)

    pad = -w_rwkv.shape[1] % IN_PROJ_TN_SHIFT
    w_pad = jnp.concatenate([w_rwkv, jnp.zeros((k, pad), w_rwkv.dtype)], axis=1)
    mu_pad = jnp.concatenate([mu, jnp.zeros((2, pad), mu.dtype)], axis=1)
    p_rwkv = call(_inproj_shift_kernel, w_pad, IN_PROJ_TN_SHIFT, [mu_pad],
                  [pl.BlockSpec((2, IN_PROJ_TN_SHIFT), lambda i, j: (0, j))], "in_proj_shift")
    p_rest = call(_inproj_plain_kernel, w_rest, IN_PROJ_TN, [], [], "in_proj")
    return p_rwkv, p_rest


def _bias_kernel(tab_ref, o_ref):
    h = pl.program_id(0)
    qi = lax.broadcasted_iota(jnp.int32, (BLOCK, 3 * BLOCK), 0)
    kj = lax.broadcasted_iota(jnp.int32, (BLOCK, 3 * BLOCK), 1)
    rel = kj - BLOCK - qi
    n = jnp.abs(rel)
    n2 = n * n
    large = jnp.full_like(n, NUM_BUCKETS // 4)
    for kpow in range(1, 8):
        large = large + jnp.where(n2 >= 64 * (2 ** kpow), 1, 0)
    bucket = jnp.where(rel > 0, NUM_BUCKETS // 2, 0) + jnp.where(n < NUM_BUCKETS // 4, n, large)
    acc = jnp.zeros((BLOCK, 3 * BLOCK), F32)
    for bkt in range(NUM_BUCKETS):
        acc = jnp.where(bucket == bkt, tab_ref[h, bkt], acc)
    o_ref[0] = acc


def _bias_tiles(rel_bias_t):
    nh = rel_bias_t.shape[0]
    return pl.pallas_call(
        _bias_kernel,
        grid=(nh,),
        in_specs=[pl.BlockSpec(memory_space=pltpu.SMEM)],
        out_specs=pl.BlockSpec((1, BLOCK, 3 * BLOCK), lambda h: (h, 0, 0)),
        out_shape=jax.ShapeDtypeStruct((nh, BLOCK, 3 * BLOCK), F32),
        compiler_params=_cparams("arbitrary"),
        name="t5_bias_tiles",
    )(rel_bias_t)


def _rwkv_kernel(pr_ref, pk_ref, pv_ref, pl_ref,
                 w0_ref, wup_ref, a0_ref, aup_ref, gup_ref, kk_ref, ka_ref, rk_ref,
                 gng_ref, gnb_ref, o_ref,
                 al_s, lw_s, cum_s, rr_s, k_s, kk_s, r_s, v_s, y_s, m_s, n_s, *, seq):
    S = seq
    C = RWKV_CHUNK
    nchunk = S // C
    row = lax.broadcasted_iota(jnp.int32, (S, 1), 0)
    lane = lax.broadcasted_iota(jnp.int32, (1, LANES), 1)
    head0 = lane < HEAD_DIM
    ri = lax.broadcasted_iota(jnp.int32, (LANES, LANES), 0)
    ci = lax.broadcasted_iota(jnp.int32, (LANES, LANES), 1)
    same_head = (ri < HEAD_DIM) == (ci < HEAD_DIM)
    eye = ri == ci
    seg_ones = jnp.where(same_head, 1.0, 0.0).astype(BF16)

    GRP = min(S, 4 * C)
    gi = lax.broadcasted_iota(jnp.int32, (GRP, GRP), 0)
    gj = lax.broadcasted_iota(jnp.int32, (GRP, GRP), 1)
    same_chunk = (gi // C) == (gj // C)
    scan_tri = (jnp.where(same_chunk & (gj <= gi), 1.0, 0.0).astype(BF16),
                jnp.where(same_chunk & (gj >= gi), 1.0, 0.0).astype(BF16))

    r = pr_ref[0].astype(F32)
    k = pk_ref[0].astype(F32)
    v = pv_ref[0].astype(F32)
    lo = pl_ref[0].astype(F32)
    l1 = lo[:, :LANES]
    l2 = lo[:, LANES:]
    g = _bdot(_sigmoid(l2), gup_ref[...])
    kkr = k * kk_ref[...]
    kk = kkr * lax.rsqrt(jnp.maximum(_split_dot(kkr * kkr, seg_ones), 1e-12))
    v_s[...] = v
    k_s[...] = k
    kk_s[...] = kk
    r_s[...] = r
    tanh_l1 = jnp.tanh(l1)

    for z in range(2):
        wl = w0_ref[z:z + 1, :] + _bdot(tanh_l1, wup_ref[z])
        half_c = 0.5 * math.exp(-0.5) * math.log2(math.e)
        logw = -half_c * jnp.tanh(0.5 * wl) - half_c
        al_s[z] = a0_ref[z:z + 1, :] + _bdot(l1, aup_ref[z])
        lw_hi = logw.astype(BF16)
        lw_lo = (logw - lw_hi.astype(F32)).astype(BF16)
        tri = scan_tri[z]
        cum = jnp.concatenate(
            [jnp.dot(tri, lw_hi[g0:g0 + GRP], preferred_element_type=F32)
             + jnp.dot(tri, lw_lo[g0:g0 + GRP], preferred_element_type=F32)
             for g0 in range(0, S, GRP)], axis=0)
        lw_s[z] = logw
        cum_s[z] = cum

    Q = 4 * HEAD_DIM
    qi = lax.broadcasted_iota(jnp.int32, (Q, Q), 0)
    qj = lax.broadcasted_iota(jnp.int32, (Q, Q), 1)
    same_blk = (qi // HEAD_DIM) == (qj // HEAD_DIM)
    ct = lax.broadcasted_iota(jnp.int32, (C, Q), 0)
    cj = lax.broadcasted_iota(jnp.int32, (C, Q), 1)
    dt = ((cj % C) - ct) * jnp.where(cj < 2 * C, 1, -1)
    strict_m = dt < 0
    incl_m = dt <= 0

    def scaled(z, rz):
        a = _sigmoid(al_s[z, rz, :])
        al_s[z, rz, :] = a
        kkc = kk_s[rz, :]
        cum = cum_s[z, rz, :]
        e_out = jnp.exp2(-cum)
        kd = k_s[rz, :] * (1.0 + (a - 1.0) * ka_ref[...])
        return (kkc * jnp.exp2(cum - lw_s[z, rz, :]), r_s[rz, :] * jnp.exp2(cum), kd * e_out, (a * kkc) * e_out)

    def bd(a):
        return jnp.where(same_blk, jnp.concatenate([a.astype(BF16)] * 4, axis=0), 0.0)

    def dot(a, b):
        return jnp.dot(a.astype(BF16), b, preferred_element_type=F32)

    def dot_nt(a, b):
        return lax.dot_general(a.astype(BF16), b, (((1,), (1,)), ((), ())), preferred_element_type=F32)

    U = min(RWKV_UNROLL, nchunk)

    def chunk_of(g, u, z):
        c = g * U + u
        return c if z == 0 else nchunk - 1 - c

    def rows_of(c):
        return pl.ds(c * C, C) if isinstance(c, int) else pl.ds(pl.multiple_of(c * C, C), C)

    def phase1(g, between=lambda k: None):
        ur = range(U)
        cs = [(chunk_of(g, u, 0), chunk_of(g, u, 1)) for u in ur]
        rws = [(rows_of(c0_), rows_of(c1_)) for c0_, c1_ in cs]
        parts = [[scaled(z, r_[z]) for z in range(2)] for r_ in rws]
        both = lambda u, i: jnp.concatenate([parts[u][0][i], parts[u][1][i]], axis=1)
        kt = [both(u, 0) for u in ur]
        rt = [both(u, 1) for u in ur]
        vbd = [bd(jnp.concatenate([v_s[r_[0], :], v_s[r_[1], :]], axis=1)) for r_ in rws]
        lhs = [jnp.concatenate([kt[u], rt[u]], axis=0) for u in ur]
        gb = [dot_nt(lhs[u], bd(both(u, 3))) for u in ur]
        gk = [dot_nt(lhs[u], bd(both(u, 2))) for u in ur]
        between(0)
        a_rb = [jnp.where(incl_m, gb[u][C:], 0.0) for u in ur]
        a_kr = [jnp.concatenate([jnp.where(strict_m, gk[u][:C], 0.0),
                                 jnp.where(incl_m, gk[u][C:], 0.0)], axis=0) for u in ur]
        yy = [jnp.where(strict_m, -gb[u][:C], 0.0) for u in ur]
        x = list(yy)
        p = [dot(yy[u], bd(yy[u])) for u in ur]
        yy = p
        between(1)
        for step in range(1, 6):
            last = step == 5
            p = [dot(x[u] if last else jnp.concatenate([yy[u], x[u]], axis=0), bd(yy[u])) for u in ur]
            x = [x[u] + yy[u] + (p[u] if last else p[u][C:]) for u in ur]
            if not last:
                yy = [p[u][:C] for u in ur]
            between(1 + step)
        avr = [dot(a_kr[u], vbd[u]) for u in ur]
        between(7)
        tz = []
        for u in ur:
            av = avr[u][:C]
            zz = jnp.concatenate([kt[u], av], axis=1)
            tz.append(zz + dot(x[u], jnp.concatenate([bd(kt[u]), bd(av)], axis=1)))
        cor = [dot(a_rb[u], jnp.concatenate([bd(tz[u][:, :Q]), bd(tz[u][:, Q:])], axis=1)) for u in ur]
        for u in ur:
            rr = rt[u] - cor[u][:, :Q]
            yl = avr[u][C:] - cor[u][:, Q:]
            for z in range(2):
                zl = slice(z * LANES, (z + 1) * LANES)
                rz = rws[u][z]
                y_s[z, rz, :] = yl[:, zl]
                rr_s[z, rz, :] = rr[:, zl]
                far = C - 1 if z == 0 else 0
                pt = jnp.exp2(cum_s[z, rz, :][far:far + 1, :])
                ends = jnp.concatenate([parts[u][z][3] * pt, parts[u][z][2] * pt], axis=0)
                rhs = jnp.concatenate(
                    [jnp.concatenate([-tz[u][:, zl], -tz[u][:, Q + z * LANES:Q + (z + 1) * LANES]], axis=1),
                     jnp.concatenate([jnp.zeros((C, LANES), F32), v_s[rz, :]], axis=1)], axis=0)
                mn = _bdot(ends.T, rhs)
                m_s[z, cs[u][z]] = jnp.where(eye, jnp.broadcast_to(pt, (LANES, LANES)), 0.0) \
                    + jnp.where(same_head, mn[:, :LANES], 0.0)
                n_s[z, cs[u][z]] = jnp.where(same_head, mn[:, LANES:], 0.0)

    def carry_step(g, u, hs):
        for z in range(2):
            c = chunk_of(g, u, z)
            rows = rows_of(c)
            y_s[z, rows, :] = y_s[z, rows, :] + dot(rr_s[z, rows, :], hs[z])
            hs[z] = (dot(m_s[z, c], hs[z]) + n_s[z, c]).astype(BF16)

    def phase2(g, hs):
        hs = list(hs)
        for u in range(U):
            carry_step(g, u, hs)
        return tuple(hs)

    def lagged(g, hs):
        hs = list(hs)
        steps_per_gap = -(-U // 8)

        def between(k):
            for u in range(k * steps_per_gap, min(U, (k + 1) * steps_per_gap)):
                carry_step(g - 1, u, hs)

        phase1(g, between)
        return tuple(hs)

    ngroup = nchunk // U
    h0 = jnp.zeros((LANES, LANES), BF16)
    phase1(0)
    hs = lax.fori_loop(1, ngroup, lagged, (h0, h0))
    phase2(ngroup - 1, hs)

    y = y_s[0] + y_s[1]
    inv_n = 1.0 / HEAD_DIM
    mu_y = _split_dot(y, seg_ones) * inv_n
    yc = y - mu_y
    var_y = _split_dot(yc * yc, seg_ones) * inv_n
    yn = yc * lax.rsqrt(var_y + RWKV_GN_EPS) * gng_ref[...] + gnb_ref[...]
    kd_sum = k_s[...] * (2.0 + (al_s[0] + al_s[1] - 2.0) * ka_ref[...])
    bonus = _split_dot(r_s[...] * kd_sum * rk_ref[...], seg_ones) * v_s[...]
    o_ref[0] = ((yn + bonus) * g).astype(o_ref.dtype)


def _rwkv_branch(p, w0, w_up, a0, a_up, g_up, k_k, k_a, r_k, gn_g, gn_b, out_dtype):
    B, S, _ = p.shape
    npair = RWKV_DIM // LANES
    zpad = jnp.zeros((2, LANES - DECAY_LORA, RWKV_DIM), F32)
    wup_p = jnp.concatenate([w_up, zpad], axis=1).astype(BF16)
    aup_p = jnp.concatenate([zpad, a_up], axis=1).astype(BF16)
    row = lambda a: a.reshape(1, RWKV_DIM)
    lora_blk = 3 * RWKV_DIM // (2 * LANES)

    def pspec(off):
        return pl.BlockSpec((1, S, LANES), lambda b, j: (b, 0, off + j))

    def vspec(rows, off=0):
        return pl.BlockSpec((rows, LANES), lambda b, j: (0, off + j))

    scratch = [pltpu.VMEM((2, S, LANES), F32) for _ in range(4)] + [
        pltpu.VMEM((S, LANES), F32) for _ in range(4)] + [pltpu.VMEM((2, S, LANES), F32),
        pltpu.VMEM((2, S // RWKV_CHUNK, LANES, LANES), F32),
        pltpu.VMEM((2, S // RWKV_CHUNK, LANES, LANES), F32)]
    return pl.pallas_call(
        functools.partial(_rwkv_kernel, seq=S),
        grid=(B, npair),
        in_specs=[pspec(0), pspec(npair), pspec(2 * npair),
                  pl.BlockSpec((1, S, 2 * LANES), lambda b, j: (b, 0, lora_blk)),
                  vspec(2),
                  pl.BlockSpec((2, LANES, LANES), lambda b, j: (0, 0, j)),
                  vspec(2),
                  pl.BlockSpec((2, LANES, LANES), lambda b, j: (0, 0, j)),
                  vspec(GATE_LORA), vspec(1), vspec(1), vspec(1), vspec(1), vspec(1)],
        out_specs=pl.BlockSpec((1, S, LANES), lambda b, j: (b, 0, j)),
        out_shape=jax.ShapeDtypeStruct((B, S, RWKV_DIM), out_dtype),
        scratch_shapes=scratch,
        compiler_params=_cparams("arbitrary", "arbitrary"),
        name="rwkv7_scan",
    )(p, p, p, p, w0, wup_p, a0, aup_p, g_up.astype(BF16),
      row(k_k), row(k_a), row(r_k), row(gn_g), row(gn_b))


def _win_kernel(q_ref, k_ref, v_ref, biast_ref, sink_ref, o_ref, k4_s, vt_s, *, seq):
    S = seq
    G = WIN_GROUP
    NB = S // BLOCK
    GW = G * HEAD_DIM
    hk = pl.program_id(1)
    odd = (hk % 2) == 1
    lane = lax.broadcasted_iota(jnp.int32, (1, LANES), 1)

    def own_head_twice(full):
        mine_first = jnp.where(odd, pltpu.roll(full, HEAD_DIM, 1), full)
        return jnp.where(lane < HEAD_DIM, mine_first, pltpu.roll(mine_first, HEAD_DIM, 1))

    k2 = own_head_twice(k_ref[0].astype(F32)).astype(BF16)
    v2 = own_head_twice(v_ref[0].astype(F32))
    zero_k = jnp.zeros((BLOCK, GW), BF16)
    zero_v = jnp.zeros((HEAD_DIM, BLOCK), BF16)
    k4_s[0] = zero_k
    k4_s[NB + 1] = zero_k
    vt_s[0] = zero_v
    vt_s[NB + 1] = zero_v
    for blk in range(NB):
        rows = slice(blk * BLOCK, (blk + 1) * BLOCK)
        k4_s[blk + 1] = jnp.concatenate([k2[rows], k2[rows]], axis=1)
        vt_s[blk + 1] = v2[rows].T[:HEAD_DIM].astype(BF16)
    log2e = math.log2(math.e)
    bias_t = jnp.concatenate([biast_ref[gi] for gi in range(G)], axis=1) * log2e
    krow = lax.broadcasted_iota(jnp.int32, (3 * BLOCK, G * BLOCK), 0)
    qcol = lax.broadcasted_iota(jnp.int32, (3 * BLOCK, G * BLOCK), 1) % BLOCK
    bias_t = jnp.where(jnp.abs(krow - BLOCK - qcol) <= WINDOW, bias_t, NEG_INF)
    cgrp = lax.broadcasted_iota(jnp.int32, (1, G * BLOCK), 1) // BLOCK
    sink_row = jnp.zeros((1, G * BLOCK), F32)
    for gi in range(G):
        sink_row = jnp.where(cgrp == gi, sink_ref[hk * G + gi] * log2e, sink_row)
    lane_q = lax.broadcasted_iota(jnp.int32, (1, GW), 1) // HEAD_DIM
    U = min(WIN_UNROLL, NB)

    def body(it, carry):
        ns = [it * U + u for u in range(U)]
        sts = []
        for n in ns:
            qb = q_ref[0, pl.ds(pl.multiple_of(n * BLOCK, BLOCK), BLOCK), :].astype(F32) \
                * (log2e * HEAD_DIM ** -0.5)
            qm = jnp.concatenate([jnp.where(lane_q == gi, qb, 0.0) for gi in range(G)], axis=0).astype(BF16)
            kband = jnp.concatenate([k4_s[n], k4_s[n + 1], k4_s[n + 2]], axis=0)
            sts.append(lax.dot_general(kband, qm, (((1,), (1,)), ((), ())), preferred_element_type=F32))
        es, dens = [], []
        for u, n in enumerate(ns):
            st = sts[u] + bias_t
            st = jnp.concatenate([st[:BLOCK] + jnp.where(n == 0, NEG_INF, 0.0), st[BLOCK:2 * BLOCK],
                                  st[2 * BLOCK:] + jnp.where(n == NB - 1, NEG_INF, 0.0)], axis=0)
            m = jnp.maximum(jnp.max(st, axis=0, keepdims=True), sink_row)
            e = jnp.exp2(st - m)
            es.append(e.astype(BF16))
            dens.append(jnp.sum(e, axis=0, keepdims=True) + jnp.exp2(sink_row - m))
        for u, n in enumerate(ns):
            vband = jnp.concatenate([vt_s[n], vt_s[n + 1], vt_s[n + 2]], axis=1)
            acc = jnp.dot(vband, es[u], preferred_element_type=F32) / dens[u]
            o_t = jnp.concatenate([acc[:, gi * BLOCK:(gi + 1) * BLOCK] for gi in range(G)], axis=0)
            o_ref[0, pl.ds(pl.multiple_of(n * BLOCK, BLOCK), BLOCK), :] = o_t.T.astype(o_ref.dtype)
        return carry

    lax.fori_loop(0, NB // U, body, 0)


def _window_branch(p, bias_tiles_t, sink, out_dtype):
    B, S, _ = p.shape
    gw = WIN_GROUP * HEAD_DIM
    nb = S // BLOCK
    assert nb % min(WIN_UNROLL, nb) == 0
    qoff = OFF_WIN // gw
    koff = (OFF_WIN + WIN_Q_COLS) // LANES
    voff = (OFF_WIN + WIN_Q_COLS + WIN_KV_COLS) // LANES
    return pl.pallas_call(
        functools.partial(_win_kernel, seq=S),
        grid=(B, WIN_KV_HEADS),
        in_specs=[pl.BlockSpec((1, S, gw), lambda b, h: (b, 0, qoff + h)),
                  pl.BlockSpec((1, S, LANES), lambda b, h: (b, 0, koff + h // 2)),
                  pl.BlockSpec((1, S, LANES), lambda b, h: (b, 0, voff + h // 2)),
                  pl.BlockSpec((WIN_GROUP, 3 * BLOCK, BLOCK), lambda b, h: (h, 0, 0)),
                  pl.BlockSpec(memory_space=pltpu.SMEM)],
        out_specs=pl.BlockSpec((1, S, gw), lambda b, h: (b, 0, h)),
        out_shape=jax.ShapeDtypeStruct((B, S, WIN_Q_COLS), out_dtype),
        scratch_shapes=[pltpu.VMEM((nb + 2, BLOCK, gw), BF16),
                        pltpu.VMEM((nb + 2, HEAD_DIM, BLOCK), BF16)],
        compiler_params=_cparams("arbitrary", "arbitrary"),
        name="window_attn",
    )(p, p, p, bias_tiles_t, sink)


def _diff_kernel(q_ref, k_ref, v_ref, bandt_ref, tab_ref, lam_ref, g_ref, o_ref, ka_s, vt_s, *,
                 seq, lambda_init):
    S = seq
    NB = S // BLOCK
    CB = min(DIFF_CHUNK_BLOCKS, NB)
    h = pl.program_id(1)
    it = pl.program_id(2)
    halves = q_ref.shape[1] // BLOCK
    log2e = math.log2(math.e)
    lane = lax.broadcasted_iota(jnp.int32, (1, LANES), 1)
    c0 = lane < HEAD_DIM

    @pl.when(it == 0)
    def _():
        jblk = lax.broadcasted_iota(jnp.int32, (S, LANES), 0) // BLOCK
        lj = lax.broadcasted_iota(jnp.int32, (S, LANES), 1)
        ka_s[:, :LANES] = k_ref[0].astype(BF16)
        ka_s[:, LANES:] = jnp.where((lj < 2 * NB) & (lj % NB == jblk), 1.0, 0.0).astype(BF16)
        for blk in range(NB):
            vt_s[blk] = v_ref[0, blk * BLOCK:(blk + 1) * BLOCK, :].astype(F32).T.astype(BF16)

    far_l = tab_ref[WIN_Q_HEADS + h, NUM_BUCKETS // 2 - 1] * log2e
    far_r = tab_ref[WIN_Q_HEADS + h, NUM_BUCKETS - 1] * log2e
    band_t = bandt_ref[0] * log2e
    lam = lam_ref[...].astype(F32)
    lam_full = (jnp.exp(jnp.sum(lam[0:1] * lam[1:2], axis=-1, keepdims=True))
                - jnp.exp(jnp.sum(lam[2:3] * lam[3:4], axis=-1, keepdims=True)) + lambda_init)
    ibs = [it * halves + hf for hf in range(halves)]
    zblk = jnp.zeros((BLOCK, BLOCK), F32)
    qq, bands = [], []
    for hf in range(halves):
        q = q_ref[0, hf * BLOCK:(hf + 1) * BLOCK, :].astype(F32) * (log2e * HEAD_DIM ** -0.5)
        kb = lane % NB
        val = jnp.where(kb < ibs[hf] - 1, far_l, jnp.where(kb > ibs[hf] + 1, far_r, 0.0))
        val = jnp.where(lane < 2 * NB, val, 0.0)
        hi = val.astype(BF16)
        lo = (val - hi.astype(F32)).astype(BF16)
        bcols = jnp.broadcast_to(jnp.where(lane < NB, hi, lo), (BLOCK, LANES))
        qq.append(jnp.concatenate(
            [jnp.concatenate([jnp.where(c0, q, 0.0).astype(BF16), bcols], axis=1),
             jnp.concatenate([jnp.where(c0, 0.0, q).astype(BF16), bcols], axis=1)], axis=0))
        tiles = [jnp.where(ibs[hf] > 0, band_t[:BLOCK], 0.0), band_t[BLOCK:2 * BLOCK],
                 jnp.where(ibs[hf] < NB - 1, band_t[2 * BLOCK:], 0.0)]
        col = jnp.concatenate([zblk] * hf + tiles + [zblk] * (CB - 3 - hf), axis=0)
        bands.append(jnp.concatenate([col, col], axis=1))

    m_run, l_run, acc = [None] * halves, [None] * halves, [None] * halves
    for r in range(NB // CB):
        kbs = [lax.rem(it * halves - 1 + CB * r + u + NB, NB) for u in range(CB)]
        kc = jnp.concatenate([ka_s[pl.ds(pl.multiple_of(kb * BLOCK, BLOCK), BLOCK), :] for kb in kbs], axis=0)
        vct = jnp.concatenate([vt_s[kb] for kb in kbs], axis=1)
        s = [lax.dot_general(kc, qq[hf], (((1,), (1,)), ((), ())), preferred_element_type=F32)
             for hf in range(halves)]
        if r == 0:
            s = [s[hf] + bands[hf] for hf in range(halves)]
        m_loc = [jnp.max(s[hf], axis=0, keepdims=True) for hf in range(halves)]
        for hf in range(halves):
            if r == 0:
                m_new = m_loc[hf]
                e = jnp.exp2(s[hf] - m_new)
                l_run[hf] = jnp.sum(e, axis=0, keepdims=True)
                acc[hf] = jnp.dot(vct, e.astype(BF16), preferred_element_type=F32)
            else:
                m_new = jnp.maximum(m_run[hf], m_loc[hf])
                alpha = jnp.exp2(m_run[hf] - m_new)
                e = jnp.exp2(s[hf] - m_new)
                l_run[hf] = alpha * l_run[hf] + jnp.sum(e, axis=0, keepdims=True)
                acc[hf] = alpha * acc[hf] + jnp.dot(vct, e.astype(BF16), preferred_element_type=F32)
            m_run[hf] = m_new
    for hf in range(halves):
        rinv = 1.0 / l_run[hf]
        o_t = acc[hf][:, :BLOCK] * rinv[:, :BLOCK] - acc[hf][:, BLOCK:] * (lam_full * rinv[:, BLOCK:])
        o = o_t.T
        o = o * lax.rsqrt(jnp.mean(o * o, axis=-1, keepdims=True) + 1e-5) * g_ref[...] * (1.0 - lambda_init)
        o_ref[0, hf * BLOCK:(hf + 1) * BLOCK, :] = o.astype(o_ref.dtype)


def _diff_branch(p, band_tiles, rel_bias_t, lam, subln_g, lambda_init, out_dtype):
    B, S, _ = p.shape
    qoff = OFF_DIFF // LANES
    koff = (OFF_DIFF + DIFF_QK_COLS) // LANES
    voff = (OFF_DIFF + 2 * DIFF_QK_COLS) // LANES
    nb = S // BLOCK
    cb = min(DIFF_CHUNK_BLOCKS, nb)
    tq = DIFF_Q_TILE
    while tq > (cb - 2) * BLOCK or S % tq:
        tq //= 2
    assert 2 * nb <= LANES and tq >= BLOCK and nb % cb == 0 and nb >= 4
    return pl.pallas_call(
        functools.partial(_diff_kernel, seq=S, lambda_init=lambda_init),
        grid=(B, DIFF_HEADS, S // tq),
        in_specs=[pl.BlockSpec((1, tq, LANES), lambda b, h, i: (b, i, qoff + h)),
                  pl.BlockSpec((1, S, LANES), lambda b, h, i: (b, 0, koff + h)),
                  pl.BlockSpec((1, S, LANES), lambda b, h, i: (b, 0, voff + h)),
                  pl.BlockSpec((1, 3 * BLOCK, BLOCK), lambda b, h, i: (WIN_Q_HEADS + h, 0, 0)),
                  pl.BlockSpec(memory_space=pltpu.SMEM),
                  pl.BlockSpec((4, HEAD_DIM), lambda b, h, i: (0, 0)),
                  pl.BlockSpec((1, DIFF_V_DIM), lambda b, h, i: (0, 0))],
        out_specs=pl.BlockSpec((1, tq, LANES), lambda b, h, i: (b, i, h)),
        out_shape=jax.ShapeDtypeStruct((B, S, DIFF_V_COLS), out_dtype),
        scratch_shapes=[pltpu.VMEM((S, 2 * LANES), BF16),
                        pltpu.VMEM((S // BLOCK, DIFF_V_DIM, BLOCK), BF16)],
        compiler_params=_cparams("arbitrary", "arbitrary", "arbitrary"),
        name="diff_attn",
    )(p, p, p, jnp.swapaxes(band_tiles, 1, 2), rel_bias_t, lam, subln_g.reshape(1, DIFF_V_DIM))


def _mem_kernel(q_ref, mem_ref, wkv_ref, o_ref):
    kv = _bdot(mem_ref[0], wkv_ref[...])
    kmat = kv[:, :MEM_WIDTH]
    vmat = kv[:, MEM_WIDTH:]
    q = q_ref[0].astype(F32) * (HEAD_DIM ** -0.5)
    lane = lax.broadcasted_iota(jnp.int32, (1, MEM_WIDTH), 1) // HEAD_DIM
    out = jnp.zeros(q.shape, F32)
    for hh in range(MEM_HEADS):
        s = _bdot_nt(jnp.where(lane == hh, q, 0.0), kmat)
        m = jnp.max(s, axis=-1, keepdims=True)
        e = jnp.exp(s - m)
        a = e / jnp.sum(e, axis=-1, keepdims=True)
        out = jnp.where(lane == hh, _bdot(a, vmat), out)
    o_ref[0] = out.astype(o_ref.dtype)


def _memory_branch(p, mem, w_kv, out_dtype):
    B, S, _ = p.shape
    nm, d = mem.shape[1], mem.shape[2]
    return pl.pallas_call(
        _mem_kernel,
        grid=(B,),
        in_specs=[pl.BlockSpec((1, S, MEM_WIDTH), lambda b: (b, 0, OFF_MEM // MEM_WIDTH)),
                  pl.BlockSpec((1, nm, d), lambda b: (b, 0, 0)),
                  pl.BlockSpec((d, 2 * MEM_WIDTH), lambda b: (0, 0))],
        out_specs=pl.BlockSpec((1, S, MEM_WIDTH), lambda b: (b, 0, 0)),
        out_shape=jax.ShapeDtypeStruct((B, S, MEM_WIDTH), out_dtype),
        compiler_params=_cparams("arbitrary"),
        name="memory_attn",
    )(p, mem, w_kv)


def _merge_kernel(x_ref, yr_ref, yw_ref, yd_ref, ym_ref, pg_ref, wb_ref, wo_ref, g_ref, b_ref,
                  o_ref, *, d_model, alpha):
    D = d_model
    ys = (yr_ref[...], yw_ref[...], yd_ref[...], ym_ref[...])
    off = 0
    merged = jnp.zeros((x_ref.shape[0], D), F32)
    for bi, yb in enumerate(ys):
        rows = yb.shape[1]
        proj = _bdot(yb, wb_ref[off:off + rows, :])
        merged = merged + _sigmoid(pg_ref[:, bi * D:(bi + 1) * D].astype(F32)) * proj
        off += rows
    z = alpha * x_ref[...] + _bdot(merged, wo_ref[...])
    o_ref[...] = _layer_norm(z, g_ref[...], b_ref[...])


def _merge(x2, y_rwkv, y_win, y_diff, y_mem, p2, w_branch, w_out, ln_g, ln_b, alpha, tm):
    T, D = x2.shape
    gate_blk = OFF_GATE // (N_BRANCHES * D)
    full = lambda a: pl.BlockSpec(a.shape, lambda i: (0, 0))
    tile = lambda w: pl.BlockSpec((tm, w), lambda i: (i, 0))
    return pl.pallas_call(
        functools.partial(_merge_kernel, d_model=D, alpha=alpha),
        grid=(T // tm,),
        in_specs=[tile(D), tile(y_rwkv.shape[1]), tile(y_win.shape[1]), tile(y_diff.shape[1]),
                  tile(y_mem.shape[1]),
                  pl.BlockSpec((tm, N_BRANCHES * D), lambda i: (i, gate_blk)),
                  full(w_branch), full(w_out),
                  pl.BlockSpec((1, D), lambda i: (0, 0)), pl.BlockSpec((1, D), lambda i: (0, 0))],
        out_specs=tile(D),
        out_shape=jax.ShapeDtypeStruct((T, D), F32),
        compiler_params=_cparams("arbitrary"),
        name="merge_out_ln",
    )(x2, y_rwkv, y_win, y_diff, y_mem, p2, w_branch, w_out, ln_g.reshape(1, D), ln_b.reshape(1, D))


def _route_kernel(x_ref, wr_ref, xe_ref, slot_ref, gate_ref, xb_s, lg_s, *, seq, cap):
    S = seq
    E = N_EXPERTS
    TR = min(S, 512)
    wr = wr_ref[...]
    w_hi = wr.astype(BF16)
    w_md = (wr - w_hi.astype(F32)).astype(BF16)
    dot = lambda a, b: jnp.dot(a, b, preferred_element_type=F32)

    def logit_tile(t, carry):
        rows = pl.ds(pl.multiple_of(t * TR, TR), TR)
        x = x_ref[0, rows, :]
        x_hi = x.astype(BF16)
        x_r1 = x - x_hi.astype(F32)
        x_md = x_r1.astype(BF16)
        x_lo = (x_r1 - x_md.astype(F32)).astype(BF16)
        xb_s[rows, :] = x_hi
        lg_s[rows, :] = (dot(x_hi, w_hi) + (dot(x_hi, w_md) + dot(x_md, w_hi))
                         + (dot(x_md, w_md) + dot(x_lo, w_hi)))
        return carry

    lax.fori_loop(0, S // TR, logit_tile, 0)
    col = lax.broadcasted_iota(jnp.int32, (1, LANES), 1)
    logits = jnp.where(col < E, lg_s[...], NEG_INF)
    mx = jnp.max(logits, axis=-1, keepdims=True)
    ex = jnp.exp(logits - mx)
    aff = ex / jnp.sum(ex, axis=-1, keepdims=True)
    aff_t = aff.T[0:E, :]
    bits = pltpu.bitcast(aff_t, jnp.int32)

    def bis(_, carry):
        lo_b, step = carry
        cand = lo_b + step
        cnt = jnp.sum(jnp.where(bits >= cand, 1.0, 0.0), axis=-1, keepdims=True)
        return jnp.where(cnt >= cap, cand, lo_b), step // 2

    lo0 = jnp.zeros((E, 1), jnp.int32)
    thr, _ = lax.fori_loop(0, 31, bis, (lo0, jnp.full((E, 1), 1 << 30, jnp.int32)))
    above = bits > thr
    tie = bits == thr
    n_above = jnp.sum(jnp.where(above, 1.0, 0.0), axis=-1, keepdims=True)
    PB = min(S, 256)
    tri = (lax.broadcasted_iota(jnp.int32, (PB, PB), 0)
           < lax.broadcasted_iota(jnp.int32, (PB, PB), 1)).astype(BF16)

    def prefix_count(mask_f):
        parts, run = [], jnp.zeros((E, 1), F32)
        for blk in range(S // PB):
            seg = mask_f[:, blk * PB:(blk + 1) * PB]
            parts.append(jnp.dot(seg.astype(BF16), tri, preferred_element_type=F32) + run)
            run = run + jnp.sum(seg, axis=-1, keepdims=True)
        return jnp.concatenate(parts, axis=1)

    tie_rank = prefix_count(jnp.where(tie, 1.0, 0.0))
    sel = above | (tie & (tie_rank < cap - n_above))
    pos = prefix_count(jnp.where(sel, 1.0, 0.0))
    slot = jnp.where(sel, pos, -1.0)
    slot_pad = jnp.concatenate([slot, jnp.full((LANES - E, S), -1.0, F32)], axis=0)
    slot_ref[0] = slot_pad.T
    ci = lax.broadcasted_iota(jnp.int32, (cap, S), 0).astype(F32)
    x_hi = xb_s[...]
    for e0 in range(0, E, GATHER_STACK):
        es = range(e0, min(E, e0 + GATHER_STACK))
        onehots = [slot[e:e + 1, :] == ci for e in es]
        sel = jnp.concatenate([jnp.where(oh, 1.0, 0.0).astype(BF16) for oh in onehots], axis=0)
        xg = jnp.dot(sel, x_hi, preferred_element_type=F32).astype(xe_ref.dtype)
        for i, e in enumerate(es):
            xe_ref[e, 0] = xg[i * cap:(i + 1) * cap]
            gsel = jnp.sum(jnp.where(onehots[i], aff_t[e:e + 1, :], 0.0), axis=-1, keepdims=True)
            gate_ref[e, 0] = jnp.broadcast_to(gsel, (cap, LANES))


def _route(x3, router_pad, cap):
    B, S, D = x3.shape
    E = N_EXPERTS
    return pl.pallas_call(
        functools.partial(_route_kernel, seq=S, cap=cap),
        grid=(B,),
        in_specs=[pl.BlockSpec((1, S, D), lambda b: (b, 0, 0)),
                  pl.BlockSpec((D, LANES), lambda b: (0, 0))],
        out_specs=[pl.BlockSpec((E, 1, cap, D), lambda b: (0, b, 0, 0)),
                   pl.BlockSpec((1, S, LANES), lambda b: (b, 0, 0)),
                   pl.BlockSpec((E, 1, cap, LANES), lambda b: (0, b, 0, 0))],
        out_shape=[jax.ShapeDtypeStruct((E, B, cap, D), BF16),
                   jax.ShapeDtypeStruct((B, S, LANES), F32),
                   jax.ShapeDtypeStruct((E, B, cap, LANES), F32)],
        scratch_shapes=[pltpu.VMEM((S, D), BF16), pltpu.VMEM((S, LANES), F32)],
        compiler_params=_cparams("arbitrary"),
        name="moe_route_gather",
    )(x3, router_pad)


def _ffn_kernel(xe_ref, g_ref, wg_ref, wu_ref, wd_ref, ye_ref, acc_s):
    f = pl.program_id(1)
    nb, cap, D = xe_ref.shape[1], xe_ref.shape[2], xe_ref.shape[3]

    @pl.when(f == 0)
    def _():
        acc_s[...] = jnp.zeros(acc_s.shape, F32)

    xe = xe_ref[0].reshape(nb * cap, D)
    hg = jnp.dot(xe, wg_ref[0, 0].astype(BF16), preferred_element_type=F32)
    hu = jnp.dot(xe, wu_ref[0, 0].astype(BF16), preferred_element_type=F32)
    hid = (hg * _sigmoid(hg)) * hu
    acc_s[...] = acc_s[...] + _bdot(hid, wd_ref[0, 0])

    @pl.when(f == pl.num_programs(1) - 1)
    def _():
        gate = g_ref[0].reshape(nb * cap, LANES)[:, 0:1]
        ye_ref[0] = (acc_s[...] * gate).reshape(nb, cap, D).astype(ye_ref.dtype)


def _expert_ffn(xe, gate, w_gate, w_up, w_down, layer, tf):
    E, B, cap, D = xe.shape
    F = w_gate.shape[-1]
    return pl.pallas_call(
        _ffn_kernel,
        grid=(E, F // tf),
        in_specs=[pl.BlockSpec((1, B, cap, D), lambda e, f: (e, 0, 0, 0)),
                  pl.BlockSpec((1, B, cap, LANES), lambda e, f: (e, 0, 0, 0)),
                  pl.BlockSpec((1, 1, D, tf), lambda e, f: (layer, e, 0, f)),
                  pl.BlockSpec((1, 1, D, tf), lambda e, f: (layer, e, 0, f)),
                  pl.BlockSpec((1, 1, tf, D), lambda e, f: (layer, e, f, 0))],
        out_specs=pl.BlockSpec((1, B, cap, D), lambda e, f: (e, 0, 0, 0)),
        out_shape=jax.ShapeDtypeStruct((E, B, cap, D), BF16),
        scratch_shapes=[pltpu.VMEM((B * cap, D), F32)],
        compiler_params=_cparams("arbitrary", "arbitrary"),
        name="moe_expert_ffn",
    )(xe, gate, w_gate, w_up, w_down)


def _scatter_kernel(x_ref, slot_ref, ye_ref, g_ref, b_ref, o_ref, ob_ref, *, cap, alpha):
    ts = x_ref.shape[1]
    slot = slot_ref[0]
    ci = lax.broadcasted_iota(jnp.int32, (ts, cap), 1).astype(F32)
    sel_all = jnp.concatenate([jnp.where(slot[:, e:e + 1] == ci, 1.0, 0.0).astype(BF16)
                               for e in range(N_EXPERTS)], axis=1)
    ye_all = ye_ref[...].reshape(N_EXPERTS * cap, x_ref.shape[2])
    acc = jnp.dot(sel_all, ye_all, preferred_element_type=F32)
    out = _layer_norm(alpha * x_ref[0] + acc, g_ref[...], b_ref[...])
    o_ref[0] = out
    ob_ref[0] = out.astype(ob_ref.dtype)


def _scatter_ln(x3, slot, ye, ln_g, ln_b, alpha, ts):
    B, S, D = x3.shape
    E, _, cap, _ = ye.shape
    return pl.pallas_call(
        functools.partial(_scatter_kernel, cap=cap, alpha=alpha),
        grid=(B, S // ts),
        in_specs=[pl.BlockSpec((1, ts, D), lambda b, i: (b, i, 0)),
                  pl.BlockSpec((1, ts, LANES), lambda b, i: (b, i, 0)),
                  pl.BlockSpec((E, 1, cap, D), lambda b, i: (0, b, 0, 0)),
                  pl.BlockSpec((1, D), lambda b, i: (0, 0)),
                  pl.BlockSpec((1, D), lambda b, i: (0, 0))],
        out_specs=[pl.BlockSpec((1, ts, D), lambda b, i: (b, i, 0)),
                   pl.BlockSpec((1, ts, D), lambda b, i: (b, i, 0))],
        out_shape=[jax.ShapeDtypeStruct((B, S, D), F32), jax.ShapeDtypeStruct((B, S, D), BF16)],
        compiler_params=_cparams("arbitrary", "arbitrary"),
        name="moe_scatter_ln",
    )(x3, slot, ye, ln_g.reshape(1, D), ln_b.reshape(1, D))


def kernel(x, mem, rel_bias, w_in, rwkv_mu, rwkv_w0, rwkv_w_up, rwkv_a0, rwkv_a_up, rwkv_g_up, rwkv_k_k, rwkv_k_a, rwkv_r_k, rwkv_gn_g, rwkv_gn_b, win_sink, diff_lambda, diff_subln_g, mem_w_kv, w_branch, w_out, ln1_g, ln1_b, router, exp_w_gate, exp_w_up, exp_w_down, ln2_g, ln2_b):
    B, S, D = x.shape
    assert D == D_MODEL and w_in.shape[2] == RWKV_COLS + REST_COLS
    depth = w_in.shape[0]
    alpha = (2 * depth) ** 0.25
    cap = CAPACITY_FACTOR * S // N_EXPERTS
    T = B * S
    rel_bias_t = rel_bias.T
    bias_tiles = _bias_tiles(rel_bias_t)
    bias_tiles_t = jnp.swapaxes(bias_tiles, 1, 2)
    act = BF16
    xb = x.astype(BF16)
    for l in range(depth):
        lambda_init = 0.8 - 0.6 * math.exp(-0.3 * l)
        gate0 = w_in.shape[2] - N_BRANCHES * D
        w_rest = jnp.concatenate([w_in[l][:, gate0:], w_in[l][:, RWKV_COLS:gate0]], axis=1).astype(BF16)
        p_rwkv, p2 = _in_proj(xb.reshape(T, D), w_in[l][:, :RWKV_COLS].astype(BF16), w_rest,
                              rwkv_mu[l], S, act)
        p = p2.reshape(B, S, -1)
        y_rwkv = _rwkv_branch(p_rwkv.reshape(B, S, -1), rwkv_w0[l], rwkv_w_up[l], rwkv_a0[l], rwkv_a_up[l],
                              rwkv_g_up[l], rwkv_k_k[l], rwkv_k_a[l], rwkv_r_k[l], rwkv_gn_g[l],
                              rwkv_gn_b[l], act)
        y_win = _window_branch(p, bias_tiles_t, win_sink[l], act)
        y_diff = _diff_branch(p, bias_tiles, rel_bias_t, diff_lambda[l], diff_subln_g[l], lambda_init, act)
        y_mem = _memory_branch(p, mem, mem_w_kv[l].astype(BF16), act)
        x1 = _merge(x.reshape(T, D), y_rwkv.reshape(T, -1), y_win.reshape(T, -1), y_diff.reshape(T, -1),
                    y_mem.reshape(T, -1), p2, w_branch[l].astype(BF16), w_out[l].astype(BF16),
                    ln1_g[l], ln1_b[l], alpha, MERGE_ROWS).reshape(B, S, D)
        router_pad = jnp.concatenate([router[l], jnp.zeros((D, LANES - N_EXPERTS), F32)], axis=1)
        xe, slot, gate = _route(x1, router_pad, cap)
        ye = _expert_ffn(xe, gate, exp_w_gate, exp_w_up, exp_w_down, l, FFN_HIDDEN_TILE)
        x, xb = _scatter_ln(x1, slot, ye, ln2_g[l], ln2_b[l], alpha, SCATTER_ROWS)
    return x
```
